```python
import jax, jax.numpy as jnp
from jax import lax
import numpy as np

D_MODEL = 2048
BATCH = 16
SEQ = 2048
DEPTH = 1
DEC_BATCH = 128
DEC_SEQ = 8
PAST_LEN = 16384
PAGE_SIZE = 128

HEAD_DIM = 64
N_HEADS_ATTN = D_MODEL // HEAD_DIM
N_KV_HEADS = max(N_HEADS_ATTN // 8, 1)
GQA = N_HEADS_ATTN // N_KV_HEADS
D_ATTN = N_HEADS_ATTN * HEAD_DIM
D_KV = N_KV_HEADS * HEAD_DIM
WINDOW = 128
SSM_HEAD_DIM = 64
D_SSM = D_MODEL
N_HEADS_SSM = D_SSM // SSM_HEAD_DIM
N_GROUPS = 4
D_STATE = 128
CONV_K = 4
SSM_CHUNK = 128
D_MIX = D_ATTN + D_SSM
CONV_DIM = D_SSM + 2 * N_GROUPS * D_STATE
SPLIT_SIZES = (D_ATTN, D_KV, D_KV, D_ATTN, D_SSM, CONV_DIM, N_HEADS_SSM)
SPLIT_POINTS = tuple(int(s) for s in np.cumsum(SPLIT_SIZES)[:-1])
IN_DIM = int(sum(SPLIT_SIZES))
EPS = 1e-6
DT_MIN = 1e-3
DT_MAX = 1e-1

kernel_name = 'hymba_swa_sink_ssd_step'


def rmsnorm(x, g):
    xf = x.astype(jnp.float32)
    y = xf * lax.rsqrt(jnp.mean(xf * xf, axis=-1, keepdims=True) + EPS)
    return (y * g.astype(jnp.float32)).astype(x.dtype)


def sink_attend(q, k, v, mask, sink):
    s = jnp.einsum('...qkgd,...skd->...kgqs', q, k).astype(jnp.float32) * (HEAD_DIM ** -0.5)
    s = jnp.where(mask, s, -jnp.inf)
    snk = sink.astype(jnp.float32).reshape(N_KV_HEADS, GQA, 1, 1)
    m = jnp.maximum(jnp.max(s, axis=-1, keepdims=True), snk)
    p = jnp.exp(s - m)
    p = p / (jnp.sum(p, axis=-1, keepdims=True) + jnp.exp(snk - m))
    return jnp.einsum('...kgqs,...skd->...qkgd', p.astype(v.dtype), v)


def swa_prompt(q, k, v, sink):
    n, S = q.shape[:2]
    L = WINDOW
    nb = S // L
    qb = q.reshape(n, nb, L, N_KV_HEADS, GQA, HEAD_DIM)
    kb = k.reshape(n, nb, L, N_KV_HEADS, HEAD_DIM)
    vb = v.reshape(n, nb, L, N_KV_HEADS, HEAD_DIM)
    prev = lambda t: jnp.concatenate([jnp.zeros_like(t[:, :1]), t[:, :-1]], axis=1)
    kk = jnp.concatenate([prev(kb), kb], axis=2)
    vv = jnp.concatenate([prev(vb), vb], axis=2)
    rel = jnp.arange(L)[:, None] + L - jnp.arange(2 * L)[None, :]
    has_key = (jnp.arange(nb)[:, None, None] > 0) | (jnp.arange(2 * L)[None, None, :] >= L)
    mask = ((rel >= 0) & (rel < WINDOW))[None] & has_key
    o = sink_attend(qb, kk, vv, mask[:, None, None], sink)
    return o.reshape(n, S, D_ATTN)


def swa_step(q, k, v, kbuf, vbuf, sink):
    n, T = q.shape[:2]
    W = kbuf.shape[1]
    kk = jnp.concatenate([kbuf, k], axis=1)
    vv = jnp.concatenate([vbuf, v], axis=1)
    rel = jnp.arange(T)[:, None] + W - jnp.arange(W + T)[None, :]
    mask = (rel >= 0) & (rel < WINDOW)
    o = sink_attend(q.reshape(n, T, N_KV_HEADS, GQA, HEAD_DIM), kk, vv, mask, sink)
    return o.reshape(n, T, D_ATTN), kk[:, -W:], vv[:, -W:]


def causal_conv(xpad, w, b):
    C = xpad.shape[-1]
    y = lax.conv_general_dilated(xpad, w[:, None, :].astype(xpad.dtype), (1,), 'VALID',
                                 dimension_numbers=('NWC', 'WIO', 'NWC'), feature_group_count=C)
    return y + b.astype(xpad.dtype)


def ssd(x, dt, A, Bm, Cm, h0, chunk):
    n, T, H, P = x.shape
    G, N = Bm.shape[-2:]
    R = H // G
    L = chunk
    nc = T // L
    f32 = jnp.float32
    x = x.astype(f32).reshape(n, nc, L, G, R, P)
    dt = dt.reshape(n, nc, L, G, R)
    Bm = Bm.astype(f32).reshape(n, nc, L, G, N)
    Cm = Cm.astype(f32).reshape(n, nc, L, G, N)
    a = lax.cumsum(dt * A.reshape(G, R), axis=2)
    xdt = x * dt[..., None]
    causal = jnp.tril(jnp.ones((L, L), bool))[:, :, None, None]
    decay = jnp.exp(jnp.where(causal, a[:, :, :, None] - a[:, :, None, :], -jnp.inf))
    cb = jnp.einsum('bclge,bcsge->bclsg', Cm, Bm)
    y_intra = jnp.einsum('bclsg,bclsgr,bcsgrp->bclgrp', cb, decay, xdt)
    decay_end = jnp.exp(a[:, :, -1:] - a)
    s_chunk = jnp.einsum('bclge,bclgr,bclgrp->bcgrpe', Bm, decay_end, xdt)
    chunk_decay = jnp.exp(a[:, :, -1])

    def step(h, inp):
        d, s = inp
        return d[..., None, None] * h + s, h

    h_last, h_prev = lax.scan(step, h0.astype(f32).reshape(n, G, R, P, N),
                              (jnp.moveaxis(chunk_decay, 1, 0), jnp.moveaxis(s_chunk, 1, 0)))
    h_prev = jnp.moveaxis(h_prev, 0, 1)
    y_inter = jnp.einsum('bclge,bcgrpe,bclgr->bclgrp', Cm, h_prev, jnp.exp(a))
    return (y_intra + y_inter).reshape(n, T, H, P), h_last.reshape(n, H, P, N)


def layer(x, kbuf, vbuf, conv_buf, h0, norm_pre, w_in, attn_sink, attn_norm, conv_w, conv_b,
          dt_bias, a_log, d_skip, ssm_norm, w_out, norm_post):
    n, T, _ = x.shape
    f32 = jnp.float32
    h = rmsnorm(x, norm_pre)
    u = jnp.einsum('btd,de->bte', h, w_in)
    q, k, v, g_a, z, xbc, dt = jnp.split(u, SPLIT_POINTS, axis=-1)
    k = k.reshape(n, T, N_KV_HEADS, HEAD_DIM)
    v = v.reshape(n, T, N_KV_HEADS, HEAD_DIM)
    if kbuf is None:
        o_a = swa_prompt(q, k, v, attn_sink)
        new_k, new_v = k[:, -WINDOW:], v[:, -WINDOW:]
    else:
        o_a, new_k, new_v = swa_step(q, k, v, kbuf, vbuf, attn_sink)
    attn_out = rmsnorm(o_a * jax.nn.silu(g_a), attn_norm)
    xpad = jnp.concatenate([conv_buf.astype(xbc.dtype), xbc], axis=1)
    new_conv = xpad[:, -(CONV_K - 1):]
    xbc_c = jax.nn.silu(causal_conv(xpad, conv_w, conv_b))
    xs, Bm, Cm = jnp.split(xbc_c, (D_SSM, D_SSM + N_GROUPS * D_STATE), axis=-1)
    xs = xs.reshape(n, T, N_HEADS_SSM, SSM_HEAD_DIM)
    dt = jax.nn.softplus(dt.astype(f32) + dt_bias.astype(f32))
    A = -jnp.exp(a_log.astype(f32))
    chunk = SSM_CHUNK if T % SSM_CHUNK == 0 else T
    y, h_new = ssd(xs, dt, A, Bm.reshape(n, T, N_GROUPS, D_STATE), Cm.reshape(n, T, N_GROUPS, D_STATE), h0, chunk)
    y = y + d_skip.astype(f32)[:, None] * xs.astype(f32)
    yz = (y.reshape(n, T, D_SSM) * jax.nn.silu(z.astype(f32))).reshape(n, T, N_GROUPS, D_SSM // N_GROUPS)
    ssm_out = rmsnorm(yz, ssm_norm.reshape(N_GROUPS, D_SSM // N_GROUPS)).reshape(n, T, D_SSM).astype(x.dtype)
    mix = jnp.concatenate([attn_out, ssm_out], axis=-1)
    out = jnp.einsum('bte,ed->btd', mix, w_out)
    return x + rmsnorm(out, norm_post), new_k, new_v, new_conv, h_new.astype(h0.dtype)


def setup_inputs(seed: int = 0) -> dict:
    key = jax.random.key(seed)
    ks = jax.random.split(key, 20)
    f32 = jnp.float32
    W_CACHE = min(WINDOW, PAST_LEN)
    nrm = lambda k, s: jax.random.normal(k, s, f32)
    dt0 = jnp.exp(jax.random.uniform(ks[14], (DEPTH, N_HEADS_SSM), f32, np.log(DT_MIN), np.log(DT_MAX)))
    return {
        'x_prompt': nrm(ks[0], (BATCH, SEQ, D_MODEL)),
        'x_sample': nrm(ks[1], (DEC_BATCH, DEC_SEQ, D_MODEL)),
        'cache_k': nrm(ks[2], (DEPTH, DEC_BATCH, W_CACHE, N_KV_HEADS, HEAD_DIM)),
        'cache_v': nrm(ks[3], (DEPTH, DEC_BATCH, W_CACHE, N_KV_HEADS, HEAD_DIM)),
        'state_conv': nrm(ks[4], (DEPTH, DEC_BATCH, CONV_K - 1, CONV_DIM)),
        'state_ssm': 0.5 * nrm(ks[5], (DEPTH, DEC_BATCH, N_HEADS_SSM, SSM_HEAD_DIM, D_STATE)),
        'norm_pre': 1.0 + 0.05 * nrm(ks[6], (DEPTH, D_MODEL)),
        'w_in': nrm(ks[7], (DEPTH, D_MODEL, IN_DIM)) * D_MODEL ** -0.5,
        'attn_sink': nrm(ks[8], (DEPTH, N_HEADS_ATTN)),
        'attn_norm': 1.0 + 0.05 * nrm(ks[9], (DEPTH, D_ATTN)),
        'conv_w': nrm(ks[10], (DEPTH, CONV_K, CONV_DIM)) * CONV_K ** -0.5,
        'conv_b': 0.02 * nrm(ks[11], (DEPTH, CONV_DIM)),
        'dt_bias': dt0 + jnp.log(-jnp.expm1(-dt0)),
        'a_log': jnp.log(jax.random.uniform(ks[12], (DEPTH, N_HEADS_SSM), f32, 1.0, 16.0)),
        'd_skip': 1.0 + 0.1 * nrm(ks[13], (DEPTH, N_HEADS_SSM)),
        'ssm_norm': 1.0 + 0.05 * nrm(ks[15], (DEPTH, D_SSM)),
        'w_out': nrm(ks[16], (DEPTH, D_MIX, D_MODEL)) * D_MIX ** -0.5,
        'norm_post': 1.0 + 0.05 * nrm(ks[17], (DEPTH, D_MODEL)),
    }


def reference(x_prompt, x_sample, cache_k, cache_v, state_conv, state_ssm, norm_pre, w_in, attn_sink,
              attn_norm, conv_w, conv_b, dt_bias, a_log, d_skip, ssm_norm, w_out, norm_post):
    yp, ys = x_prompt, x_sample
    kp_l, vp_l, cp_l, hp_l, ks_l, vs_l, cs_l, hs_l = [], [], [], [], [], [], [], []
    for l in range(DEPTH):
        params = (norm_pre[l], w_in[l], attn_sink[l], attn_norm[l], conv_w[l], conv_b[l],
                  dt_bias[l], a_log[l], d_skip[l], ssm_norm[l], w_out[l], norm_post[l])
        nb = yp.shape[0]
        conv0 = jnp.zeros((nb, CONV_K - 1, CONV_DIM), yp.dtype)
        h00 = jnp.zeros((nb, N_HEADS_SSM, SSM_HEAD_DIM, D_STATE), state_ssm.dtype)
        yp, kp, vp, cp, hp = layer(yp, None, None, conv0, h00, *params)
        ys, ksm, vsm, csm, hsm = layer(ys, cache_k[l], cache_v[l], state_conv[l], state_ssm[l], *params)
        kp_l.append(kp); vp_l.append(vp); cp_l.append(cp); hp_l.append(hp)
        ks_l.append(ksm); vs_l.append(vsm); cs_l.append(csm); hs_l.append(hsm)
    return (yp, ys, jnp.stack(kp_l), jnp.stack(vp_l), jnp.stack(cp_l), jnp.stack(hp_l),
            jnp.stack(ks_l), jnp.stack(vs_l), jnp.stack(cs_l), jnp.stack(hs_l))
```

```python
import functools

import jax
import jax.numpy as jnp
from jax import lax
from jax.experimental import pallas as pl
from jax.experimental.pallas import tpu as pltpu

F32 = jnp.float32
BF16 = jnp.bfloat16

D_MODEL = 2048
HEAD_DIM = 64
N_KV_HEADS = 4
D_ATTN = 2048
D_KV = 256
WINDOW = 128
D_SSM = 2048
N_HEADS_SSM = 32
N_GROUPS = 4
D_STATE = 128
CONV_K = 4
GROUP_W = D_SSM // N_GROUPS
BC_W = N_GROUPS * D_STATE
CONV_DIM = D_SSM + 2 * BC_W
EPS = 1e-6
LANES = 128
SUBLANES = 8

OFF_Q, OFF_G, OFF_Z, OFF_X = 0, 2048, 4096, 6144
OFF_K, OFF_V, OFF_B, OFF_C = 8192, 8448, 8704, 9216
U_W = 9728
DT_W = LANES

VMEM_LIMIT = 56 * 1024 * 1024


def _silu(x):
    return x * jax.nn.sigmoid(x)


def _cparams(sem):
    return pltpu.CompilerParams(dimension_semantics=sem, vmem_limit_bytes=VMEM_LIMIT)


def _inproj_body(x_ref, g_ref, w_ref, wdt_ref, u_ref, dt_ref, h_scr, *, row_chunk):
    tm = x_ref.shape[0]

    @pl.when(pl.program_id(1) == 0)
    def _():
        def norm_rows(r, carry):
            rows = pl.ds(pl.multiple_of(r * row_chunk, row_chunk), row_chunk)
            x = x_ref[rows, :]
            y = x * lax.rsqrt(jnp.mean(x * x, axis=-1, keepdims=True) + EPS)
            h_scr[rows, :] = (y * g_ref[...]).astype(BF16)
            return carry
        lax.fori_loop(0, tm // row_chunk, norm_rows, 0)
        dt_ref[...] = jnp.dot(h_scr[...], wdt_ref[...], preferred_element_type=F32)

    u_ref[...] = jnp.dot(h_scr[...], w_ref[...], preferred_element_type=F32)


def _inproj(x2d, gain, w_main, w_dt, tm, tn):
    m = x2d.shape[0]
    tm = min(tm, m)
    return pl.pallas_call(
        functools.partial(_inproj_body, row_chunk=min(128, tm)),
        name="inproj",
        grid=(m // tm, U_W // tn),
        in_specs=[
            pl.BlockSpec((tm, D_MODEL), lambda i, j: (i, 0)),
            pl.BlockSpec((1, D_MODEL), lambda i, j: (0, 0)),
            pl.BlockSpec((D_MODEL, tn), lambda i, j: (0, j)),
            pl.BlockSpec((D_MODEL, DT_W), lambda i, j: (0, 0)),
        ],
        out_specs=[
            pl.BlockSpec((tm, tn), lambda i, j: (i, j)),
            pl.BlockSpec((tm, DT_W), lambda i, j: (i, 0)),
        ],
        out_shape=[jax.ShapeDtypeStruct((m, U_W), F32), jax.ShapeDtypeStruct((m, DT_W), F32)],
        scratch_shapes=[pltpu.VMEM((tm, D_MODEL), BF16)],
        compiler_params=_cparams(("parallel", "arbitrary")),
    )(x2d, gain, w_main, w_dt)


def _outproj_body(a_ref, s_ref, x_ref, wa_ref, ws_ref, g_ref, y_ref):
    o = jnp.dot(a_ref[...].astype(BF16), wa_ref[...], preferred_element_type=F32)
    o = o + jnp.dot(s_ref[...].astype(BF16), ws_ref[...], preferred_element_type=F32)
    o = o * lax.rsqrt(jnp.mean(o * o, axis=-1, keepdims=True) + EPS)
    y_ref[...] = x_ref[...] + o * g_ref[...]


def _outproj(attn, ssm, x2d, w_a, w_s, gain, tm):
    m = x2d.shape[0]
    tm = min(tm, m)
    row = lambda i: (i, 0)
    fixed = lambda i: (0, 0)
    return pl.pallas_call(
        _outproj_body,
        name="outproj",
        grid=(m // tm,),
        in_specs=[
            pl.BlockSpec((tm, D_ATTN), row),
            pl.BlockSpec((tm, D_SSM), row),
            pl.BlockSpec((tm, D_MODEL), row),
            pl.BlockSpec((D_ATTN, D_MODEL), fixed),
            pl.BlockSpec((D_SSM, D_MODEL), fixed),
            pl.BlockSpec((1, D_MODEL), fixed),
        ],
        out_specs=pl.BlockSpec((tm, D_MODEL), row),
        out_shape=jax.ShapeDtypeStruct((m, D_MODEL), F32),
        compiler_params=_cparams(("parallel",)),
    )(attn, ssm, x2d, w_a, w_s, gain)


def _head_pair_blockdiag(slab):
    lo = lax.broadcasted_iota(jnp.int32, slab.shape, 1) < HEAD_DIM
    rolled = pltpu.roll(slab, HEAD_DIM, axis=1)
    zero = jnp.zeros_like(slab)
    bd_a = jnp.concatenate([jnp.where(lo, slab, zero), jnp.where(lo, zero, rolled)], axis=0)
    bd_b = jnp.concatenate([jnp.where(lo, rolled, zero), jnp.where(lo, zero, slab)], axis=0)
    return bd_a.astype(BF16), bd_b.astype(BF16)


def _attend(q, kk, vv, mask, sink_ref, o_scr):
    tq, tk = mask.shape
    kk = kk * (HEAD_DIM ** -0.5)
    kbd, vbd = [], []
    for c in range(D_KV // LANES):
        kbd.extend(_head_pair_blockdiag(kk[:, c * LANES:(c + 1) * LANES]))
        vbd.extend(_head_pair_blockdiag(vv[:, c * LANES:(c + 1) * LANES]))
    lo = lax.broadcasted_iota(jnp.int32, (tq, LANES), 1) < HEAD_DIM
    pairs_per_group = (D_ATTN // N_KV_HEADS) // LANES
    for g in range(N_KV_HEADS):
        for j in range(pairs_per_group):
            col = g * pairs_per_group + j
            qp = q[:, col * LANES:(col + 1) * LANES].astype(BF16)
            s = lax.dot_general(qp, kbd[g], (((1,), (1,)), ((), ())), preferred_element_type=F32)
            ps, invs = [], []
            for e in range(2):
                se = jnp.where(mask, s[:, e * tk:(e + 1) * tk], -jnp.inf)
                snk = sink_ref[2 * col + e]
                mx = jnp.maximum(jnp.max(se, axis=-1, keepdims=True), snk)
                p = jnp.exp(se - mx)
                den = jnp.sum(p, axis=-1, keepdims=True) + jnp.exp(snk - mx)
                ps.append(p)
                invs.append(1.0 / den)
            pp = jnp.concatenate(ps, axis=1).astype(BF16)
            o = jnp.dot(pp, vbd[g], preferred_element_type=F32)
            o_scr[:, col * LANES:(col + 1) * LANES] = o * jnp.where(lo, invs[0], invs[1])


def _gate_norm(o, gate, gain):
    y = o * _silu(gate)
    y = y * lax.rsqrt(jnp.mean(y * y, axis=-1, keepdims=True) + EPS)
    return y * gain


def _attn_prompt_body(sink_ref, q_ref, g_ref, kp_ref, kc_ref, vp_ref, vc_ref, gain_ref, out_ref, o_scr):
    i = pl.program_id(1)
    kk = jnp.concatenate([kp_ref[0], kc_ref[0]], axis=0)
    vv = jnp.concatenate([vp_ref[0], vc_ref[0]], axis=0)
    ql = lax.broadcasted_iota(jnp.int32, (WINDOW, 2 * WINDOW), 0)
    ks = lax.broadcasted_iota(jnp.int32, (WINDOW, 2 * WINDOW), 1)
    rel = ql + WINDOW - ks
    first_key = jnp.where(i > 0, 0, WINDOW)
    mask = (rel >= 0) & (rel < WINDOW) & (ks >= first_key)
    _attend(q_ref[0], kk, vv, mask, sink_ref, o_scr)
    out_ref[0] = _gate_norm(o_scr[...], g_ref[0], gain_ref[...]).astype(BF16)


def _attn_prompt(u3, sink, gain):
    b, s, _ = u3.shape
    nb = s // WINDOW
    blk = lambda w, col: pl.BlockSpec((1, WINDOW, w), lambda bi, i: (bi, i, col))
    prev = lambda w, col: pl.BlockSpec((1, WINDOW, w), lambda bi, i: (bi, jnp.maximum(i - 1, 0), col))
    return pl.pallas_call(
        _attn_prompt_body,
        name="attn_prompt",
        grid=(b, nb),
        in_specs=[
            pl.BlockSpec(memory_space=pltpu.SMEM),
            blk(D_ATTN, OFF_Q // D_ATTN),
            blk(D_ATTN, OFF_G // D_ATTN),
            prev(D_KV, OFF_K // D_KV),
            blk(D_KV, OFF_K // D_KV),
            prev(D_KV, OFF_V // D_KV),
            blk(D_KV, OFF_V // D_KV),
            pl.BlockSpec((1, D_ATTN), lambda bi, i: (0, 0)),
        ],
        out_specs=pl.BlockSpec((1, WINDOW, D_ATTN), lambda bi, i: (bi, i, 0)),
        out_shape=jax.ShapeDtypeStruct((b, s, D_ATTN), BF16),
        scratch_shapes=[pltpu.VMEM((WINDOW, D_ATTN), F32)],
        compiler_params=_cparams(("parallel", "arbitrary")),
    )(sink, u3, u3, u3, u3, u3, u3, gain)


Q_PAD = 16


def _attn_step_body(sink_ref, q_ref, g_ref, kn_ref, vn_ref, ck_ref, cv_ref, gain_ref,
                    out_ref, ko_ref, vo_ref, o_scr, *, nb, t):
    tk = 2 * WINDOW
    ql = lax.broadcasted_iota(jnp.int32, (Q_PAD, tk), 0)
    ks = lax.broadcasted_iota(jnp.int32, (Q_PAD, tk), 1)
    rel = ql + WINDOW - ks
    mask = (rel >= 0) & (rel < WINDOW) & (ql < t)
    pad_k = jnp.zeros((tk - WINDOW - t, D_KV), F32)
    pad_q = jnp.zeros((Q_PAD - t, D_ATTN), F32)
    for b in range(nb):
        kk = jnp.concatenate([ck_ref[b], kn_ref[b], pad_k], axis=0)
        vv = jnp.concatenate([cv_ref[b], vn_ref[b], pad_k], axis=0)
        q = jnp.concatenate([q_ref[b], pad_q], axis=0)
        _attend(q, kk, vv, mask, sink_ref, o_scr)
        o = o_scr[0:t, :]
        out_ref[b] = _gate_norm(o, g_ref[b], gain_ref[...])
        ko_ref[b] = kk[t:t + WINDOW, :]
        vo_ref[b] = vv[t:t + WINDOW, :]


def _attn_step(u3, cache_k, cache_v, sink, gain, nb):
    n, t, _ = u3.shape
    blk = lambda w, col: pl.BlockSpec((nb, t, w), lambda i: (i, 0, col))
    cache = pl.BlockSpec((nb, WINDOW, D_KV), lambda i: (i, 0, 0))
    return pl.pallas_call(
        functools.partial(_attn_step_body, nb=nb, t=t),
        name="attn_step",
        grid=(n // nb,),
        in_specs=[
            pl.BlockSpec(memory_space=pltpu.SMEM),
            blk(D_ATTN, OFF_Q // D_ATTN),
            blk(D_ATTN, OFF_G // D_ATTN),
            blk(D_KV, OFF_K // D_KV),
            blk(D_KV, OFF_V // D_KV),
            cache,
            cache,
            pl.BlockSpec((1, D_ATTN), lambda i: (0, 0)),
        ],
        out_specs=[pl.BlockSpec((nb, t, D_ATTN), lambda i: (i, 0, 0)), cache, cache],
        out_shape=[
            jax.ShapeDtypeStruct((n, t, D_ATTN), F32),
            jax.ShapeDtypeStruct((n, WINDOW, D_KV), F32),
            jax.ShapeDtypeStruct((n, WINDOW, D_KV), F32),
        ],
        scratch_shapes=[pltpu.VMEM((Q_PAD, D_ATTN), F32)],
        compiler_params=_cparams(("parallel",)),
    )(sink, u3, u3, u3, u3, cache_k, cache_v, gain)


ROWS = 128


def _expand_heads(v):
    r = v.shape[0]
    lo = lax.broadcasted_iota(jnp.int32, (r, LANES), 1) < HEAD_DIM
    cols = []
    for c in range(D_SSM // LANES):
        va = jnp.broadcast_to(v[:, 2 * c:2 * c + 1], (r, LANES))
        vb = jnp.broadcast_to(v[:, 2 * c + 1:2 * c + 2], (r, LANES))
        cols.append(jnp.where(lo, va, vb))
    return jnp.concatenate(cols, axis=1)


def _conv_taps(load_rows, w_ref, b_ref):
    acc = b_ref[...] + w_ref[CONV_K - 1:CONV_K, :] * load_rows(CONV_K - 1)
    for k in range(CONV_K - 1):
        acc = acc + w_ref[k:k + 1, :] * load_rows(k)
    return _silu(acc)


def _ssd_intra(xs, bm, cm, dt_raw, dtb_ref, alog_ref, causal, same, ybuf):
    dt = jax.nn.softplus(dt_raw + dtb_ref[...])
    a_neg = -jnp.exp(alog_ref[...])
    dta = dt * a_neg
    hi = lax.Precision.HIGHEST
    a = jnp.dot(causal.astype(F32), dta, precision=hi, preferred_element_type=F32)
    tot = jnp.dot(same.astype(F32), dta, precision=hi, preferred_element_type=F32)
    a_t = a.T
    dt_t = dt.T
    lo = lax.broadcasted_iota(jnp.int32, (ROWS, LANES), 1) < HEAD_DIM
    heads_per_group = N_HEADS_SSM // N_GROUPS
    for g in range(N_GROUPS):
        cg = cm[:, g * D_STATE:(g + 1) * D_STATE].astype(BF16)
        bg = bm[:, g * D_STATE:(g + 1) * D_STATE].astype(BF16)
        cb = lax.dot_general(cg, bg, (((1,), (1,)), ((), ())), preferred_element_type=F32)
        for j in range(heads_per_group // 2):
            c = g * (heads_per_group // 2) + j
            ms = []
            for e in range(2):
                h = 2 * c + e
                diff = jnp.broadcast_to(a[:, h:h + 1], (ROWS, ROWS)) - a_t[h:h + 1, :]
                dec = jnp.exp(jnp.where(causal, diff, -jnp.inf))
                ms.append(cb * dec * dt_t[h:h + 1, :])
            lhs = jnp.concatenate(ms, axis=1).astype(BF16)
            xc = xs[:, c * LANES:(c + 1) * LANES]
            zero = jnp.zeros_like(xc)
            rhs = jnp.concatenate([jnp.where(lo, xc, zero), jnp.where(lo, zero, xc)], axis=0).astype(BF16)
            ybuf[:, c * LANES:(c + 1) * LANES] = jnp.dot(lhs, rhs, preferred_element_type=F32)
    e_expa = _expand_heads(jnp.exp(a))
    e_w = _expand_heads(jnp.exp(tot - a) * dt)
    return e_expa, e_w


def _gated_group_norm(y, z, gain_ref, out_ref_store):
    yz = y * _silu(z)
    for g in range(N_GROUPS):
        blk = yz[:, g * GROUP_W:(g + 1) * GROUP_W]
        nrm = blk * lax.rsqrt(jnp.mean(blk * blk, axis=-1, keepdims=True) + EPS)
        out_ref_store(g, nrm * gain_ref[:, g * GROUP_W:(g + 1) * GROUP_W])


def _ssd_prompt_body(z_ref, xs_ref, b_ref, c_ref, dt_ref,
                     cwx, cwb, cwc, cbx, cbb, cbc, dtb_ref, alog_ref, dskip_ref, gain_ref,
                     out_ref, cvx_ref, cvb_ref, cvc_ref, st_ref,
                     xpx, xpb, xpc, h_t, ybuf):
    ci = pl.program_id(1)
    last = pl.num_programs(1) - 1
    pad = SUBLANES

    @pl.when(ci == 0)
    def _():
        xpx[0:pad, :] = jnp.zeros((pad, D_SSM), F32)
        xpb[0:pad, :] = jnp.zeros((pad, BC_W), F32)
        xpc[0:pad, :] = jnp.zeros((pad, BC_W), F32)
        h_t[...] = jnp.zeros_like(h_t)

    xpx[pad:pad + ROWS, :] = xs_ref[0]
    xpb[pad:pad + ROWS, :] = b_ref[0]
    xpc[pad:pad + ROWS, :] = c_ref[0]
    first = pad - (CONV_K - 1)
    xs = _conv_taps(lambda k: xpx[first + k:first + k + ROWS, :], cwx, cbx)
    bm = _conv_taps(lambda k: xpb[first + k:first + k + ROWS, :], cwb, cbb)
    cm = _conv_taps(lambda k: xpc[first + k:first + k + ROWS, :], cwc, cbc)

    rl = lax.broadcasted_iota(jnp.int32, (ROWS, ROWS), 0)
    cs = lax.broadcasted_iota(jnp.int32, (ROWS, ROWS), 1)
    causal = cs <= rl
    same = cs >= 0
    e_expa, e_w = _ssd_intra(xs, bm, cm, dt_ref[0], dtb_ref, alog_ref, causal, same, ybuf)

    xw = (xs * e_w).astype(BF16)
    for g in range(N_GROUPS):
        gs = slice(g * GROUP_W, (g + 1) * GROUP_W)
        ds = slice(g * D_STATE, (g + 1) * D_STATE)
        h_prev = h_t[:, gs]
        y_inter = jnp.dot(cm[:, ds].astype(BF16), h_prev.astype(BF16), preferred_element_type=F32)
        ybuf[:, gs] = ybuf[:, gs] + y_inter * e_expa[:, gs]
        s_t = jnp.dot(bm[:, ds].T.astype(BF16), xw[:, gs], preferred_element_type=F32)
        h_t[:, gs] = h_prev * e_expa[ROWS - 1:ROWS, gs] + s_t

    y = ybuf[...] + dskip_ref[...] * xs

    def store(g, v):
        out_ref[0, :, g * GROUP_W:(g + 1) * GROUP_W] = v.astype(BF16)
    _gated_group_norm(y, z_ref[0], gain_ref, store)

    xpx[0:pad, :] = xpx[ROWS:ROWS + pad, :]
    xpb[0:pad, :] = xpb[ROWS:ROWS + pad, :]
    xpc[0:pad, :] = xpc[ROWS:ROWS + pad, :]

    @pl.when(ci == last)
    def _():
        cvx_ref[0] = xpx[first:pad, :]
        cvb_ref[0] = xpb[first:pad, :]
        cvc_ref[0] = xpc[first:pad, :]
        st_ref[0] = h_t[...].T


def _ssd_prompt(u3, dt3, p):
    b, s, _ = u3.shape
    nc = s // ROWS
    blk = lambda w, col: pl.BlockSpec((1, ROWS, w), lambda bi, ci: (bi, ci, col))
    par = lambda r, w: pl.BlockSpec((r, w), lambda bi, ci: (0, 0))
    per_b = lambda r, w: pl.BlockSpec((1, r, w), lambda bi, ci: (bi, 0, 0))
    tail = CONV_K - 1
    return pl.pallas_call(
        _ssd_prompt_body,
        name="ssd_prompt",
        grid=(b, nc),
        in_specs=[
            blk(D_SSM, OFF_Z // D_SSM), blk(D_SSM, OFF_X // D_SSM),
            blk(BC_W, OFF_B // BC_W), blk(BC_W, OFF_C // BC_W),
            blk(DT_W, 0),
            par(CONV_K, D_SSM), par(CONV_K, BC_W), par(CONV_K, BC_W),
            par(1, D_SSM), par(1, BC_W), par(1, BC_W),
            par(1, DT_W), par(1, DT_W), par(1, D_SSM), par(1, D_SSM),
        ],
        out_specs=[
            blk(D_SSM, 0),
            per_b(tail, D_SSM), per_b(tail, BC_W), per_b(tail, BC_W),
            per_b(D_SSM, D_STATE),
        ],
        out_shape=[
            jax.ShapeDtypeStruct((b, s, D_SSM), BF16),
            jax.ShapeDtypeStruct((b, tail, D_SSM), F32),
            jax.ShapeDtypeStruct((b, tail, BC_W), F32),
            jax.ShapeDtypeStruct((b, tail, BC_W), F32),
            jax.ShapeDtypeStruct((b, D_SSM, D_STATE), F32),
        ],
        scratch_shapes=[
            pltpu.VMEM((ROWS + SUBLANES, D_SSM), F32),
            pltpu.VMEM((ROWS + SUBLANES, BC_W), F32),
            pltpu.VMEM((ROWS + SUBLANES, BC_W), F32),
            pltpu.VMEM((D_STATE, D_SSM), F32),
            pltpu.VMEM((ROWS, D_SSM), F32),
        ],
        compiler_params=_cparams(("parallel", "arbitrary")),
    )(u3, u3, u3, u3, dt3,
      p["cwx"], p["cwb"], p["cwc"], p["cbx"], p["cbb"], p["cbc"],
      p["dtb"], p["alog"], p["dskip"], p["ssm_gain"])


def _ssd_step_body(z_ref, xs_ref, b_ref, c_ref, dt_ref, sx_ref, sb_ref, sc_ref, h0_ref,
                   cwx, cwb, cwc, cbx, cbb, cbc, dtb_ref, alog_ref, dskip_ref, gain_ref,
                   out_ref, cvx_ref, cvb_ref, cvc_ref, st_ref,
                   xpx, xpb, xpc, ybuf, cm_scr, ea_scr, ea_t_scr, xw_t_scr, *, nb, t):
    real = nb * t
    tail = CONV_K - 1
    lo_row = SUBLANES - tail

    def conv(xp, new_ref, st_in_ref, st_out_ref, w_ref, b_ref):
        xp[:, lo_row:SUBLANES, :] = st_in_ref[...]
        xp[:, SUBLANES:SUBLANES + t, :] = new_ref[...]
        st_out_ref[...] = xp[:, SUBLANES + t - tail:SUBLANES + t, :]
        w = new_ref.shape[-1]
        act = _conv_taps(lambda k: xp[:, lo_row + k:lo_row + k + t, :].reshape(real, w), w_ref, b_ref)
        return jnp.concatenate([act, jnp.zeros((ROWS - real, w), F32)], axis=0)

    xs = conv(xpx, xs_ref, sx_ref, cvx_ref, cwx, cbx)
    bm = conv(xpb, b_ref, sb_ref, cvb_ref, cwb, cbb)
    cm = conv(xpc, c_ref, sc_ref, cvc_ref, cwc, cbc)
    dt_raw = jnp.concatenate([dt_ref[...].reshape(real, DT_W), jnp.zeros((ROWS - real, DT_W), F32)], axis=0)

    rl = lax.broadcasted_iota(jnp.int32, (ROWS, ROWS), 0)
    cs = lax.broadcasted_iota(jnp.int32, (ROWS, ROWS), 1)
    same = (rl // t) == (cs // t)
    causal = same & (cs <= rl)
    e_expa, e_w = _ssd_intra(xs, bm, cm, dt_raw, dtb_ref, alog_ref, causal, same, ybuf)

    cm_scr[...] = cm
    ea_scr[...] = e_expa
    ea_t_scr[...] = e_expa.T
    xw_t_scr[...] = (xs * e_w).T.astype(BF16)
    row_id = lax.broadcasted_iota(jnp.int32, (ROWS, D_STATE), 0)
    pair = 2 * SUBLANES
    for b in range(nb):
        in_seq = (row_id >= b * t) & (row_id < (b + 1) * t)
        r0 = b * t
        last_col = r0 + t - 1
        for g in range(N_GROUPS):
            gs = slice(g * GROUP_W, (g + 1) * GROUP_W)
            ds = slice(g * D_STATE, (g + 1) * D_STATE)
            h0 = h0_ref[b, gs, :]
            c_rows = cm_scr[r0:r0 + pair, ds].astype(BF16)
            y_inter = lax.dot_general(c_rows, h0.astype(BF16), (((1,), (1,)), ((), ())),
                                      preferred_element_type=F32)
            ybuf[r0:r0 + t, gs] = ybuf[r0:r0 + t, gs] + y_inter[0:t] * ea_scr[r0:r0 + t, gs]
            b_rows = jnp.where(in_seq, bm[:, ds], 0.0).astype(BF16)
            s_new = jnp.dot(xw_t_scr[gs, :], b_rows, preferred_element_type=F32)
            decay = jnp.broadcast_to(ea_t_scr[gs, last_col:last_col + 1], (GROUP_W, D_STATE))
            st_ref[b, gs, :] = h0 * decay + s_new

    y = ybuf[0:real, :] + dskip_ref[...] * xs[0:real, :]
    z = z_ref[...].reshape(real, D_SSM)

    def store(g, v):
        out_ref[:, :, g * GROUP_W:(g + 1) * GROUP_W] = v.reshape(nb, t, GROUP_W)
    _gated_group_norm(y, z, gain_ref, store)


def _ssd_step(u3, dt3, conv_state, h0, p, nb):
    n, t, _ = u3.shape
    tail = CONV_K - 1
    blk = lambda w, col: pl.BlockSpec((nb, t, w), lambda i: (i, 0, col))
    cst = lambda w, col: pl.BlockSpec((nb, tail, w), lambda i: (i, 0, col))
    par = lambda r, w: pl.BlockSpec((r, w), lambda i: (0, 0))
    state = pl.BlockSpec((nb, D_SSM, D_STATE), lambda i: (i, 0, 0))
    return pl.pallas_call(
        functools.partial(_ssd_step_body, nb=nb, t=t),
        name="ssd_step",
        grid=(n // nb,),
        in_specs=[
            blk(D_SSM, OFF_Z // D_SSM), blk(D_SSM, OFF_X // D_SSM),
            blk(BC_W, OFF_B // BC_W), blk(BC_W, OFF_C // BC_W),
            blk(DT_W, 0),
            cst(D_SSM, 0), cst(BC_W, D_SSM // BC_W), cst(BC_W, D_SSM // BC_W + 1),
            state,
            par(CONV_K, D_SSM), par(CONV_K, BC_W), par(CONV_K, BC_W),
            par(1, D_SSM), par(1, BC_W), par(1, BC_W),
            par(1, DT_W), par(1, DT_W), par(1, D_SSM), par(1, D_SSM),
        ],
        out_specs=[
            blk(D_SSM, 0),
            cst(D_SSM, 0), cst(BC_W, 0), cst(BC_W, 0),
            state,
        ],
        out_shape=[
            jax.ShapeDtypeStruct((n, t, D_SSM), F32),
            jax.ShapeDtypeStruct((n, tail, D_SSM), F32),
            jax.ShapeDtypeStruct((n, tail, BC_W), F32),
            jax.ShapeDtypeStruct((n, tail, BC_W), F32),
            jax.ShapeDtypeStruct((n, D_SSM, D_STATE), F32),
        ],
        scratch_shapes=[
            pltpu.VMEM((nb, SUBLANES + t, D_SSM), F32),
            pltpu.VMEM((nb, SUBLANES + t, BC_W), F32),
            pltpu.VMEM((nb, SUBLANES + t, BC_W), F32),
            pltpu.VMEM((ROWS, D_SSM), F32),
            pltpu.VMEM((ROWS, BC_W), F32),
            pltpu.VMEM((ROWS, D_SSM), F32),
            pltpu.VMEM((D_SSM, ROWS), F32),
            pltpu.VMEM((D_SSM, ROWS), BF16),
        ],
        compiler_params=_cparams(("parallel",)),
    )(u3, u3, u3, u3, dt3, conv_state, conv_state, conv_state, h0,
      p["cwx"], p["cwb"], p["cwc"], p["cbx"], p["cbb"], p["cbc"],
      p["dtb"], p["alog"], p["dskip"], p["ssm_gain"])


def _layer_params(norm_pre, w_in, attn_sink, attn_norm, conv_w, conv_b, dt_bias, a_log, d_skip,
                  ssm_norm, w_out, norm_post):
    q0, k0, v0, g0, z0, x0 = 0, 2048, 2304, 2560, 4608, 6656
    b0, c0, dt0 = x0 + D_SSM, x0 + D_SSM + BC_W, x0 + CONV_DIM
    seg = lambda lo, w: w_in[:, lo:lo + w]
    w_main = jnp.concatenate(
        [seg(q0, D_ATTN), seg(g0, D_ATTN), seg(z0, D_SSM), seg(x0, D_SSM),
         seg(k0, D_KV), seg(v0, D_KV), seg(b0, BC_W), seg(c0, BC_W)], axis=1).astype(BF16)
    w_dt = jnp.pad(seg(dt0, N_HEADS_SSM), ((0, 0), (0, DT_W - N_HEADS_SSM))).astype(BF16)
    pad_h = lambda v: jnp.pad(v, (0, DT_W - N_HEADS_SSM)).reshape(1, DT_W)
    return dict(
        norm_pre=norm_pre.reshape(1, D_MODEL), w_main=w_main, w_dt=w_dt,
        sink=attn_sink, attn_gain=attn_norm.reshape(1, D_ATTN),
        cwx=conv_w[:, :D_SSM], cwb=conv_w[:, D_SSM:D_SSM + BC_W], cwc=conv_w[:, D_SSM + BC_W:],
        cbx=conv_b[:D_SSM].reshape(1, -1), cbb=conv_b[D_SSM:D_SSM + BC_W].reshape(1, -1),
        cbc=conv_b[D_SSM + BC_W:].reshape(1, -1),
        dtb=pad_h(dt_bias), alog=pad_h(a_log),
        dskip=jnp.repeat(d_skip, HEAD_DIM).reshape(1, D_SSM),
        ssm_gain=ssm_norm.reshape(1, D_SSM),
        w_out_a=w_out[:D_ATTN].astype(BF16), w_out_s=w_out[D_ATTN:].astype(BF16),
        norm_post=norm_post.reshape(1, D_MODEL),
    )


IN_TM, IN_TN, OUT_TM = 1024, 512, 512
STEP_NB_ATTN, STEP_NB_SSD = 8, 8


def _layer(x, caches, p):
    n, t, _ = x.shape
    x2d = x.reshape(n * t, D_MODEL)
    u, dt = _inproj(x2d, p["norm_pre"], p["w_main"], p["w_dt"], IN_TM, IN_TN)
    u3 = u.reshape(n, t, U_W)
    dt3 = dt.reshape(n, t, DT_W)
    if caches is None:
        attn = _attn_prompt(u3, p["sink"], p["attn_gain"])
        new_k = u3[:, t - WINDOW:, OFF_K:OFF_K + D_KV]
        new_v = u3[:, t - WINDOW:, OFF_V:OFF_V + D_KV]
        ssm, cvx, cvb, cvc, h_new = _ssd_prompt(u3, dt3, p)
    else:
        kbuf, vbuf, conv_buf, h0 = caches
        attn, new_k, new_v = _attn_step(u3, kbuf.reshape(n, WINDOW, D_KV), vbuf.reshape(n, WINDOW, D_KV),
                                        p["sink"], p["attn_gain"], STEP_NB_ATTN)
        ssm, cvx, cvb, cvc, h_new = _ssd_step(u3, dt3, conv_buf, h0.reshape(n, D_SSM, D_STATE), p, STEP_NB_SSD)
    y = _outproj(attn.reshape(n * t, D_ATTN), ssm.reshape(n * t, D_SSM), x2d,
                 p["w_out_a"], p["w_out_s"], p["norm_post"], OUT_TM)
    return (y.reshape(n, t, D_MODEL),
            new_k.reshape(n, WINDOW, N_KV_HEADS, HEAD_DIM), new_v.reshape(n, WINDOW, N_KV_HEADS, HEAD_DIM),
            jnp.concatenate([cvx, cvb, cvc], axis=-1),
            h_new.reshape(n, N_HEADS_SSM, HEAD_DIM, D_STATE))


def kernel(x_prompt, x_sample, cache_k, cache_v, state_conv, state_ssm, norm_pre, w_in, attn_sink,
           attn_norm, conv_w, conv_b, dt_bias, a_log, d_skip, ssm_norm, w_out, norm_post):
    depth = w_in.shape[0]
    yp, ys = x_prompt, x_sample
    outs_p, outs_s = [], []
    for l in range(depth):
        p = _layer_params(norm_pre[l], w_in[l], attn_sink[l], attn_norm[l], conv_w[l], conv_b[l],
                          dt_bias[l], a_log[l], d_skip[l], ssm_norm[l], w_out[l], norm_post[l])
        yp, *rest_p = _layer(yp, None, p)
        ys, *rest_s = _layer(ys, (cache_k[l], cache_v[l], state_conv[l], state_ssm[l]), p)
        outs_p.append(rest_p)
        outs_s.append(rest_s)
    stack = lambda outs, i: jnp.stack([o[i] for o in outs])
    return (yp, ys,
            stack(outs_p, 0), stack(outs_p, 1), stack(outs_p, 2), stack(outs_p, 3),
            stack(outs_s, 0), stack(outs_s, 1), stack(outs_s, 2), stack(outs_s, 3))
```

```python
import functools

import jax
import jax.numpy as jnp
from jax import lax
from jax.experimental import pallas as pl
from jax.experimental.pallas import tpu as pltpu

F32 = jnp.float32
BF16 = jnp.bfloat16

D_MODEL = 2048
HEAD_DIM = 64
N_KV_HEADS = 4
D_ATTN = 2048
D_KV = 256
WINDOW = 128
D_SSM = 2048
N_HEADS_SSM = 32
N_GROUPS = 4
D_STATE = 128
CONV_K = 4
GROUP_W = D_SSM // N_GROUPS
BC_W = N_GROUPS * D_STATE
CONV_DIM = D_SSM + 2 * BC_W
EPS = 1e-6
LOG2E = 1.4426950408889634
LANES = 128
SUBLANES = 8

OFF_Q, OFF_G, OFF_Z, OFF_X = 0, 2048, 4096, 6144
OFF_K, OFF_V, OFF_B, OFF_C = 8192, 8448, 8704, 9216
U_W = 9728
DT_W = LANES

VMEM_LIMIT = 56 * 1024 * 1024


def _silu(x):
    h = 0.5 * x
    return h + h * jnp.tanh(h)


def _cparams(sem):
    return pltpu.CompilerParams(dimension_semantics=sem, vmem_limit_bytes=VMEM_LIMIT)


def _inproj_body(x_ref, g_ref, w_ref, wdt_ref, u_ref, dt_ref, h_scr, *, row_chunk):
    tm = x_ref.shape[0]

    @pl.when(pl.program_id(1) == 0)
    def _():
        def norm_rows(r, carry):
            rows = pl.ds(pl.multiple_of(r * row_chunk, row_chunk), row_chunk)
            x = x_ref[rows, :]
            y = x * lax.rsqrt(jnp.mean(x * x, axis=-1, keepdims=True) + EPS)
            h_scr[rows, :] = (y * g_ref[...]).astype(BF16)
            return carry
        lax.fori_loop(0, tm // row_chunk, norm_rows, 0)
        dt_ref[...] = jnp.dot(h_scr[...], wdt_ref[...], preferred_element_type=F32)

    u_ref[...] = jnp.dot(h_scr[...], w_ref[...], preferred_element_type=F32)


def _inproj(x2d, gain, w_main, w_dt, tm, tn):
    m = x2d.shape[0]
    tm = min(tm, m)
    return pl.pallas_call(
        functools.partial(_inproj_body, row_chunk=min(128, tm)),
        name="inproj",
        grid=(m // tm, U_W // tn),
        in_specs=[
            pl.BlockSpec((tm, D_MODEL), lambda i, j: (i, 0)),
            pl.BlockSpec((1, D_MODEL), lambda i, j: (0, 0)),
            pl.BlockSpec((D_MODEL, tn), lambda i, j: (0, j)),
            pl.BlockSpec((D_MODEL, DT_W), lambda i, j: (0, 0)),
        ],
        out_specs=[
            pl.BlockSpec((tm, tn), lambda i, j: (i, j)),
            pl.BlockSpec((tm, DT_W), lambda i, j: (i, 0)),
        ],
        out_shape=[jax.ShapeDtypeStruct((m, U_W), F32), jax.ShapeDtypeStruct((m, DT_W), F32)],
        scratch_shapes=[pltpu.VMEM((tm, D_MODEL), BF16)],
        compiler_params=_cparams(("parallel", "arbitrary")),
    )(x2d, gain, w_main, w_dt)


def _outproj_body(a_ref, s_ref, x_ref, wa_ref, ws_ref, g_ref, y_ref):
    o = jnp.dot(a_ref[...].astype(BF16), wa_ref[...], preferred_element_type=F32)
    o = o + jnp.dot(s_ref[...].astype(BF16), ws_ref[...], preferred_element_type=F32)
    o = o * lax.rsqrt(jnp.mean(o * o, axis=-1, keepdims=True) + EPS)
    y_ref[...] = x_ref[...] + o * g_ref[...]


def _outproj(attn, ssm, x2d, w_a, w_s, gain, tm):
    m = x2d.shape[0]
    tm = min(tm, m)
    row = lambda i: (i, 0)
    fixed = lambda i: (0, 0)
    return pl.pallas_call(
        _outproj_body,
        name="outproj",
        grid=(m // tm,),
        in_specs=[
            pl.BlockSpec((tm, D_ATTN), row),
            pl.BlockSpec((tm, D_SSM), row),
            pl.BlockSpec((tm, D_MODEL), row),
            pl.BlockSpec((D_ATTN, D_MODEL), fixed),
            pl.BlockSpec((D_SSM, D_MODEL), fixed),
            pl.BlockSpec((1, D_MODEL), fixed),
        ],
        out_specs=pl.BlockSpec((tm, D_MODEL), row),
        out_shape=jax.ShapeDtypeStruct((m, D_MODEL), F32),
        compiler_params=_cparams(("parallel",)),
    )(attn, ssm, x2d, w_a, w_s, gain)


def _head_pair_blockdiag(slab):
    lo = lax.broadcasted_iota(jnp.int32, slab.shape, 1) < HEAD_DIM
    rolled = pltpu.roll(slab, HEAD_DIM, axis=1)
    zero = jnp.zeros_like(slab)
    bd_a = jnp.concatenate([jnp.where(lo, slab, zero), jnp.where(lo, zero, rolled)], axis=0)
    bd_b = jnp.concatenate([jnp.where(lo, rolled, zero), jnp.where(lo, zero, slab)], axis=0)
    return bd_a.astype(BF16), bd_b.astype(BF16)


def _attend(q, kk, vv, mask, sink_ref, o_scr):
    tq, tk = mask.shape
    kk = kk * (HEAD_DIM ** -0.5 * LOG2E)
    kbd, vbd = [], []
    for c in range(D_KV // LANES):
        kbd.extend(_head_pair_blockdiag(kk[:, c * LANES:(c + 1) * LANES]))
        vbd.extend(_head_pair_blockdiag(vv[:, c * LANES:(c + 1) * LANES]))
    lo = lax.broadcasted_iota(jnp.int32, (tq, LANES), 1) < HEAD_DIM
    pairs_per_group = (D_ATTN // N_KV_HEADS) // LANES
    for g in range(N_KV_HEADS):
        cols = [g * pairs_per_group + j for j in range(pairs_per_group)]
        q_stack = jnp.concatenate([q[:, c * LANES:(c + 1) * LANES] for c in cols], axis=0).astype(BF16)
        s = lax.dot_general(q_stack, kbd[g], (((1,), (1,)), ((), ())), preferred_element_type=F32)
        p_rows, inv_rows = [], []
        for j, col in enumerate(cols):
            ps, invs = [], []
            for e in range(2):
                se = jnp.where(mask, s[j * tq:(j + 1) * tq, e * tk:(e + 1) * tk], -jnp.inf)
                snk = sink_ref[2 * col + e] * LOG2E
                mx = jnp.maximum(jnp.max(se, axis=-1, keepdims=True), snk)
                p = jnp.exp2(se - mx)
                den = jnp.sum(p, axis=-1, keepdims=True) + jnp.exp2(snk - mx)
                ps.append(p)
                invs.append(1.0 / den)
            p_rows.append(jnp.concatenate(ps, axis=1).astype(BF16))
            inv_rows.append(jnp.where(lo, invs[0], invs[1]))
        o = jnp.dot(jnp.concatenate(p_rows, axis=0), vbd[g], preferred_element_type=F32)
        for j, col in enumerate(cols):
            o_scr[:, col * LANES:(col + 1) * LANES] = o[j * tq:(j + 1) * tq, :] * inv_rows[j]


def _gate_norm(o, gate, gain):
    y = o * _silu(gate)
    y = y * lax.rsqrt(jnp.mean(y * y, axis=-1, keepdims=True) + EPS)
    return y * gain


def _attn_prompt_body(sink_ref, q_ref, g_ref, kp_ref, kc_ref, vp_ref, vc_ref, gain_ref, out_ref, o_scr):
    i = pl.program_id(1)
    kk = jnp.concatenate([kp_ref[0], kc_ref[0]], axis=0)
    vv = jnp.concatenate([vp_ref[0], vc_ref[0]], axis=0)
    ql = lax.broadcasted_iota(jnp.int32, (WINDOW, 2 * WINDOW), 0)
    ks = lax.broadcasted_iota(jnp.int32, (WINDOW, 2 * WINDOW), 1)
    rel = ql + WINDOW - ks
    first_key = jnp.where(i > 0, 0, WINDOW)
    mask = (rel >= 0) & (rel < WINDOW) & (ks >= first_key)
    _attend(q_ref[0], kk, vv, mask, sink_ref, o_scr)
    out_ref[0] = _gate_norm(o_scr[...], g_ref[0], gain_ref[...]).astype(BF16)


def _attn_prompt(u3, sink, gain):
    b, s, _ = u3.shape
    nb = s // WINDOW
    blk = lambda w, col: pl.BlockSpec((1, WINDOW, w), lambda bi, i: (bi, i, col))
    prev = lambda w, col: pl.BlockSpec((1, WINDOW, w), lambda bi, i: (bi, jnp.maximum(i - 1, 0), col))
    return pl.pallas_call(
        _attn_prompt_body,
        name="attn_prompt",
        grid=(b, nb),
        in_specs=[
            pl.BlockSpec(memory_space=pltpu.SMEM),
            blk(D_ATTN, OFF_Q // D_ATTN),
            blk(D_ATTN, OFF_G // D_ATTN),
            prev(D_KV, OFF_K // D_KV),
            blk(D_KV, OFF_K // D_KV),
            prev(D_KV, OFF_V // D_KV),
            blk(D_KV, OFF_V // D_KV),
            pl.BlockSpec((1, D_ATTN), lambda bi, i: (0, 0)),
        ],
        out_specs=pl.BlockSpec((1, WINDOW, D_ATTN), lambda bi, i: (bi, i, 0)),
        out_shape=jax.ShapeDtypeStruct((b, s, D_ATTN), BF16),
        scratch_shapes=[pltpu.VMEM((WINDOW, D_ATTN), F32)],
        compiler_params=_cparams(("parallel", "arbitrary")),
    )(sink, u3, u3, u3, u3, u3, u3, gain)


Q_PAD = 16


def _attn_step_body(sink_ref, q_ref, g_ref, kn_ref, vn_ref, ck_ref, cv_ref, gain_ref,
                    out_ref, ko_ref, vo_ref, o_scr, *, nb, t):
    tk = 2 * WINDOW
    ql = lax.broadcasted_iota(jnp.int32, (Q_PAD, tk), 0)
    ks = lax.broadcasted_iota(jnp.int32, (Q_PAD, tk), 1)
    rel = ql + WINDOW - ks
    mask = (rel >= 0) & (rel < WINDOW) & (ql < t)
    pad_k = jnp.zeros((tk - WINDOW - t, D_KV), F32)
    pad_q = jnp.zeros((Q_PAD - t, D_ATTN), F32)

    def one_sequence(b, carry):
        kk = jnp.concatenate([ck_ref[b], kn_ref[b], pad_k], axis=0)
        vv = jnp.concatenate([cv_ref[b], vn_ref[b], pad_k], axis=0)
        q = jnp.concatenate([q_ref[b], pad_q], axis=0)
        _attend(q, kk, vv, mask, sink_ref, o_scr.at[b])
        out_ref[b] = _gate_norm(o_scr[b, 0:t, :], g_ref[b], gain_ref[...])
        ko_ref[b] = kk[t:t + WINDOW, :]
        vo_ref[b] = vv[t:t + WINDOW, :]
        return carry

    lax.fori_loop(0, nb, one_sequence, 0, unroll=2)


def _attn_step(u3, cache_k, cache_v, sink, gain, nb):
    n, t, _ = u3.shape
    blk = lambda w, col: pl.BlockSpec((nb, t, w), lambda i: (i, 0, col))
    cache = pl.BlockSpec((nb, WINDOW, D_KV), lambda i: (i, 0, 0))
    return pl.pallas_call(
        functools.partial(_attn_step_body, nb=nb, t=t),
        name="attn_step",
        grid=(n // nb,),
        in_specs=[
            pl.BlockSpec(memory_space=pltpu.SMEM),
            blk(D_ATTN, OFF_Q // D_ATTN),
            blk(D_ATTN, OFF_G // D_ATTN),
            blk(D_KV, OFF_K // D_KV),
            blk(D_KV, OFF_V // D_KV),
            cache,
            cache,
            pl.BlockSpec((1, D_ATTN), lambda i: (0, 0)),
        ],
        out_specs=[pl.BlockSpec((nb, t, D_ATTN), lambda i: (i, 0, 0)), cache, cache],
        out_shape=[
            jax.ShapeDtypeStruct((n, t, D_ATTN), F32),
            jax.ShapeDtypeStruct((n, WINDOW, D_KV), F32),
            jax.ShapeDtypeStruct((n, WINDOW, D_KV), F32),
        ],
        scratch_shapes=[pltpu.VMEM((nb, Q_PAD, D_ATTN), F32)],
        compiler_params=_cparams(("parallel",)),
    )(sink, u3, u3, u3, u3, cache_k, cache_v, gain)


ROWS = 128


def _expand_heads(v):
    r = v.shape[0]
    lo = lax.broadcasted_iota(jnp.int32, (r, LANES), 1) < HEAD_DIM
    cols = []
    for c in range(D_SSM // LANES):
        va = jnp.broadcast_to(v[:, 2 * c:2 * c + 1], (r, LANES))
        vb = jnp.broadcast_to(v[:, 2 * c + 1:2 * c + 2], (r, LANES))
        cols.append(jnp.where(lo, va, vb))
    return jnp.concatenate(cols, axis=1)


def _conv_taps(load_rows, w_ref, b_ref):
    acc = b_ref[...] + w_ref[CONV_K - 1:CONV_K, :] * load_rows(CONV_K - 1)
    for k in range(CONV_K - 1):
        acc = acc + w_ref[k:k + 1, :] * load_rows(k)
    return _silu(acc)


def _conv_rolled(x, tail_scr, w_ref, b_ref):
    taps = CONV_K - 1
    width = x.shape[1]
    tiles = ROWS // SUBLANES
    with_prev = jnp.concatenate([tail_scr[...], x], axis=0).reshape(tiles + 1, SUBLANES, width)
    row = lax.broadcasted_iota(jnp.int32, (tiles, SUBLANES, width), 1)
    acc = b_ref[...] + w_ref[taps:taps + 1, :] * x
    for j in range(1, CONV_K):
        rot = pltpu.roll(with_prev, j, axis=1)
        shifted = jnp.where(row < j, rot[:tiles], rot[1:]).reshape(ROWS, width)
        acc = acc + w_ref[taps - j:taps - j + 1, :] * shifted
    tail_scr[...] = x[ROWS - SUBLANES:, :]
    return _silu(acc)


def _ssd_intra(xs, bm, cm, dt_raw, dtb_ref, alog_ref, causal, same, ybuf):
    dt = jax.nn.softplus(dt_raw + dtb_ref[...])
    a_neg = -jnp.exp(alog_ref[...])
    dta = dt * (a_neg * LOG2E)
    hi = lax.Precision.HIGHEST
    a = jnp.dot(causal.astype(F32), dta, precision=hi, preferred_element_type=F32)
    tot = jnp.dot(same.astype(F32), dta, precision=hi, preferred_element_type=F32)
    r_t = (a - jnp.log2(dt)).T
    lo = lax.broadcasted_iota(jnp.int32, (ROWS, LANES), 1) < HEAD_DIM
    heads_per_group = N_HEADS_SSM // N_GROUPS
    for g in range(N_GROUPS):
        cg = cm[:, g * D_STATE:(g + 1) * D_STATE].astype(BF16)
        bg = bm[:, g * D_STATE:(g + 1) * D_STATE].astype(BF16)
        cb = lax.dot_general(cg, bg, (((1,), (1,)), ((), ())), preferred_element_type=F32)
        for j in range(heads_per_group // 2):
            c = g * (heads_per_group // 2) + j
            ms = []
            for e in range(2):
                h = 2 * c + e
                diff = jnp.broadcast_to(a[:, h:h + 1], (ROWS, ROWS)) - r_t[h:h + 1, :]
                ms.append(cb * jnp.exp2(jnp.where(causal, diff, -jnp.inf)))
            lhs = jnp.concatenate(ms, axis=1).astype(BF16)
            xc = xs[:, c * LANES:(c + 1) * LANES]
            zero = jnp.zeros_like(xc)
            rhs = jnp.concatenate([jnp.where(lo, xc, zero), jnp.where(lo, zero, xc)], axis=0).astype(BF16)
            ybuf[:, c * LANES:(c + 1) * LANES] = jnp.dot(lhs, rhs, preferred_element_type=F32)
    e_expa = _expand_heads(jnp.exp2(a))
    e_w = _expand_heads(jnp.exp2(tot - a) * dt)
    return e_expa, e_w


def _gated_group_norm(y, z, gain_ref, out_ref_store):
    yz = y * _silu(z)
    for g in range(N_GROUPS):
        blk = yz[:, g * GROUP_W:(g + 1) * GROUP_W]
        nrm = blk * lax.rsqrt(jnp.mean(blk * blk, axis=-1, keepdims=True) + EPS)
        out_ref_store(g, nrm * gain_ref[:, g * GROUP_W:(g + 1) * GROUP_W])


def _ssd_prompt_body(z_ref, xs_ref, b_ref, c_ref, dt_ref,
                     cwx, cwb, cwc, cbx, cbb, cbc, dtb_ref, alog_ref, dskip_ref, gain_ref,
                     out_ref, cvx_ref, cvb_ref, cvc_ref, st_ref,
                     tlx, tlb, tlc, h_t, ybuf):
    ci = pl.program_id(1)
    last = pl.num_programs(1) - 1

    @pl.when(ci == 0)
    def _():
        tlx[...] = jnp.zeros_like(tlx)
        tlb[...] = jnp.zeros_like(tlb)
        tlc[...] = jnp.zeros_like(tlc)
        h_t[...] = jnp.zeros_like(h_t)

    xs = _conv_rolled(xs_ref[0], tlx, cwx, cbx)
    bm = _conv_rolled(b_ref[0], tlb, cwb, cbb)
    cm = _conv_rolled(c_ref[0], tlc, cwc, cbc)

    rl = lax.broadcasted_iota(jnp.int32, (ROWS, ROWS), 0)
    cs = lax.broadcasted_iota(jnp.int32, (ROWS, ROWS), 1)
    causal = cs <= rl
    same = cs >= 0
    e_expa, e_w = _ssd_intra(xs, bm, cm, dt_ref[0], dtb_ref, alog_ref, causal, same, ybuf)

    xw = (xs * e_w).astype(BF16)
    for g in range(N_GROUPS):
        gs = slice(g * GROUP_W, (g + 1) * GROUP_W)
        ds = slice(g * D_STATE, (g + 1) * D_STATE)
        h_prev = h_t[:, gs]
        y_inter = jnp.dot(cm[:, ds].astype(BF16), h_prev.astype(BF16), preferred_element_type=F32)
        ybuf[:, gs] = ybuf[:, gs] + y_inter * e_expa[:, gs]
        s_t = jnp.dot(bm[:, ds].T.astype(BF16), xw[:, gs], preferred_element_type=F32)
        h_t[:, gs] = h_prev * e_expa[ROWS - 1:ROWS, gs] + s_t

    y = ybuf[...] + dskip_ref[...] * xs

    def store(g, v):
        out_ref[0, :, g * GROUP_W:(g + 1) * GROUP_W] = v.astype(BF16)
    _gated_group_norm(y, z_ref[0], gain_ref, store)

    @pl.when(ci == last)
    def _():
        first = SUBLANES - (CONV_K - 1)
        cvx_ref[0] = tlx[first:SUBLANES, :]
        cvb_ref[0] = tlb[first:SUBLANES, :]
        cvc_ref[0] = tlc[first:SUBLANES, :]
        st_ref[0] = h_t[...].T


def _ssd_prompt(u3, dt3, p):
    b, s, _ = u3.shape
    nc = s // ROWS
    blk = lambda w, col: pl.BlockSpec((1, ROWS, w), lambda bi, ci: (bi, ci, col))
    par = lambda r, w: pl.BlockSpec((r, w), lambda bi, ci: (0, 0))
    per_b = lambda r, w: pl.BlockSpec((1, r, w), lambda bi, ci: (bi, 0, 0))
    tail = CONV_K - 1
    return pl.pallas_call(
        _ssd_prompt_body,
        name="ssd_prompt",
        grid=(b, nc),
        in_specs=[
            blk(D_SSM, OFF_Z // D_SSM), blk(D_SSM, OFF_X // D_SSM),
            blk(BC_W, OFF_B // BC_W), blk(BC_W, OFF_C // BC_W),
            blk(DT_W, 0),
            par(CONV_K, D_SSM), par(CONV_K, BC_W), par(CONV_K, BC_W),
            par(1, D_SSM), par(1, BC_W), par(1, BC_W),
            par(1, DT_W), par(1, DT_W), par(1, D_SSM), par(1, D_SSM),
        ],
        out_specs=[
            blk(D_SSM, 0),
            per_b(tail, D_SSM), per_b(tail, BC_W), per_b(tail, BC_W),
            per_b(D_SSM, D_STATE),
        ],
        out_shape=[
            jax.ShapeDtypeStruct((b, s, D_SSM), BF16),
            jax.ShapeDtypeStruct((b, tail, D_SSM), F32),
            jax.ShapeDtypeStruct((b, tail, BC_W), F32),
            jax.ShapeDtypeStruct((b, tail, BC_W), F32),
            jax.ShapeDtypeStruct((b, D_SSM, D_STATE), F32),
        ],
        scratch_shapes=[
            pltpu.VMEM((SUBLANES, D_SSM), F32),
            pltpu.VMEM((SUBLANES, BC_W), F32),
            pltpu.VMEM((SUBLANES, BC_W), F32),
            pltpu.VMEM((D_STATE, D_SSM), F32),
            pltpu.VMEM((ROWS, D_SSM), F32),
        ],
        compiler_params=_cparams(("parallel", "arbitrary")),
    )(u3, u3, u3, u3, dt3,
      p["cwx"], p["cwb"], p["cwc"], p["cbx"], p["cbb"], p["cbc"],
      p["dtb"], p["alog"], p["dskip"], p["ssm_gain"])


def _ssd_step_body(z_ref, xs_ref, b_ref, c_ref, dt_ref, sx_ref, sb_ref, sc_ref, h0_ref,
                   cwx, cwb, cwc, cbx, cbb, cbc, dtb_ref, alog_ref, dskip_ref, gain_ref,
                   out_ref, cvx_ref, cvb_ref, cvc_ref, st_ref,
                   xpx, xpb, xpc, ybuf, cm_scr, ea_scr, ea_t_scr, xw_t_scr, *, nb, t):
    real = nb * t
    tail = CONV_K - 1
    lo_row = SUBLANES - tail

    def conv(xp, new_ref, st_in_ref, st_out_ref, w_ref, b_ref):
        xp[:, lo_row:SUBLANES, :] = st_in_ref[...]
        xp[:, SUBLANES:SUBLANES + t, :] = new_ref[...]
        st_out_ref[...] = xp[:, SUBLANES + t - tail:SUBLANES + t, :]
        w = new_ref.shape[-1]
        act = _conv_taps(lambda k: xp[:, lo_row + k:lo_row + k + t, :].reshape(real, w), w_ref, b_ref)
        return jnp.concatenate([act, jnp.zeros((ROWS - real, w), F32)], axis=0)

    xs = conv(xpx, xs_ref, sx_ref, cvx_ref, cwx, cbx)
    bm = conv(xpb, b_ref, sb_ref, cvb_ref, cwb, cbb)
    cm = conv(xpc, c_ref, sc_ref, cvc_ref, cwc, cbc)
    dt_raw = jnp.concatenate([dt_ref[...].reshape(real, DT_W), jnp.zeros((ROWS - real, DT_W), F32)], axis=0)

    rl = lax.broadcasted_iota(jnp.int32, (ROWS, ROWS), 0)
    cs = lax.broadcasted_iota(jnp.int32, (ROWS, ROWS), 1)
    same = (rl // t) == (cs // t)
    causal = same & (cs <= rl)
    e_expa, e_w = _ssd_intra(xs, bm, cm, dt_raw, dtb_ref, alog_ref, causal, same, ybuf)

    cm_scr[...] = cm
    ea_scr[...] = e_expa
    ea_t_scr[...] = e_expa.T
    xw_t_scr[...] = (xs * e_w).T.astype(BF16)
    row_id = lax.broadcasted_iota(jnp.int32, (ROWS, D_STATE), 0)
    pair = 2 * SUBLANES
    for b in range(nb):
        in_seq = (row_id >= b * t) & (row_id < (b + 1) * t)
        r0 = b * t
        last_col = r0 + t - 1
        for g in range(N_GROUPS):
            gs = slice(g * GROUP_W, (g + 1) * GROUP_W)
            ds = slice(g * D_STATE, (g + 1) * D_STATE)
            h0 = h0_ref[b, gs, :]
            c_rows = cm_scr[r0:r0 + pair, ds].astype(BF16)
            y_inter = lax.dot_general(c_rows, h0.astype(BF16), (((1,), (1,)), ((), ())),
                                      preferred_element_type=F32)
            ybuf[r0:r0 + t, gs] = ybuf[r0:r0 + t, gs] + y_inter[0:t] * ea_scr[r0:r0 + t, gs]
            b_rows = jnp.where(in_seq, bm[:, ds], 0.0).astype(BF16)
            s_new = jnp.dot(xw_t_scr[gs, :], b_rows, preferred_element_type=F32)
            decay = jnp.broadcast_to(ea_t_scr[gs, last_col:last_col + 1], (GROUP_W, D_STATE))
            st_ref[b, gs, :] = h0 * decay + s_new

    y = ybuf[0:real, :] + dskip_ref[...] * xs[0:real, :]
    z = z_ref[...].reshape(real, D_SSM)

    def store(g, v):
        out_ref[:, :, g * GROUP_W:(g + 1) * GROUP_W] = v.reshape(nb, t, GROUP_W)
    _gated_group_norm(y, z, gain_ref, store)


def _ssd_step(u3, dt3, conv_state, h0, p, nb):
    n, t, _ = u3.shape
    tail = CONV_K - 1
    blk = lambda w, col: pl.BlockSpec((nb, t, w), lambda i: (i, 0, col))
    cst = lambda w, col: pl.BlockSpec((nb, tail, w), lambda i: (i, 0, col))
    par = lambda r, w: pl.BlockSpec((r, w), lambda i: (0, 0))
    state = pl.BlockSpec((nb, D_SSM, D_STATE), lambda i: (i, 0, 0))
    return pl.pallas_call(
        functools.partial(_ssd_step_body, nb=nb, t=t),
        name="ssd_step",
        grid=(n // nb,),
        in_specs=[
            blk(D_SSM, OFF_Z // D_SSM), blk(D_SSM, OFF_X // D_SSM),
            blk(BC_W, OFF_B // BC_W), blk(BC_W, OFF_C // BC_W),
            blk(DT_W, 0),
            cst(D_SSM, 0), cst(BC_W, D_SSM // BC_W), cst(BC_W, D_SSM // BC_W + 1),
            state,
            par(CONV_K, D_SSM), par(CONV_K, BC_W), par(CONV_K, BC_W),
            par(1, D_SSM), par(1, BC_W), par(1, BC_W),
            par(1, DT_W), par(1, DT_W), par(1, D_SSM), par(1, D_SSM),
        ],
        out_specs=[
            blk(D_SSM, 0),
            cst(D_SSM, 0), cst(BC_W, 0), cst(BC_W, 0),
            state,
        ],
        out_shape=[
            jax.ShapeDtypeStruct((n, t, D_SSM), F32),
            jax.ShapeDtypeStruct((n, tail, D_SSM), F32),
            jax.ShapeDtypeStruct((n, tail, BC_W), F32),
            jax.ShapeDtypeStruct((n, tail, BC_W), F32),
            jax.ShapeDtypeStruct((n, D_SSM, D_STATE), F32),
        ],
        scratch_shapes=[
            pltpu.VMEM((nb, SUBLANES + t, D_SSM), F32),
            pltpu.VMEM((nb, SUBLANES + t, BC_W), F32),
            pltpu.VMEM((nb, SUBLANES + t, BC_W), F32),
            pltpu.VMEM((ROWS, D_SSM), F32),
            pltpu.VMEM((ROWS, BC_W), F32),
            pltpu.VMEM((ROWS, D_SSM), F32),
            pltpu.VMEM((D_SSM, ROWS), F32),
            pltpu.VMEM((D_SSM, ROWS), BF16),
        ],
        compiler_params=_cparams(("parallel",)),
    )(u3, u3, u3, u3, dt3, conv_state, conv_state, conv_state, h0,
      p["cwx"], p["cwb"], p["cwc"], p["cbx"], p["cbb"], p["cbc"],
      p["dtb"], p["alog"], p["dskip"], p["ssm_gain"])


def _layer_params(norm_pre, w_in, attn_sink, attn_norm, conv_w, conv_b, dt_bias, a_log, d_skip,
                  ssm_norm, w_out, norm_post):
    q0, k0, v0, g0, z0, x0 = 0, 2048, 2304, 2560, 4608, 6656
    b0, c0, dt0 = x0 + D_SSM, x0 + D_SSM + BC_W, x0 + CONV_DIM
    seg = lambda lo, w: w_in[:, lo:lo + w]
    w_main = jnp.concatenate(
        [seg(q0, D_ATTN), seg(g0, D_ATTN), seg(z0, D_SSM), seg(x0, D_SSM),
         seg(k0, D_KV), seg(v0, D_KV), seg(b0, BC_W), seg(c0, BC_W)], axis=1).astype(BF16)
    w_dt = jnp.pad(seg(dt0, N_HEADS_SSM), ((0, 0), (0, DT_W - N_HEADS_SSM))).astype(BF16)
    pad_h = lambda v: jnp.pad(v, (0, DT_W - N_HEADS_SSM)).reshape(1, DT_W)
    return dict(
        norm_pre=norm_pre.reshape(1, D_MODEL), w_main=w_main, w_dt=w_dt,
        sink=attn_sink, attn_gain=attn_norm.reshape(1, D_ATTN),
        cwx=conv_w[:, :D_SSM], cwb=conv_w[:, D_SSM:D_SSM + BC_W], cwc=conv_w[:, D_SSM + BC_W:],
        cbx=conv_b[:D_SSM].reshape(1, -1), cbb=conv_b[D_SSM:D_SSM + BC_W].reshape(1, -1),
        cbc=conv_b[D_SSM + BC_W:].reshape(1, -1),
        dtb=pad_h(dt_bias), alog=pad_h(a_log),
        dskip=jnp.repeat(d_skip, HEAD_DIM).reshape(1, D_SSM),
        ssm_gain=ssm_norm.reshape(1, D_SSM),
        w_out_a=w_out[:D_ATTN].astype(BF16), w_out_s=w_out[D_ATTN:].astype(BF16),
        norm_post=norm_post.reshape(1, D_MODEL),
    )


IN_TM, IN_TN, OUT_TM = 1024, 512, 512
STEP_NB_ATTN, STEP_NB_SSD = 8, 8


def _layer(x, caches, p):
    n, t, _ = x.shape
    x2d = x.reshape(n * t, D_MODEL)
    u, dt = _inproj(x2d, p["norm_pre"], p["w_main"], p["w_dt"], IN_TM, IN_TN)
    u3 = u.reshape(n, t, U_W)
    dt3 = dt.reshape(n, t, DT_W)
    if caches is None:
        attn = _attn_prompt(u3, p["sink"], p["attn_gain"])
        new_k = u3[:, t - WINDOW:, OFF_K:OFF_K + D_KV]
        new_v = u3[:, t - WINDOW:, OFF_V:OFF_V + D_KV]
        ssm, cvx, cvb, cvc, h_new = _ssd_prompt(u3, dt3, p)
    else:
        kbuf, vbuf, conv_buf, h0 = caches
        attn, new_k, new_v = _attn_step(u3, kbuf.reshape(n, WINDOW, D_KV), vbuf.reshape(n, WINDOW, D_KV),
                                        p["sink"], p["attn_gain"], STEP_NB_ATTN)
        ssm, cvx, cvb, cvc, h_new = _ssd_step(u3, dt3, conv_buf, h0.reshape(n, D_SSM, D_STATE), p, STEP_NB_SSD)
    y = _outproj(attn.reshape(n * t, D_ATTN), ssm.reshape(n * t, D_SSM), x2d,
                 p["w_out_a"], p["w_out_s"], p["norm_post"], OUT_TM)
    return (y.reshape(n, t, D_MODEL),
            new_k.reshape(n, WINDOW, N_KV_HEADS, HEAD_DIM), new_v.reshape(n, WINDOW, N_KV_HEADS, HEAD_DIM),
            jnp.concatenate([cvx, cvb, cvc], axis=-1),
            h_new.reshape(n, N_HEADS_SSM, HEAD_DIM, D_STATE))


def kernel(x_prompt, x_sample, cache_k, cache_v, state_conv, state_ssm, norm_pre, w_in, attn_sink,
           attn_norm, conv_w, conv_b, dt_bias, a_log, d_skip, ssm_norm, w_out, norm_post):
    depth = w_in.shape[0]
    yp, ys = x_prompt, x_sample
    outs_p, outs_s = [], []
    for l in range(depth):
        p = _layer_params(norm_pre[l], w_in[l], attn_sink[l], attn_norm[l], conv_w[l], conv_b[l],
                          dt_bias[l], a_log[l], d_skip[l], ssm_norm[l], w_out[l], norm_post[l])
        yp, *rest_p = _layer(yp, None, p)
        ys, *rest_s = _layer(ys, (cache_k[l], cache_v[l], state_conv[l], state_ssm[l]), p)
        outs_p.append(rest_p)
        outs_s.append(rest_s)
    stack = lambda outs, i: jnp.stack([o[i] for o in outs])
    return (yp, ys,
            stack(outs_p, 0), stack(outs_p, 1), stack(outs_p, 2), stack(outs_p, 3),
            stack(outs_s, 0), stack(outs_s, 1), stack(outs_s, 2), stack(outs_s, 3))
```

```python
import functools

import jax
import jax.numpy as jnp
from jax import lax
from jax.experimental import pallas as pl
from jax.experimental.pallas import tpu as pltpu

F32 = jnp.float32
BF16 = jnp.bfloat16

D_MODEL = 2048
HEAD_DIM = 64
N_KV_HEADS = 4
D_ATTN = 2048
D_KV = 256
WINDOW = 128
D_SSM = 2048
N_HEADS_SSM = 32
N_GROUPS = 4
D_STATE = 128
CONV_K = 4
GROUP_W = D_SSM // N_GROUPS
BC_W = N_GROUPS * D_STATE
CONV_DIM = D_SSM + 2 * BC_W
EPS = 1e-6
LOG2E = 1.4426950408889634
LANES = 128
SUBLANES = 8

DT_W = LANES
WA_COLS = 2 * D_ATTN + 2 * D_KV
WS_COLS = D_SSM + CONV_DIM + DT_W

VMEM_LIMIT = 56 * 1024 * 1024


def _silu(x):
    h = 0.5 * x
    return h + h * jnp.tanh(h)


def _cparams(sem):
    return pltpu.CompilerParams(dimension_semantics=sem, vmem_limit_bytes=VMEM_LIMIT)


def _inproj_attn_body(x_ref, g_ref, w_ref, h_ref, qg_ref, kv_ref, *, row_chunk):
    tm = x_ref.shape[0]

    def norm_rows(r, carry):
        rows = pl.ds(pl.multiple_of(r * row_chunk, row_chunk), row_chunk)
        x = x_ref[rows, :]
        y = x * lax.rsqrt(jnp.mean(x * x, axis=-1, keepdims=True) + EPS)
        h_ref[rows, :] = (y * g_ref[...]).astype(BF16)
        return carry
    lax.fori_loop(0, tm // row_chunk, norm_rows, 0)

    h = h_ref[...]
    dot = lambda lo, hi: jnp.dot(h, w_ref[:, lo:hi], preferred_element_type=F32)
    qg_ref[:, 0:D_ATTN] = dot(0, D_ATTN).astype(qg_ref.dtype)
    kv_ref[...] = dot(D_ATTN, D_ATTN + 2 * D_KV)
    qg_ref[:, D_ATTN:2 * D_ATTN] = dot(D_ATTN + 2 * D_KV, WA_COLS).astype(qg_ref.dtype)


def _inproj_ssm_body(h_ref, w_ref, z_ref, xbc_ref, dt_ref):
    h = h_ref[...]
    dot = lambda lo, hi: jnp.dot(h, w_ref[:, lo:hi], preferred_element_type=F32)
    z_ref[...] = dot(0, D_SSM).astype(z_ref.dtype)
    xbc_ref[:, 0:D_SSM] = dot(D_SSM, 2 * D_SSM)
    xbc_ref[:, D_SSM:CONV_DIM] = dot(2 * D_SSM, D_SSM + CONV_DIM)
    dt_ref[...] = dot(D_SSM + CONV_DIM, WS_COLS)


def _inproj(x2d, gain, w_attn, w_ssm, tm, act_dtype):
    m = x2d.shape[0]
    tm = min(tm, m)
    row = lambda w: pl.BlockSpec((tm, w), lambda i: (i, 0))
    resident = lambda r, w: pl.BlockSpec((r, w), lambda i: (0, 0), pipeline_mode=pl.Buffered(1))
    h, qg, kv = pl.pallas_call(
        functools.partial(_inproj_attn_body, row_chunk=min(128, tm)),
        name="inproj_attn",
        grid=(m // tm,),
        in_specs=[row(D_MODEL), resident(1, D_MODEL), resident(D_MODEL, WA_COLS)],
        out_specs=[row(D_MODEL), row(2 * D_ATTN), row(2 * D_KV)],
        out_shape=[jax.ShapeDtypeStruct((m, D_MODEL), BF16),
                   jax.ShapeDtypeStruct((m, 2 * D_ATTN), act_dtype),
                   jax.ShapeDtypeStruct((m, 2 * D_KV), F32)],
        compiler_params=_cparams(("parallel",)),
    )(x2d, gain, w_attn)
    z, xbc, dt = pl.pallas_call(
        _inproj_ssm_body,
        name="inproj_ssm",
        grid=(m // tm,),
        in_specs=[row(D_MODEL), resident(D_MODEL, WS_COLS)],
        out_specs=[row(D_SSM), row(CONV_DIM), row(DT_W)],
        out_shape=[jax.ShapeDtypeStruct((m, D_SSM), act_dtype),
                   jax.ShapeDtypeStruct((m, CONV_DIM), F32),
                   jax.ShapeDtypeStruct((m, DT_W), F32)],
        compiler_params=_cparams(("parallel",)),
    )(h, w_ssm)
    return qg, kv, z, xbc, dt


def _outproj_body(a_ref, s_ref, x_ref, wa_ref, ws_ref, g_ref, y_ref):
    o = jnp.dot(a_ref[...].astype(BF16), wa_ref[...], preferred_element_type=F32)
    o = o + jnp.dot(s_ref[...].astype(BF16), ws_ref[...], preferred_element_type=F32)
    o = o * lax.rsqrt(jnp.mean(o * o, axis=-1, keepdims=True) + EPS)
    y_ref[...] = x_ref[...] + o * g_ref[...]


def _outproj(attn, ssm, x2d, w_out, gain, tm):
    m = x2d.shape[0]
    tm = min(tm, m)
    row = lambda i: (i, 0)
    once = dict(pipeline_mode=pl.Buffered(1))
    return pl.pallas_call(
        _outproj_body,
        name="outproj",
        grid=(m // tm,),
        in_specs=[
            pl.BlockSpec((tm, D_ATTN), row),
            pl.BlockSpec((tm, D_SSM), row),
            pl.BlockSpec((tm, D_MODEL), row),
            pl.BlockSpec((D_ATTN, D_MODEL), lambda i: (0, 0), **once),
            pl.BlockSpec((D_SSM, D_MODEL), lambda i: (1, 0), **once),
            pl.BlockSpec((1, D_MODEL), lambda i: (0, 0), **once),
        ],
        out_specs=pl.BlockSpec((tm, D_MODEL), row),
        out_shape=jax.ShapeDtypeStruct((m, D_MODEL), F32),
        compiler_params=_cparams(("parallel",)),
    )(attn, ssm, x2d, w_out, w_out, gain)


def _head_pair_blockdiag(slab):
    lo = lax.broadcasted_iota(jnp.int32, slab.shape, 1) < HEAD_DIM
    rolled = pltpu.roll(slab, HEAD_DIM, axis=1)
    zero = jnp.zeros_like(slab)
    bd_a = jnp.concatenate([jnp.where(lo, slab, zero), jnp.where(lo, zero, rolled)], axis=0)
    bd_b = jnp.concatenate([jnp.where(lo, rolled, zero), jnp.where(lo, zero, slab)], axis=0)
    return bd_a.astype(BF16), bd_b.astype(BF16)


def _attend(q, kk, vv, mask, sink_ref, o_scr):
    tq, tk = mask.shape
    kk = kk * (HEAD_DIM ** -0.5 * LOG2E)
    kbd, vbd = [], []
    for c in range(D_KV // LANES):
        kbd.extend(_head_pair_blockdiag(kk[:, c * LANES:(c + 1) * LANES]))
        vbd.extend(_head_pair_blockdiag(vv[:, c * LANES:(c + 1) * LANES]))
    lo = lax.broadcasted_iota(jnp.int32, (tq, LANES), 1) < HEAD_DIM
    pairs_per_group = (D_ATTN // N_KV_HEADS) // LANES
    for g in range(N_KV_HEADS):
        cols = [g * pairs_per_group + j for j in range(pairs_per_group)]
        q_stack = jnp.concatenate([q[:, c * LANES:(c + 1) * LANES] for c in cols], axis=0).astype(BF16)
        s = lax.dot_general(q_stack, kbd[g], (((1,), (1,)), ((), ())), preferred_element_type=F32)
        p_rows, inv_rows = [], []
        for j, col in enumerate(cols):
            ps, invs = [], []
            for e in range(2):
                se = jnp.where(mask, s[j * tq:(j + 1) * tq, e * tk:(e + 1) * tk], -jnp.inf)
                snk = sink_ref[2 * col + e] * LOG2E
                mx = jnp.maximum(jnp.max(se, axis=-1, keepdims=True), snk)
                p = jnp.exp2(se - mx)
                den = jnp.sum(p, axis=-1, keepdims=True) + jnp.exp2(snk - mx)
                ps.append(p)
                invs.append(1.0 / den)
            p_rows.append(jnp.concatenate(ps, axis=1).astype(BF16))
            inv_rows.append(jnp.where(lo, invs[0], invs[1]))
        o = jnp.dot(jnp.concatenate(p_rows, axis=0), vbd[g], preferred_element_type=F32)
        for j, col in enumerate(cols):
            o_scr[:, col * LANES:(col + 1) * LANES] = o[j * tq:(j + 1) * tq, :] * inv_rows[j]


def _gate_norm(o, gate, gain):
    y = o * _silu(gate)
    y = y * lax.rsqrt(jnp.mean(y * y, axis=-1, keepdims=True) + EPS)
    return y * gain


def _attn_prompt_body(sink_ref, q_ref, g_ref, kp_ref, kc_ref, vp_ref, vc_ref, gain_ref, out_ref, o_scr):
    i = pl.program_id(1)
    kk = jnp.concatenate([kp_ref[0], kc_ref[0]], axis=0)
    vv = jnp.concatenate([vp_ref[0], vc_ref[0]], axis=0)
    ql = lax.broadcasted_iota(jnp.int32, (WINDOW, 2 * WINDOW), 0)
    ks = lax.broadcasted_iota(jnp.int32, (WINDOW, 2 * WINDOW), 1)
    rel = ql + WINDOW - ks
    first_key = jnp.where(i > 0, 0, WINDOW)
    mask = (rel >= 0) & (rel < WINDOW) & (ks >= first_key)
    _attend(q_ref[0], kk, vv, mask, sink_ref, o_scr)
    out_ref[0] = _gate_norm(o_scr[...], g_ref[0].astype(F32), gain_ref[...]).astype(BF16)


def _attn_prompt(qg3, kv3, sink, gain):
    b, s, _ = qg3.shape
    nb = s // WINDOW
    blk = lambda w, col: pl.BlockSpec((1, WINDOW, w), lambda bi, i: (bi, i, col))
    prev = lambda w, col: pl.BlockSpec((1, WINDOW, w), lambda bi, i: (bi, jnp.maximum(i - 1, 0), col))
    return pl.pallas_call(
        _attn_prompt_body,
        name="attn_prompt",
        grid=(b, nb),
        in_specs=[
            pl.BlockSpec(memory_space=pltpu.SMEM),
            blk(D_ATTN, 0), blk(D_ATTN, 1),
            prev(D_KV, 0), blk(D_KV, 0),
            prev(D_KV, 1), blk(D_KV, 1),
            pl.BlockSpec((1, D_ATTN), lambda bi, i: (0, 0)),
        ],
        out_specs=pl.BlockSpec((1, WINDOW, D_ATTN), lambda bi, i: (bi, i, 0)),
        out_shape=jax.ShapeDtypeStruct((b, s, D_ATTN), BF16),
        scratch_shapes=[pltpu.VMEM((WINDOW, D_ATTN), F32)],
        compiler_params=_cparams(("parallel", "arbitrary")),
    )(sink, qg3, qg3, kv3, kv3, kv3, kv3, gain)


Q_PAD = 16


def _attn_step_body(sink_ref, q_ref, g_ref, kn_ref, vn_ref, ck_ref, cv_ref, gain_ref,
                    out_ref, ko_ref, vo_ref, o_scr, *, nb, t):
    tk = 2 * WINDOW
    ql = lax.broadcasted_iota(jnp.int32, (Q_PAD, tk), 0)
    ks = lax.broadcasted_iota(jnp.int32, (Q_PAD, tk), 1)
    rel = ql + WINDOW - ks
    mask = (rel >= 0) & (rel < WINDOW) & (ql < t)
    pad_k = jnp.zeros((tk - WINDOW - t, D_KV), F32)
    pad_q = jnp.zeros((Q_PAD - t, D_ATTN), F32)

    def one_sequence(b, carry):
        kk = jnp.concatenate([ck_ref[b], kn_ref[b], pad_k], axis=0)
        vv = jnp.concatenate([cv_ref[b], vn_ref[b], pad_k], axis=0)
        q = jnp.concatenate([q_ref[b], pad_q], axis=0)
        _attend(q, kk, vv, mask, sink_ref, o_scr.at[b])
        out_ref[b] = _gate_norm(o_scr[b, 0:t, :], g_ref[b], gain_ref[...])
        ko_ref[b] = kk[t:t + WINDOW, :]
        vo_ref[b] = vv[t:t + WINDOW, :]
        return carry

    lax.fori_loop(0, nb, one_sequence, 0, unroll=2)


def _attn_step(qg3, kv3, cache_k, cache_v, sink, gain, nb):
    n, t, _ = qg3.shape
    blk = lambda w, col: pl.BlockSpec((nb, t, w), lambda i: (i, 0, col))
    cache = pl.BlockSpec((nb, WINDOW, D_KV), lambda i: (i, 0, 0))
    return pl.pallas_call(
        functools.partial(_attn_step_body, nb=nb, t=t),
        name="attn_step",
        grid=(n // nb,),
        in_specs=[
            pl.BlockSpec(memory_space=pltpu.SMEM),
            blk(D_ATTN, 0), blk(D_ATTN, 1),
            blk(D_KV, 0), blk(D_KV, 1),
            cache,
            cache,
            pl.BlockSpec((1, D_ATTN), lambda i: (0, 0)),
        ],
        out_specs=[pl.BlockSpec((nb, t, D_ATTN), lambda i: (i, 0, 0)), cache, cache],
        out_shape=[
            jax.ShapeDtypeStruct((n, t, D_ATTN), F32),
            jax.ShapeDtypeStruct((n, WINDOW, D_KV), F32),
            jax.ShapeDtypeStruct((n, WINDOW, D_KV), F32),
        ],
        scratch_shapes=[pltpu.VMEM((nb, Q_PAD, D_ATTN), F32)],
        compiler_params=_cparams(("parallel",)),
    )(sink, qg3, qg3, kv3, kv3, cache_k, cache_v, gain)


ROWS = 128


def _expand_heads(v):
    r = v.shape[0]
    lo = lax.broadcasted_iota(jnp.int32, (r, LANES), 1) < HEAD_DIM
    cols = []
    for c in range(D_SSM // LANES):
        va = jnp.broadcast_to(v[:, 2 * c:2 * c + 1], (r, LANES))
        vb = jnp.broadcast_to(v[:, 2 * c + 1:2 * c + 2], (r, LANES))
        cols.append(jnp.where(lo, va, vb))
    return jnp.concatenate(cols, axis=1)


def _conv_taps(load_rows, w_ref, b_ref):
    acc = b_ref[...] + w_ref[CONV_K - 1:CONV_K, :] * load_rows(CONV_K - 1)
    for k in range(CONV_K - 1):
        acc = acc + w_ref[k:k + 1, :] * load_rows(k)
    return _silu(acc)


def _conv_rolled(x, tail_scr, w_ref, b_ref):
    taps = CONV_K - 1
    width = x.shape[1]
    tiles = ROWS // SUBLANES
    with_prev = jnp.concatenate([tail_scr[...], x], axis=0).reshape(tiles + 1, SUBLANES, width)
    row = lax.broadcasted_iota(jnp.int32, (tiles, SUBLANES, width), 1)
    acc = b_ref[...] + w_ref[taps:taps + 1, :] * x
    for j in range(1, CONV_K):
        rot = pltpu.roll(with_prev, j, axis=1)
        shifted = jnp.where(row < j, rot[:tiles], rot[1:]).reshape(ROWS, width)
        acc = acc + w_ref[taps - j:taps - j + 1, :] * shifted
    tail_scr[...] = x[ROWS - SUBLANES:, :]
    return _silu(acc)


def _ssd_intra(xs, bm, cm, dt_raw, dtb_ref, alog_ref, causal, same, ybuf):
    dt = jax.nn.softplus(dt_raw + dtb_ref[...])
    a_neg = -jnp.exp(alog_ref[...])
    dta = dt * (a_neg * LOG2E)
    hi = lax.Precision.HIGHEST
    a = jnp.dot(causal.astype(F32), dta, precision=hi, preferred_element_type=F32)
    tot = jnp.dot(same.astype(F32), dta, precision=hi, preferred_element_type=F32)
    r_t = (a - jnp.log2(dt)).T
    lo = lax.broadcasted_iota(jnp.int32, (ROWS, LANES), 1) < HEAD_DIM
    heads_per_group = N_HEADS_SSM // N_GROUPS
    for g in range(N_GROUPS):
        cg = cm[:, g * D_STATE:(g + 1) * D_STATE].astype(BF16)
        bg = bm[:, g * D_STATE:(g + 1) * D_STATE].astype(BF16)
        cb = lax.dot_general(cg, bg, (((1,), (1,)), ((), ())), preferred_element_type=F32)
        for j in range(heads_per_group // 2):
            c = g * (heads_per_group // 2) + j
            ms = []
            for e in range(2):
                h = 2 * c + e
                diff = jnp.broadcast_to(a[:, h:h + 1], (ROWS, ROWS)) - r_t[h:h + 1, :]
                ms.append(cb * jnp.exp2(jnp.where(causal, diff, -jnp.inf)))
            lhs = jnp.concatenate(ms, axis=1).astype(BF16)
            xc = xs[:, c * LANES:(c + 1) * LANES]
            zero = jnp.zeros_like(xc)
            rhs = jnp.concatenate([jnp.where(lo, xc, zero), jnp.where(lo, zero, xc)], axis=0).astype(BF16)
            ybuf[:, c * LANES:(c + 1) * LANES] = jnp.dot(lhs, rhs, preferred_element_type=F32)
    e_expa = _expand_heads(jnp.exp2(a))
    e_w = _expand_heads(jnp.exp2(tot - a) * dt)
    return e_expa, e_w


def _gated_group_norm(y, z, gain_ref, out_ref_store):
    yz = y * _silu(z)
    for g in range(N_GROUPS):
        blk = yz[:, g * GROUP_W:(g + 1) * GROUP_W]
        nrm = blk * lax.rsqrt(jnp.mean(blk * blk, axis=-1, keepdims=True) + EPS)
        out_ref_store(g, nrm * gain_ref[:, g * GROUP_W:(g + 1) * GROUP_W])


def _ssd_prompt_body(z_ref, xs_ref, b_ref, c_ref, dt_ref,
                     cwx, cwb, cwc, cbx, cbb, cbc, dtb_ref, alog_ref, dskip_ref, gain_ref,
                     out_ref, cvx_ref, cvb_ref, cvc_ref, st_ref,
                     tlx, tlb, tlc, h_t, ybuf):
    ci = pl.program_id(1)
    last = pl.num_programs(1) - 1

    @pl.when(ci == 0)
    def _():
        tlx[...] = jnp.zeros_like(tlx)
        tlb[...] = jnp.zeros_like(tlb)
        tlc[...] = jnp.zeros_like(tlc)
        h_t[...] = jnp.zeros_like(h_t)

    xs = _conv_rolled(xs_ref[0], tlx, cwx, cbx)
    bm = _conv_rolled(b_ref[0], tlb, cwb, cbb)
    cm = _conv_rolled(c_ref[0], tlc, cwc, cbc)

    rl = lax.broadcasted_iota(jnp.int32, (ROWS, ROWS), 0)
    cs = lax.broadcasted_iota(jnp.int32, (ROWS, ROWS), 1)
    causal = cs <= rl
    same = cs >= 0
    e_expa, e_w = _ssd_intra(xs, bm, cm, dt_ref[0], dtb_ref, alog_ref, causal, same, ybuf)

    xw = (xs * e_w).astype(BF16)
    for g in range(N_GROUPS):
        gs = slice(g * GROUP_W, (g + 1) * GROUP_W)
        ds = slice(g * D_STATE, (g + 1) * D_STATE)
        h_prev = h_t[:, gs]
        y_inter = jnp.dot(cm[:, ds].astype(BF16), h_prev.astype(BF16), preferred_element_type=F32)
        ybuf[:, gs] = ybuf[:, gs] + y_inter * e_expa[:, gs]
        s_t = jnp.dot(bm[:, ds].T.astype(BF16), xw[:, gs], preferred_element_type=F32)
        h_t[:, gs] = h_prev * e_expa[ROWS - 1:ROWS, gs] + s_t

    y = ybuf[...] + dskip_ref[...] * xs

    def store(g, v):
        out_ref[0, :, g * GROUP_W:(g + 1) * GROUP_W] = v.astype(BF16)
    _gated_group_norm(y, z_ref[0].astype(F32), gain_ref, store)

    @pl.when(ci == last)
    def _():
        first = SUBLANES - (CONV_K - 1)
        cvx_ref[0] = tlx[first:SUBLANES, :]
        cvb_ref[0] = tlb[first:SUBLANES, :]
        cvc_ref[0] = tlc[first:SUBLANES, :]
        st_ref[0] = h_t[...].T


def _ssd_prompt(z3, xbc3, dt3, p):
    b, s, _ = z3.shape
    nc = s // ROWS
    blk = lambda w, col: pl.BlockSpec((1, ROWS, w), lambda bi, ci: (bi, ci, col))
    par = lambda r, w: pl.BlockSpec((r, w), lambda bi, ci: (0, 0))
    per_b = lambda r, w: pl.BlockSpec((1, r, w), lambda bi, ci: (bi, 0, 0))
    tail = CONV_K - 1
    return pl.pallas_call(
        _ssd_prompt_body,
        name="ssd_prompt",
        grid=(b, nc),
        in_specs=[
            blk(D_SSM, 0), blk(D_SSM, 0),
            blk(BC_W, D_SSM // BC_W), blk(BC_W, D_SSM // BC_W + 1),
            blk(DT_W, 0),
            par(CONV_K, D_SSM), par(CONV_K, BC_W), par(CONV_K, BC_W),
            par(1, D_SSM), par(1, BC_W), par(1, BC_W),
            par(1, DT_W), par(1, DT_W), par(1, D_SSM), par(1, D_SSM),
        ],
        out_specs=[
            blk(D_SSM, 0),
            per_b(tail, D_SSM), per_b(tail, BC_W), per_b(tail, BC_W),
            per_b(D_SSM, D_STATE),
        ],
        out_shape=[
            jax.ShapeDtypeStruct((b, s, D_SSM), BF16),
            jax.ShapeDtypeStruct((b, tail, D_SSM), F32),
            jax.ShapeDtypeStruct((b, tail, BC_W), F32),
            jax.ShapeDtypeStruct((b, tail, BC_W), F32),
            jax.ShapeDtypeStruct((b, D_SSM, D_STATE), F32),
        ],
        scratch_shapes=[
            pltpu.VMEM((SUBLANES, D_SSM), F32),
            pltpu.VMEM((SUBLANES, BC_W), F32),
            pltpu.VMEM((SUBLANES, BC_W), F32),
            pltpu.VMEM((D_STATE, D_SSM), F32),
            pltpu.VMEM((ROWS, D_SSM), F32),
        ],
        compiler_params=_cparams(("parallel", "arbitrary")),
    )(z3, xbc3, xbc3, xbc3, dt3,
      p["cwx"], p["cwb"], p["cwc"], p["cbx"], p["cbb"], p["cbc"],
      p["dtb"], p["alog"], p["dskip"], p["ssm_gain"])


def _ssd_step_body(z_ref, xs_ref, b_ref, c_ref, dt_ref, sx_ref, sb_ref, sc_ref, h0_ref,
                   cwx, cwb, cwc, cbx, cbb, cbc, dtb_ref, alog_ref, dskip_ref, gain_ref,
                   out_ref, cvx_ref, cvb_ref, cvc_ref, st_ref,
                   xpx, xpb, xpc, ybuf, cm_scr, ea_scr, ea_t_scr, xw_t_scr, *, nb, t):
    real = nb * t
    tail = CONV_K - 1
    lo_row = SUBLANES - tail

    def conv(xp, new_ref, st_in_ref, st_out_ref, w_ref, b_ref):
        xp[:, lo_row:SUBLANES, :] = st_in_ref[...]
        xp[:, SUBLANES:SUBLANES + t, :] = new_ref[...]
        st_out_ref[...] = xp[:, SUBLANES + t - tail:SUBLANES + t, :]
        w = new_ref.shape[-1]
        act = _conv_taps(lambda k: xp[:, lo_row + k:lo_row + k + t, :].reshape(real, w), w_ref, b_ref)
        return jnp.concatenate([act, jnp.zeros((ROWS - real, w), F32)], axis=0)

    xs = conv(xpx, xs_ref, sx_ref, cvx_ref, cwx, cbx)
    bm = conv(xpb, b_ref, sb_ref, cvb_ref, cwb, cbb)
    cm = conv(xpc, c_ref, sc_ref, cvc_ref, cwc, cbc)
    dt_raw = jnp.concatenate([dt_ref[...].reshape(real, DT_W), jnp.zeros((ROWS - real, DT_W), F32)], axis=0)

    rl = lax.broadcasted_iota(jnp.int32, (ROWS, ROWS), 0)
    cs = lax.broadcasted_iota(jnp.int32, (ROWS, ROWS), 1)
    same = (rl // t) == (cs // t)
    causal = same & (cs <= rl)
    e_expa, e_w = _ssd_intra(xs, bm, cm, dt_raw, dtb_ref, alog_ref, causal, same, ybuf)

    cm_scr[...] = cm
    ea_scr[...] = e_expa
    ea_t_scr[...] = e_expa.T
    xw_t_scr[...] = (xs * e_w).T.astype(BF16)
    row_id = lax.broadcasted_iota(jnp.int32, (ROWS, D_STATE), 0)
    pair = 2 * SUBLANES
    for b in range(nb):
        in_seq = (row_id >= b * t) & (row_id < (b + 1) * t)
        r0 = b * t
        last_col = r0 + t - 1
        for g in range(N_GROUPS):
            gs = slice(g * GROUP_W, (g + 1) * GROUP_W)
            ds = slice(g * D_STATE, (g + 1) * D_STATE)
            h0 = h0_ref[b, gs, :]
            c_rows = cm_scr[r0:r0 + pair, ds].astype(BF16)
            y_inter = lax.dot_general(c_rows, h0.astype(BF16), (((1,), (1,)), ((), ())),
                                      preferred_element_type=F32)
            ybuf[r0:r0 + t, gs] = ybuf[r0:r0 + t, gs] + y_inter[0:t] * ea_scr[r0:r0 + t, gs]
            b_rows = jnp.where(in_seq, bm[:, ds], 0.0).astype(BF16)
            s_new = jnp.dot(xw_t_scr[gs, :], b_rows, preferred_element_type=F32)
            decay = jnp.broadcast_to(ea_t_scr[gs, last_col:last_col + 1], (GROUP_W, D_STATE))
            st_ref[b, gs, :] = h0 * decay + s_new

    y = ybuf[0:real, :] + dskip_ref[...] * xs[0:real, :]
    z = z_ref[...].reshape(real, D_SSM)

    def store(g, v):
        out_ref[:, :, g * GROUP_W:(g + 1) * GROUP_W] = v.reshape(nb, t, GROUP_W)
    _gated_group_norm(y, z, gain_ref, store)


def _ssd_step(z3, xbc3, dt3, conv_state, h0, p, nb):
    n, t, _ = z3.shape
    tail = CONV_K - 1
    blk = lambda w, col: pl.BlockSpec((nb, t, w), lambda i: (i, 0, col))
    cst = lambda w, col: pl.BlockSpec((nb, tail, w), lambda i: (i, 0, col))
    par = lambda r, w: pl.BlockSpec((r, w), lambda i: (0, 0))
    state = pl.BlockSpec((nb, D_SSM, D_STATE), lambda i: (i, 0, 0))
    return pl.pallas_call(
        functools.partial(_ssd_step_body, nb=nb, t=t),
        name="ssd_step",
        grid=(n // nb,),
        in_specs=[
            blk(D_SSM, 0), blk(D_SSM, 0),
            blk(BC_W, D_SSM // BC_W), blk(BC_W, D_SSM // BC_W + 1),
            blk(DT_W, 0),
            cst(D_SSM, 0), cst(BC_W, D_SSM // BC_W), cst(BC_W, D_SSM // BC_W + 1),
            state,
            par(CONV_K, D_SSM), par(CONV_K, BC_W), par(CONV_K, BC_W),
            par(1, D_SSM), par(1, BC_W), par(1, BC_W),
            par(1, DT_W), par(1, DT_W), par(1, D_SSM), par(1, D_SSM),
        ],
        out_specs=[
            blk(D_SSM, 0),
            cst(D_SSM, 0), cst(BC_W, 0), cst(BC_W, 0),
            state,
        ],
        out_shape=[
            jax.ShapeDtypeStruct((n, t, D_SSM), F32),
            jax.ShapeDtypeStruct((n, tail, D_SSM), F32),
            jax.ShapeDtypeStruct((n, tail, BC_W), F32),
            jax.ShapeDtypeStruct((n, tail, BC_W), F32),
            jax.ShapeDtypeStruct((n, D_SSM, D_STATE), F32),
        ],
        scratch_shapes=[
            pltpu.VMEM((nb, SUBLANES + t, D_SSM), F32),
            pltpu.VMEM((nb, SUBLANES + t, BC_W), F32),
            pltpu.VMEM((nb, SUBLANES + t, BC_W), F32),
            pltpu.VMEM((ROWS, D_SSM), F32),
            pltpu.VMEM((ROWS, BC_W), F32),
            pltpu.VMEM((ROWS, D_SSM), F32),
            pltpu.VMEM((D_SSM, ROWS), F32),
            pltpu.VMEM((D_SSM, ROWS), BF16),
        ],
        compiler_params=_cparams(("parallel",)),
    )(z3, xbc3, xbc3, xbc3, dt3, conv_state, conv_state, conv_state, h0,
      p["cwx"], p["cwb"], p["cwc"], p["cbx"], p["cbb"], p["cbc"],
      p["dtb"], p["alog"], p["dskip"], p["ssm_gain"])


def _layer_params(norm_pre, w_in, attn_sink, attn_norm, conv_w, conv_b, dt_bias, a_log, d_skip,
                  ssm_norm, w_out, norm_post):
    w_attn = w_in[:, :WA_COLS].astype(BF16)
    w_ssm = jnp.pad(w_in[:, WA_COLS:], ((0, 0), (0, DT_W - N_HEADS_SSM))).astype(BF16)
    pad_h = lambda v: jnp.pad(v, (0, DT_W - N_HEADS_SSM)).reshape(1, DT_W)
    return dict(
        norm_pre=norm_pre.reshape(1, D_MODEL), w_attn=w_attn, w_ssm=w_ssm,
        sink=attn_sink, attn_gain=attn_norm.reshape(1, D_ATTN),
        cwx=conv_w[:, :D_SSM], cwb=conv_w[:, D_SSM:D_SSM + BC_W], cwc=conv_w[:, D_SSM + BC_W:],
        cbx=conv_b[:D_SSM].reshape(1, -1), cbb=conv_b[D_SSM:D_SSM + BC_W].reshape(1, -1),
        cbc=conv_b[D_SSM + BC_W:].reshape(1, -1),
        dtb=pad_h(dt_bias), alog=pad_h(a_log),
        dskip=jnp.broadcast_to(d_skip[:, None], (N_HEADS_SSM, HEAD_DIM)).reshape(1, D_SSM),
        ssm_gain=ssm_norm.reshape(1, D_SSM),
        w_out=w_out.astype(BF16),
        norm_post=norm_post.reshape(1, D_MODEL),
    )


IN_TM, OUT_TM = 512, 512
STEP_NB_ATTN, STEP_NB_SSD = 8, 8


def _layer(x, caches, p):
    n, t, _ = x.shape
    x2d = x.reshape(n * t, D_MODEL)
    act_dtype = BF16 if caches is None else F32
    qg, kv, z, xbc, dt = _inproj(x2d, p["norm_pre"], p["w_attn"], p["w_ssm"], IN_TM, act_dtype)
    qg3, kv3 = qg.reshape(n, t, 2 * D_ATTN), kv.reshape(n, t, 2 * D_KV)
    z3, xbc3, dt3 = z.reshape(n, t, D_SSM), xbc.reshape(n, t, CONV_DIM), dt.reshape(n, t, DT_W)
    if caches is None:
        attn = _attn_prompt(qg3, kv3, p["sink"], p["attn_gain"])
        new_k = kv3[:, t - WINDOW:, :D_KV]
        new_v = kv3[:, t - WINDOW:, D_KV:]
        ssm, cvx, cvb, cvc, h_new = _ssd_prompt(z3, xbc3, dt3, p)
    else:
        kbuf, vbuf, conv_buf, h0 = caches
        attn, new_k, new_v = _attn_step(qg3, kv3, kbuf.reshape(n, WINDOW, D_KV), vbuf.reshape(n, WINDOW, D_KV),
                                        p["sink"], p["attn_gain"], STEP_NB_ATTN)
        ssm, cvx, cvb, cvc, h_new = _ssd_step(z3, xbc3, dt3, conv_buf, h0.reshape(n, D_SSM, D_STATE), p,
                                              STEP_NB_SSD)
    y = _outproj(attn.reshape(n * t, D_ATTN), ssm.reshape(n * t, D_SSM), x2d,
                 p["w_out"], p["norm_post"], OUT_TM)
    return (y.reshape(n, t, D_MODEL),
            new_k.reshape(n, WINDOW, N_KV_HEADS, HEAD_DIM), new_v.reshape(n, WINDOW, N_KV_HEADS, HEAD_DIM),
            jnp.concatenate([cvx, cvb, cvc], axis=-1),
            h_new.reshape(n, N_HEADS_SSM, HEAD_DIM, D_STATE))


def kernel(x_prompt, x_sample, cache_k, cache_v, state_conv, state_ssm, norm_pre, w_in, attn_sink,
           attn_norm, conv_w, conv_b, dt_bias, a_log, d_skip, ssm_norm, w_out, norm_post):
    depth = w_in.shape[0]
    yp, ys = x_prompt, x_sample
    outs_p, outs_s = [], []
    for l in range(depth):
        p = _layer_params(norm_pre[l], w_in[l], attn_sink[l], attn_norm[l], conv_w[l], conv_b[l],
                          dt_bias[l], a_log[l], d_skip[l], ssm_norm[l], w_out[l], norm_post[l])
        yp, *rest_p = _layer(yp, None, p)
        ys, *rest_s = _layer(ys, (cache_k[l], cache_v[l], state_conv[l], state_ssm[l]), p)
        outs_p.append(rest_p)
        outs_s.append(rest_s)
    stack = lambda outs, i: jnp.stack([o[i] for o in outs])
    return (yp, ys,
            stack(outs_p, 0), stack(outs_p, 1), stack(outs_p, 2), stack(outs_p, 3),
            stack(outs_s, 0), stack(outs_s, 1), stack(outs_s, 2), stack(outs_s, 3))
```

```python
import functools

import jax
import jax.numpy as jnp
from jax import lax
from jax.experimental import pallas as pl
from jax.experimental.pallas import tpu as pltpu

F32 = jnp.float32
BF16 = jnp.bfloat16

D_MODEL = 2048
HEAD_DIM = 64
N_KV_HEADS = 4
D_ATTN = 2048
D_KV = 256
WINDOW = 128
D_SSM = 2048
N_HEADS_SSM = 32
N_GROUPS = 4
D_STATE = 128
CONV_K = 4
GROUP_W = D_SSM // N_GROUPS
BC_W = N_GROUPS * D_STATE
CONV_DIM = D_SSM + 2 * BC_W
EPS = 1e-6
LOG2E = 1.4426950408889634
LANES = 128
SUBLANES = 8

DT_W = LANES
WA_COLS = 2 * D_ATTN + 2 * D_KV
WS_COLS = D_SSM + CONV_DIM + DT_W

VMEM_LIMIT = 56 * 1024 * 1024


def _silu(x):
    h = 0.5 * x
    return h + h * jnp.tanh(h)


def _cparams(sem):
    return pltpu.CompilerParams(dimension_semantics=sem, vmem_limit_bytes=VMEM_LIMIT)


def _inproj_attn_body(x_ref, g_ref, w_ref, h_ref, qg_ref, kv_ref, *, row_chunk):
    tm = x_ref.shape[0]

    def norm_rows(r, carry):
        rows = pl.ds(pl.multiple_of(r * row_chunk, row_chunk), row_chunk)
        x = x_ref[rows, :]
        y = x * lax.rsqrt(jnp.mean(x * x, axis=-1, keepdims=True) + EPS)
        h_ref[rows, :] = (y * g_ref[...]).astype(BF16)
        return carry
    lax.fori_loop(0, tm // row_chunk, norm_rows, 0)

    h = h_ref[...]
    dot = lambda lo, hi: jnp.dot(h, w_ref[:, lo:hi], preferred_element_type=F32)
    qg_ref[:, 0:D_ATTN] = dot(0, D_ATTN).astype(qg_ref.dtype)
    kv_ref[...] = dot(D_ATTN, D_ATTN + 2 * D_KV)
    qg_ref[:, D_ATTN:2 * D_ATTN] = dot(D_ATTN + 2 * D_KV, WA_COLS).astype(qg_ref.dtype)


def _inproj_ssm_body(h_ref, w_ref, z_ref, xbc_ref, dt_ref):
    _ssm_projections(h_ref[...], w_ref, z_ref, xbc_ref, dt_ref)


PROJ_CHUNK = 512


def _ssm_projection_steps(h, w_ref, z_ref, xbc_ref, dt_ref):
    lead = z_ref.shape[:-1]
    rows = (slice(None),) * len(lead)

    def slab(out_ref, out_lo, w_lo, width):
        def run():
            r = jnp.dot(h, w_ref[:, w_lo:w_lo + width], preferred_element_type=F32)
            out_ref[rows + (slice(out_lo, out_lo + width),)] = r.astype(out_ref.dtype).reshape(*lead, width)
        return run

    steps = [slab(z_ref, c, c, PROJ_CHUNK) for c in range(0, D_SSM, PROJ_CHUNK)]
    steps += [slab(xbc_ref, c, D_SSM + c, PROJ_CHUNK) for c in range(0, CONV_DIM, PROJ_CHUNK)]
    steps.append(slab(dt_ref, 0, D_SSM + CONV_DIM, DT_W))
    return steps


def _ssm_projections(h, w_ref, z_ref, xbc_ref, dt_ref):
    for run in _ssm_projection_steps(h, w_ref, z_ref, xbc_ref, dt_ref):
        run()


def _inproj_attn(x2d, gain, w_attn, tm, act_dtype):
    m = x2d.shape[0]
    tm = min(tm, m)
    row = lambda w: pl.BlockSpec((tm, w), lambda i: (i, 0))
    resident = lambda r, w: pl.BlockSpec((r, w), lambda i: (0, 0), pipeline_mode=pl.Buffered(1))
    return pl.pallas_call(
        functools.partial(_inproj_attn_body, row_chunk=min(128, tm)),
        name="inproj_attn",
        grid=(m // tm,),
        in_specs=[row(D_MODEL), resident(1, D_MODEL), resident(D_MODEL, WA_COLS)],
        out_specs=[row(D_MODEL), row(2 * D_ATTN), row(2 * D_KV)],
        out_shape=[jax.ShapeDtypeStruct((m, D_MODEL), BF16),
                   jax.ShapeDtypeStruct((m, 2 * D_ATTN), act_dtype),
                   jax.ShapeDtypeStruct((m, 2 * D_KV), F32)],
        compiler_params=_cparams(("parallel",)),
    )(x2d, gain, w_attn)


def _inproj_ssm(h, w_ssm, tm, act_dtype):
    m = h.shape[0]
    tm = min(tm, m)
    row = lambda w: pl.BlockSpec((tm, w), lambda i: (i, 0))
    return pl.pallas_call(
        _inproj_ssm_body,
        name="inproj_ssm",
        grid=(m // tm,),
        in_specs=[row(D_MODEL), pl.BlockSpec((D_MODEL, WS_COLS), lambda i: (0, 0), pipeline_mode=pl.Buffered(1))],
        out_specs=[row(D_SSM), row(CONV_DIM), row(DT_W)],
        out_shape=[jax.ShapeDtypeStruct((m, D_SSM), act_dtype),
                   jax.ShapeDtypeStruct((m, CONV_DIM), F32),
                   jax.ShapeDtypeStruct((m, DT_W), F32)],
        compiler_params=_cparams(("parallel",)),
    )(h, w_ssm)


def _outproj_body(a_ref, s_ref, x_ref, wa_ref, ws_ref, g_ref, y_ref):
    o = jnp.dot(a_ref[...].astype(BF16), wa_ref[...], preferred_element_type=F32)
    o = o + jnp.dot(s_ref[...].astype(BF16), ws_ref[...], preferred_element_type=F32)
    o = o * lax.rsqrt(jnp.mean(o * o, axis=-1, keepdims=True) + EPS)
    y_ref[...] = x_ref[...] + o * g_ref[...]


def _outproj(attn, ssm, x2d, w_out, gain, tm):
    m = x2d.shape[0]
    tm = min(tm, m)
    row = lambda i: (i, 0)
    once = dict(pipeline_mode=pl.Buffered(1))
    return pl.pallas_call(
        _outproj_body,
        name="outproj",
        grid=(m // tm,),
        in_specs=[
            pl.BlockSpec((tm, D_ATTN), row),
            pl.BlockSpec((tm, D_SSM), row),
            pl.BlockSpec((tm, D_MODEL), row),
            pl.BlockSpec((D_ATTN, D_MODEL), lambda i: (0, 0), **once),
            pl.BlockSpec((D_SSM, D_MODEL), lambda i: (1, 0), **once),
            pl.BlockSpec((1, D_MODEL), lambda i: (0, 0), **once),
        ],
        out_specs=pl.BlockSpec((tm, D_MODEL), row),
        out_shape=jax.ShapeDtypeStruct((m, D_MODEL), F32),
        compiler_params=_cparams(("parallel",)),
    )(attn, ssm, x2d, w_out, w_out, gain)


def _head_pair_blockdiag(slab):
    lo = lax.broadcasted_iota(jnp.int32, slab.shape, 1) < HEAD_DIM
    rolled = pltpu.roll(slab, HEAD_DIM, axis=1)
    zero = jnp.zeros_like(slab)
    bd_a = jnp.concatenate([jnp.where(lo, slab, zero), jnp.where(lo, zero, rolled)], axis=0)
    bd_b = jnp.concatenate([jnp.where(lo, rolled, zero), jnp.where(lo, zero, slab)], axis=0)
    return bd_a.astype(BF16), bd_b.astype(BF16)


def _attend(q, kk, vv, mask, sink_ref, o_scr, after_group=None):
    tq, tk = mask.shape
    kk = kk * (HEAD_DIM ** -0.5 * LOG2E)
    kbd, vbd = [], []
    for c in range(D_KV // LANES):
        kbd.extend(_head_pair_blockdiag(kk[:, c * LANES:(c + 1) * LANES]))
        vbd.extend(_head_pair_blockdiag(vv[:, c * LANES:(c + 1) * LANES]))
    lo = lax.broadcasted_iota(jnp.int32, (tq, LANES), 1) < HEAD_DIM
    pairs_per_group = (D_ATTN // N_KV_HEADS) // LANES
    for g in range(N_KV_HEADS):
        cols = [g * pairs_per_group + j for j in range(pairs_per_group)]
        q_stack = jnp.concatenate([q[:, c * LANES:(c + 1) * LANES] for c in cols], axis=0).astype(BF16)
        s = lax.dot_general(q_stack, kbd[g], (((1,), (1,)), ((), ())), preferred_element_type=F32)
        p_rows, inv_rows = [], []
        for j, col in enumerate(cols):
            ps, invs = [], []
            for e in range(2):
                se = jnp.where(mask, s[j * tq:(j + 1) * tq, e * tk:(e + 1) * tk], -jnp.inf)
                snk = sink_ref[2 * col + e] * LOG2E
                mx = jnp.maximum(jnp.max(se, axis=-1, keepdims=True), snk)
                p = jnp.exp2(se - mx)
                den = jnp.sum(p, axis=-1, keepdims=True) + jnp.exp2(snk - mx)
                ps.append(p)
                invs.append(1.0 / den)
            p_rows.append(jnp.concatenate(ps, axis=1).astype(BF16))
            inv_rows.append(jnp.where(lo, invs[0], invs[1]))
        o = jnp.dot(jnp.concatenate(p_rows, axis=0), vbd[g], preferred_element_type=F32)
        for j, col in enumerate(cols):
            o_scr[:, col * LANES:(col + 1) * LANES] = o[j * tq:(j + 1) * tq, :] * inv_rows[j]
        if after_group is not None:
            after_group(g)


def _gate_norm(o, gate, gain):
    y = o * _silu(gate)
    y = y * lax.rsqrt(jnp.mean(y * y, axis=-1, keepdims=True) + EPS)
    return y * gain


def _attn_prompt_body(sink_ref, q_ref, g_ref, kp_ref, kc_ref, vp_ref, vc_ref, gain_ref, h_ref, w_ref,
                      out_ref, z_ref, xbc_ref, dt_ref, o_scr, *, blocks):
    i = pl.program_id(1)
    ql = lax.broadcasted_iota(jnp.int32, (WINDOW, 2 * WINDOW), 0)
    ks = lax.broadcasted_iota(jnp.int32, (WINDOW, 2 * WINDOW), 1)
    rel = ql + WINDOW - ks
    band = (rel >= 0) & (rel < WINDOW)
    steps = _ssm_projection_steps(h_ref[0], w_ref, z_ref, xbc_ref, dt_ref)
    slots = blocks * N_KV_HEADS

    def interleave(slot):
        for run in steps[slot * len(steps) // slots:(slot + 1) * len(steps) // slots]:
            run()

    for sub in range(blocks):
        rows = slice(sub * WINDOW, (sub + 1) * WINDOW)
        if sub == 0:
            k_prev, v_prev = kp_ref[0], vp_ref[0]
            first_key = jnp.where(i > 0, 0, WINDOW)
            mask = band & (ks >= first_key)
        else:
            before = slice((sub - 1) * WINDOW, sub * WINDOW)
            k_prev, v_prev = kc_ref[0, before, :], vc_ref[0, before, :]
            mask = band
        kk = jnp.concatenate([k_prev, kc_ref[0, rows, :]], axis=0)
        vv = jnp.concatenate([v_prev, vc_ref[0, rows, :]], axis=0)
        _attend(q_ref[0, rows, :], kk, vv, mask, sink_ref, o_scr.at[sub],
                after_group=lambda g, sub=sub: interleave(sub * N_KV_HEADS + g))
        out_ref[0, rows, :] = _gate_norm(o_scr[sub], g_ref[0, rows, :].astype(F32), gain_ref[...]).astype(BF16)


def _attn_prompt(qg3, kv3, h3, w_ssm, sink, gain, blocks):
    b, s, _ = qg3.shape
    tm = blocks * WINDOW
    blk = lambda w, col: pl.BlockSpec((1, tm, w), lambda bi, i: (bi, i, col))
    prev = lambda w, col: pl.BlockSpec((1, WINDOW, w), lambda bi, i: (bi, jnp.maximum(i * blocks - 1, 0), col))
    once = dict(pipeline_mode=pl.Buffered(1))
    return pl.pallas_call(
        functools.partial(_attn_prompt_body, blocks=blocks),
        name="attn_prompt",
        grid=(b, s // tm),
        in_specs=[
            pl.BlockSpec(memory_space=pltpu.SMEM),
            blk(D_ATTN, 0), blk(D_ATTN, 1),
            prev(D_KV, 0), blk(D_KV, 0),
            prev(D_KV, 1), blk(D_KV, 1),
            pl.BlockSpec((1, D_ATTN), lambda bi, i: (0, 0), **once),
            blk(D_MODEL, 0),
            pl.BlockSpec((D_MODEL, WS_COLS), lambda bi, i: (0, 0), **once),
        ],
        out_specs=[blk(D_ATTN, 0), blk(D_SSM, 0), blk(CONV_DIM, 0), blk(DT_W, 0)],
        out_shape=[jax.ShapeDtypeStruct((b, s, D_ATTN), BF16),
                   jax.ShapeDtypeStruct((b, s, D_SSM), BF16),
                   jax.ShapeDtypeStruct((b, s, CONV_DIM), F32),
                   jax.ShapeDtypeStruct((b, s, DT_W), F32)],
        scratch_shapes=[pltpu.VMEM((blocks, WINDOW, D_ATTN), F32)],
        compiler_params=_cparams(("parallel", "arbitrary")),
    )(sink, qg3, qg3, kv3, kv3, kv3, kv3, gain, h3, w_ssm)


Q_PAD = 16


def _attn_step_body(sink_ref, q_ref, g_ref, kn_ref, vn_ref, ck_ref, cv_ref, gain_ref,
                    out_ref, ko_ref, vo_ref, o_scr, *, nb, t):
    tk = 2 * WINDOW
    ql = lax.broadcasted_iota(jnp.int32, (Q_PAD, tk), 0)
    ks = lax.broadcasted_iota(jnp.int32, (Q_PAD, tk), 1)
    rel = ql + WINDOW - ks
    mask = (rel >= 0) & (rel < WINDOW) & (ql < t)
    pad_k = jnp.zeros((tk - WINDOW - t, D_KV), F32)
    pad_q = jnp.zeros((Q_PAD - t, D_ATTN), F32)

    def one_sequence(b, carry):
        kk = jnp.concatenate([ck_ref[b], kn_ref[b], pad_k], axis=0)
        vv = jnp.concatenate([cv_ref[b], vn_ref[b], pad_k], axis=0)
        q = jnp.concatenate([q_ref[b], pad_q], axis=0)
        _attend(q, kk, vv, mask, sink_ref, o_scr.at[b])
        out_ref[b] = _gate_norm(o_scr[b, 0:t, :], g_ref[b], gain_ref[...])
        ko_ref[b] = kk[t:t + WINDOW, :]
        vo_ref[b] = vv[t:t + WINDOW, :]
        return carry

    lax.fori_loop(0, nb, one_sequence, 0, unroll=2)


def _attn_step(qg3, kv3, cache_k, cache_v, sink, gain, nb):
    n, t, _ = qg3.shape
    blk = lambda w, col: pl.BlockSpec((nb, t, w), lambda i: (i, 0, col))
    cache = pl.BlockSpec((nb, WINDOW, D_KV), lambda i: (i, 0, 0))
    return pl.pallas_call(
        functools.partial(_attn_step_body, nb=nb, t=t),
        name="attn_step",
        grid=(n // nb,),
        in_specs=[
            pl.BlockSpec(memory_space=pltpu.SMEM),
            blk(D_ATTN, 0), blk(D_ATTN, 1),
            blk(D_KV, 0), blk(D_KV, 1),
            cache,
            cache,
            pl.BlockSpec((1, D_ATTN), lambda i: (0, 0)),
        ],
        out_specs=[pl.BlockSpec((nb, t, D_ATTN), lambda i: (i, 0, 0)), cache, cache],
        out_shape=[
            jax.ShapeDtypeStruct((n, t, D_ATTN), F32),
            jax.ShapeDtypeStruct((n, WINDOW, D_KV), F32),
            jax.ShapeDtypeStruct((n, WINDOW, D_KV), F32),
        ],
        scratch_shapes=[pltpu.VMEM((nb, Q_PAD, D_ATTN), F32)],
        compiler_params=_cparams(("parallel",)),
    )(sink, qg3, qg3, kv3, kv3, cache_k, cache_v, gain)


ROWS = 128


def _expand_heads(v):
    r = v.shape[0]
    lo = lax.broadcasted_iota(jnp.int32, (r, LANES), 1) < HEAD_DIM
    cols = []
    for c in range(D_SSM // LANES):
        va = jnp.broadcast_to(v[:, 2 * c:2 * c + 1], (r, LANES))
        vb = jnp.broadcast_to(v[:, 2 * c + 1:2 * c + 2], (r, LANES))
        cols.append(jnp.where(lo, va, vb))
    return jnp.concatenate(cols, axis=1)


def _conv_taps(load_rows, w_ref, b_ref):
    acc = b_ref[...] + w_ref[CONV_K - 1:CONV_K, :] * load_rows(CONV_K - 1)
    for k in range(CONV_K - 1):
        acc = acc + w_ref[k:k + 1, :] * load_rows(k)
    return _silu(acc)


def _conv_rolled(x, tail_scr, w_ref, b_ref):
    taps = CONV_K - 1
    width = x.shape[1]
    tiles = ROWS // SUBLANES
    with_prev = jnp.concatenate([tail_scr[...], x], axis=0).reshape(tiles + 1, SUBLANES, width)
    row = lax.broadcasted_iota(jnp.int32, (tiles, SUBLANES, width), 1)
    acc = b_ref[...] + w_ref[taps:taps + 1, :] * x
    for j in range(1, CONV_K):
        rot = pltpu.roll(with_prev, j, axis=1)
        shifted = jnp.where(row < j, rot[:tiles], rot[1:]).reshape(ROWS, width)
        acc = acc + w_ref[taps - j:taps - j + 1, :] * shifted
    tail_scr[...] = x[ROWS - SUBLANES:, :]
    return _silu(acc)


def _ssd_intra(xs, bm, cm, dt_raw, dtb_ref, alog_ref, causal, same, ybuf):
    dt = jax.nn.softplus(dt_raw + dtb_ref[...])
    a_neg = -jnp.exp(alog_ref[...])
    dta = dt * (a_neg * LOG2E)
    hi = lax.Precision.HIGHEST
    a = jnp.dot(causal.astype(F32), dta, precision=hi, preferred_element_type=F32)
    tot = jnp.dot(same.astype(F32), dta, precision=hi, preferred_element_type=F32)
    r_t = (a - jnp.log2(dt)).T
    lo = lax.broadcasted_iota(jnp.int32, (ROWS, LANES), 1) < HEAD_DIM
    heads_per_group = N_HEADS_SSM // N_GROUPS
    for g in range(N_GROUPS):
        cg = cm[:, g * D_STATE:(g + 1) * D_STATE].astype(BF16)
        bg = bm[:, g * D_STATE:(g + 1) * D_STATE].astype(BF16)
        cb = lax.dot_general(cg, bg, (((1,), (1,)), ((), ())), preferred_element_type=F32)
        for j in range(heads_per_group // 2):
            c = g * (heads_per_group // 2) + j
            ms = []
            for e in range(2):
                h = 2 * c + e
                diff = jnp.broadcast_to(a[:, h:h + 1], (ROWS, ROWS)) - r_t[h:h + 1, :]
                ms.append(cb * jnp.exp2(jnp.where(causal, diff, -jnp.inf)))
            lhs = jnp.concatenate(ms, axis=1).astype(BF16)
            xc = xs[:, c * LANES:(c + 1) * LANES]
            zero = jnp.zeros_like(xc)
            rhs = jnp.concatenate([jnp.where(lo, xc, zero), jnp.where(lo, zero, xc)], axis=0).astype(BF16)
            ybuf[:, c * LANES:(c + 1) * LANES] = jnp.dot(lhs, rhs, preferred_element_type=F32)
    e_expa = _expand_heads(jnp.exp2(a))
    e_w = _expand_heads(jnp.exp2(tot - a) * dt)
    return e_expa, e_w


def _gated_group_norm(y, z, gain_ref, out_ref_store):
    yz = y * _silu(z)
    for g in range(N_GROUPS):
        blk = yz[:, g * GROUP_W:(g + 1) * GROUP_W]
        nrm = blk * lax.rsqrt(jnp.mean(blk * blk, axis=-1, keepdims=True) + EPS)
        out_ref_store(g, nrm * gain_ref[:, g * GROUP_W:(g + 1) * GROUP_W])


def _ssd_prompt_body(z_ref, xs_ref, b_ref, c_ref, dt_ref,
                     cwx, cwb, cwc, cbx, cbb, cbc, dtb_ref, alog_ref, dskip_ref, gain_ref,
                     out_ref, cvx_ref, cvb_ref, cvc_ref, st_ref,
                     tlx, tlb, tlc, h_t, ybuf):
    ci = pl.program_id(1)
    last = pl.num_programs(1) - 1

    @pl.when(ci == 0)
    def _():
        tlx[...] = jnp.zeros_like(tlx)
        tlb[...] = jnp.zeros_like(tlb)
        tlc[...] = jnp.zeros_like(tlc)
        h_t[...] = jnp.zeros_like(h_t)

    xs = _conv_rolled(xs_ref[0], tlx, cwx, cbx)
    bm = _conv_rolled(b_ref[0], tlb, cwb, cbb)
    cm = _conv_rolled(c_ref[0], tlc, cwc, cbc)

    rl = lax.broadcasted_iota(jnp.int32, (ROWS, ROWS), 0)
    cs = lax.broadcasted_iota(jnp.int32, (ROWS, ROWS), 1)
    causal = cs <= rl
    same = cs >= 0
    e_expa, e_w = _ssd_intra(xs, bm, cm, dt_ref[0], dtb_ref, alog_ref, causal, same, ybuf)

    xw = (xs * e_w).astype(BF16)
    for g in range(N_GROUPS):
        gs = slice(g * GROUP_W, (g + 1) * GROUP_W)
        ds = slice(g * D_STATE, (g + 1) * D_STATE)
        h_prev = h_t[:, gs]
        y_inter = jnp.dot(cm[:, ds].astype(BF16), h_prev.astype(BF16), preferred_element_type=F32)
        ybuf[:, gs] = ybuf[:, gs] + y_inter * e_expa[:, gs]
        s_t = jnp.dot(bm[:, ds].T.astype(BF16), xw[:, gs], preferred_element_type=F32)
        h_t[:, gs] = h_prev * e_expa[ROWS - 1:ROWS, gs] + s_t

    y = ybuf[...] + dskip_ref[...] * xs

    def store(g, v):
        out_ref[0, :, g * GROUP_W:(g + 1) * GROUP_W] = v.astype(BF16)
    _gated_group_norm(y, z_ref[0].astype(F32), gain_ref, store)

    @pl.when(ci == last)
    def _():
        first = SUBLANES - (CONV_K - 1)
        cvx_ref[0] = tlx[first:SUBLANES, :]
        cvb_ref[0] = tlb[first:SUBLANES, :]
        cvc_ref[0] = tlc[first:SUBLANES, :]
        st_ref[0] = h_t[...].T


def _ssd_prompt(z3, xbc3, dt3, p):
    b, s, _ = z3.shape
    nc = s // ROWS
    blk = lambda w, col: pl.BlockSpec((1, ROWS, w), lambda bi, ci: (bi, ci, col))
    par = lambda r, w: pl.BlockSpec((r, w), lambda bi, ci: (0, 0))
    per_b = lambda r, w: pl.BlockSpec((1, r, w), lambda bi, ci: (bi, 0, 0))
    tail = CONV_K - 1
    return pl.pallas_call(
        _ssd_prompt_body,
        name="ssd_prompt",
        grid=(b, nc),
        in_specs=[
            blk(D_SSM, 0), blk(D_SSM, 0),
            blk(BC_W, D_SSM // BC_W), blk(BC_W, D_SSM // BC_W + 1),
            blk(DT_W, 0),
            par(CONV_K, D_SSM), par(CONV_K, BC_W), par(CONV_K, BC_W),
            par(1, D_SSM), par(1, BC_W), par(1, BC_W),
            par(1, DT_W), par(1, DT_W), par(1, D_SSM), par(1, D_SSM),
        ],
        out_specs=[
            blk(D_SSM, 0),
            per_b(tail, D_SSM), per_b(tail, BC_W), per_b(tail, BC_W),
            per_b(D_SSM, D_STATE),
        ],
        out_shape=[
            jax.ShapeDtypeStruct((b, s, D_SSM), BF16),
            jax.ShapeDtypeStruct((b, tail, D_SSM), F32),
            jax.ShapeDtypeStruct((b, tail, BC_W), F32),
            jax.ShapeDtypeStruct((b, tail, BC_W), F32),
            jax.ShapeDtypeStruct((b, D_SSM, D_STATE), F32),
        ],
        scratch_shapes=[
            pltpu.VMEM((SUBLANES, D_SSM), F32),
            pltpu.VMEM((SUBLANES, BC_W), F32),
            pltpu.VMEM((SUBLANES, BC_W), F32),
            pltpu.VMEM((D_STATE, D_SSM), F32),
            pltpu.VMEM((ROWS, D_SSM), F32),
        ],
        compiler_params=_cparams(("parallel", "arbitrary")),
    )(z3, xbc3, xbc3, xbc3, dt3,
      p["cwx"], p["cwb"], p["cwc"], p["cbx"], p["cbb"], p["cbc"],
      p["dtb"], p["alog"], p["dskip"], p["ssm_gain"])


def _ssd_step_body(z_ref, xs_ref, b_ref, c_ref, dt_ref, sx_ref, sb_ref, sc_ref, h0_ref,
                   cwx, cwb, cwc, cbx, cbb, cbc, dtb_ref, alog_ref, dskip_ref, gain_ref,
                   out_ref, cvx_ref, cvb_ref, cvc_ref, st_ref,
                   xpx, xpb, xpc, ybuf, cm_scr, ea_scr, ea_t_scr, xw_t_scr, *, nb, t):
    real = nb * t
    tail = CONV_K - 1
    lo_row = SUBLANES - tail

    def conv(xp, new_ref, st_in_ref, st_out_ref, w_ref, b_ref):
        xp[:, lo_row:SUBLANES, :] = st_in_ref[...]
        xp[:, SUBLANES:SUBLANES + t, :] = new_ref[...]
        st_out_ref[...] = xp[:, SUBLANES + t - tail:SUBLANES + t, :]
        w = new_ref.shape[-1]
        act = _conv_taps(lambda k: xp[:, lo_row + k:lo_row + k + t, :].reshape(real, w), w_ref, b_ref)
        return jnp.concatenate([act, jnp.zeros((ROWS - real, w), F32)], axis=0)

    xs = conv(xpx, xs_ref, sx_ref, cvx_ref, cwx, cbx)
    bm = conv(xpb, b_ref, sb_ref, cvb_ref, cwb, cbb)
    cm = conv(xpc, c_ref, sc_ref, cvc_ref, cwc, cbc)
    dt_raw = jnp.concatenate([dt_ref[...].reshape(real, DT_W), jnp.zeros((ROWS - real, DT_W), F32)], axis=0)

    rl = lax.broadcasted_iota(jnp.int32, (ROWS, ROWS), 0)
    cs = lax.broadcasted_iota(jnp.int32, (ROWS, ROWS), 1)
    same = (rl // t) == (cs // t)
    causal = same & (cs <= rl)
    e_expa, e_w = _ssd_intra(xs, bm, cm, dt_raw, dtb_ref, alog_ref, causal, same, ybuf)

    cm_scr[...] = cm
    ea_scr[...] = e_expa
    ea_t_scr[...] = e_expa.T
    xw_t_scr[...] = (xs * e_w).T.astype(BF16)
    row_id = lax.broadcasted_iota(jnp.int32, (ROWS, D_STATE), 0)
    pair = 2 * SUBLANES
    for b in range(nb):
        in_seq = (row_id >= b * t) & (row_id < (b + 1) * t)
        r0 = b * t
        last_col = r0 + t - 1
        for g in range(N_GROUPS):
            gs = slice(g * GROUP_W, (g + 1) * GROUP_W)
            ds = slice(g * D_STATE, (g + 1) * D_STATE)
            h0 = h0_ref[b, gs, :]
            c_rows = cm_scr[r0:r0 + pair, ds].astype(BF16)
            y_inter = lax.dot_general(c_rows, h0.astype(BF16), (((1,), (1,)), ((), ())),
                                      preferred_element_type=F32)
            ybuf[r0:r0 + t, gs] = ybuf[r0:r0 + t, gs] + y_inter[0:t] * ea_scr[r0:r0 + t, gs]
            b_rows = jnp.where(in_seq, bm[:, ds], 0.0).astype(BF16)
            s_new = jnp.dot(xw_t_scr[gs, :], b_rows, preferred_element_type=F32)
            decay = jnp.broadcast_to(ea_t_scr[gs, last_col:last_col + 1], (GROUP_W, D_STATE))
            st_ref[b, gs, :] = h0 * decay + s_new

    y = ybuf[0:real, :] + dskip_ref[...] * xs[0:real, :]
    z = z_ref[...].reshape(real, D_SSM)

    def store(g, v):
        out_ref[:, :, g * GROUP_W:(g + 1) * GROUP_W] = v.reshape(nb, t, GROUP_W)
    _gated_group_norm(y, z, gain_ref, store)


def _ssd_step(z3, xbc3, dt3, conv_state, h0, p, nb):
    n, t, _ = z3.shape
    tail = CONV_K - 1
    blk = lambda w, col: pl.BlockSpec((nb, t, w), lambda i: (i, 0, col))
    cst = lambda w, col: pl.BlockSpec((nb, tail, w), lambda i: (i, 0, col))
    par = lambda r, w: pl.BlockSpec((r, w), lambda i: (0, 0))
    state = pl.BlockSpec((nb, D_SSM, D_STATE), lambda i: (i, 0, 0))
    return pl.pallas_call(
        functools.partial(_ssd_step_body, nb=nb, t=t),
        name="ssd_step",
        grid=(n // nb,),
        in_specs=[
            blk(D_SSM, 0), blk(D_SSM, 0),
            blk(BC_W, D_SSM // BC_W), blk(BC_W, D_SSM // BC_W + 1),
            blk(DT_W, 0),
            cst(D_SSM, 0), cst(BC_W, D_SSM // BC_W), cst(BC_W, D_SSM // BC_W + 1),
            state,
            par(CONV_K, D_SSM), par(CONV_K, BC_W), par(CONV_K, BC_W),
            par(1, D_SSM), par(1, BC_W), par(1, BC_W),
            par(1, DT_W), par(1, DT_W), par(1, D_SSM), par(1, D_SSM),
        ],
        out_specs=[
            blk(D_SSM, 0),
            cst(D_SSM, 0), cst(BC_W, 0), cst(BC_W, 0),
            state,
        ],
        out_shape=[
            jax.ShapeDtypeStruct((n, t, D_SSM), F32),
            jax.ShapeDtypeStruct((n, tail, D_SSM), F32),
            jax.ShapeDtypeStruct((n, tail, BC_W), F32),
            jax.ShapeDtypeStruct((n, tail, BC_W), F32),
            jax.ShapeDtypeStruct((n, D_SSM, D_STATE), F32),
        ],
        scratch_shapes=[
            pltpu.VMEM((nb, SUBLANES + t, D_SSM), F32),
            pltpu.VMEM((nb, SUBLANES + t, BC_W), F32),
            pltpu.VMEM((nb, SUBLANES + t, BC_W), F32),
            pltpu.VMEM((ROWS, D_SSM), F32),
            pltpu.VMEM((ROWS, BC_W), F32),
            pltpu.VMEM((ROWS, D_SSM), F32),
            pltpu.VMEM((D_SSM, ROWS), F32),
            pltpu.VMEM((D_SSM, ROWS), BF16),
        ],
        compiler_params=_cparams(("parallel",)),
    )(z3, xbc3, xbc3, xbc3, dt3, conv_state, conv_state, conv_state, h0,
      p["cwx"], p["cwb"], p["cwc"], p["cbx"], p["cbb"], p["cbc"],
      p["dtb"], p["alog"], p["dskip"], p["ssm_gain"])


def _layer_params(norm_pre, w_in, attn_sink, attn_norm, conv_w, conv_b, dt_bias, a_log, d_skip,
                  ssm_norm, w_out, norm_post):
    w_attn = w_in[:, :WA_COLS].astype(BF16)
    w_ssm = jnp.pad(w_in[:, WA_COLS:], ((0, 0), (0, DT_W - N_HEADS_SSM))).astype(BF16)
    pad_h = lambda v: jnp.pad(v, (0, DT_W - N_HEADS_SSM)).reshape(1, DT_W)
    return dict(
        norm_pre=norm_pre.reshape(1, D_MODEL), w_attn=w_attn, w_ssm=w_ssm,
        sink=attn_sink, attn_gain=attn_norm.reshape(1, D_ATTN),
        cwx=conv_w[:, :D_SSM], cwb=conv_w[:, D_SSM:D_SSM + BC_W], cwc=conv_w[:, D_SSM + BC_W:],
        cbx=conv_b[:D_SSM].reshape(1, -1), cbb=conv_b[D_SSM:D_SSM + BC_W].reshape(1, -1),
        cbc=conv_b[D_SSM + BC_W:].reshape(1, -1),
        dtb=pad_h(dt_bias), alog=pad_h(a_log),
        dskip=jnp.broadcast_to(d_skip[:, None], (N_HEADS_SSM, HEAD_DIM)).reshape(1, D_SSM),
        ssm_gain=ssm_norm.reshape(1, D_SSM),
        w_out=w_out.astype(BF16),
        norm_post=norm_post.reshape(1, D_MODEL),
    )


IN_TM, OUT_TM = 512, 512
ATTN_BLOCKS = 2
STEP_NB_ATTN, STEP_NB_SSD = 8, 8


def _layer(x, caches, p):
    n, t, _ = x.shape
    x2d = x.reshape(n * t, D_MODEL)
    act_dtype = BF16 if caches is None else F32
    h, qg, kv = _inproj_attn(x2d, p["norm_pre"], p["w_attn"], IN_TM, act_dtype)
    qg3, kv3 = qg.reshape(n, t, 2 * D_ATTN), kv.reshape(n, t, 2 * D_KV)
    if caches is None:
        attn, z3, xbc3, dt3 = _attn_prompt(qg3, kv3, h.reshape(n, t, D_MODEL), p["w_ssm"],
                                           p["sink"], p["attn_gain"], ATTN_BLOCKS)
        new_k = kv3[:, t - WINDOW:, :D_KV]
        new_v = kv3[:, t - WINDOW:, D_KV:]
        ssm, cvx, cvb, cvc, h_new = _ssd_prompt(z3, xbc3, dt3, p)
    else:
        z, xbc, dt = _inproj_ssm(h, p["w_ssm"], IN_TM, act_dtype)
        z3, xbc3, dt3 = z.reshape(n, t, D_SSM), xbc.reshape(n, t, CONV_DIM), dt.reshape(n, t, DT_W)
        kbuf, vbuf, conv_buf, h0 = caches
        attn, new_k, new_v = _attn_step(qg3, kv3, kbuf.reshape(n, WINDOW, D_KV), vbuf.reshape(n, WINDOW, D_KV),
                                        p["sink"], p["attn_gain"], STEP_NB_ATTN)
        ssm, cvx, cvb, cvc, h_new = _ssd_step(z3, xbc3, dt3, conv_buf, h0.reshape(n, D_SSM, D_STATE), p,
                                              STEP_NB_SSD)
    y = _outproj(attn.reshape(n * t, D_ATTN), ssm.reshape(n * t, D_SSM), x2d,
                 p["w_out"], p["norm_post"], OUT_TM)
    return (y.reshape(n, t, D_MODEL),
            new_k.reshape(n, WINDOW, N_KV_HEADS, HEAD_DIM), new_v.reshape(n, WINDOW, N_KV_HEADS, HEAD_DIM),
            jnp.concatenate([cvx, cvb, cvc], axis=-1),
            h_new.reshape(n, N_HEADS_SSM, HEAD_DIM, D_STATE))


def kernel(x_prompt, x_sample, cache_k, cache_v, state_conv, state_ssm, norm_pre, w_in, attn_sink,
           attn_norm, conv_w, conv_b, dt_bias, a_log, d_skip, ssm_norm, w_out, norm_post):
    depth = w_in.shape[0]
    yp, ys = x_prompt, x_sample
    outs_p, outs_s = [], []
    for l in range(depth):
        p = _layer_params(norm_pre[l], w_in[l], attn_sink[l], attn_norm[l], conv_w[l], conv_b[l],
                          dt_bias[l], a_log[l], d_skip[l], ssm_norm[l], w_out[l], norm_post[l])
        yp, *rest_p = _layer(yp, None, p)
        ys, *rest_s = _layer(ys, (cache_k[l], cache_v[l], state_conv[l], state_ssm[l]), p)
        outs_p.append(rest_p)
        outs_s.append(rest_s)
    stack = lambda outs, i: jnp.stack([o[i] for o in outs])
    return (yp, ys,
            stack(outs_p, 0), stack(outs_p, 1), stack(outs_p, 2), stack(outs_p, 3),
            stack(outs_s, 0), stack(outs_s, 1), stack(outs_s, 2), stack(outs_s, 3))
```

```python
import functools

import jax
import jax.numpy as jnp
from jax import lax
from jax.experimental import pallas as pl
from jax.experimental.pallas import tpu as pltpu

F32 = jnp.float32
BF16 = jnp.bfloat16

D_MODEL = 2048
HEAD_DIM = 64
N_KV_HEADS = 4
D_ATTN = 2048
D_KV = 256
WINDOW = 128
D_SSM = 2048
N_HEADS_SSM = 32
N_GROUPS = 4
D_STATE = 128
CONV_K = 4
GROUP_W = D_SSM // N_GROUPS
BC_W = N_GROUPS * D_STATE
CONV_DIM = D_SSM + 2 * BC_W
EPS = 1e-6
LOG2E = 1.4426950408889634
LANES = 128
SUBLANES = 8

DT_W = LANES
WA_COLS = 2 * D_ATTN + 2 * D_KV
WS_COLS = D_SSM + CONV_DIM + DT_W

VMEM_LIMIT = 56 * 1024 * 1024


def _silu(x):
    h = 0.5 * x
    return h + h * jnp.tanh(h)


def _cparams(sem):
    return pltpu.CompilerParams(dimension_semantics=sem, vmem_limit_bytes=VMEM_LIMIT)


def _inproj_attn_body(x_ref, g_ref, w_ref, h_ref, qg_ref, kv_ref, *, row_chunk):
    tm = x_ref.shape[0]

    def norm_rows(r, carry):
        rows = pl.ds(pl.multiple_of(r * row_chunk, row_chunk), row_chunk)
        x = x_ref[rows, :]
        y = x * lax.rsqrt(jnp.mean(x * x, axis=-1, keepdims=True) + EPS)
        h_ref[rows, :] = (y * g_ref[...]).astype(BF16)
        return carry
    lax.fori_loop(0, tm // row_chunk, norm_rows, 0)

    h = h_ref[...]
    dot = lambda lo, hi: jnp.dot(h, w_ref[:, lo:hi], preferred_element_type=F32)
    qg_ref[:, 0:D_ATTN] = dot(0, D_ATTN).astype(qg_ref.dtype)
    kv_ref[...] = dot(D_ATTN, D_ATTN + 2 * D_KV)
    qg_ref[:, D_ATTN:2 * D_ATTN] = dot(D_ATTN + 2 * D_KV, WA_COLS).astype(qg_ref.dtype)


def _inproj_ssm_body(h_ref, w_ref, z_ref, xbc_ref, dt_ref):
    _ssm_projections(h_ref[...], w_ref, z_ref, xbc_ref, dt_ref)


PROJ_CHUNK = 512


def _ssm_projection_steps(h, w_ref, z_ref, xbc_ref, dt_ref):
    lead = z_ref.shape[:-1]
    rows = (slice(None),) * len(lead)

    def slab(out_ref, out_lo, w_lo, width):
        def run():
            r = jnp.dot(h, w_ref[:, w_lo:w_lo + width], preferred_element_type=F32)
            out_ref[rows + (slice(out_lo, out_lo + width),)] = r.astype(out_ref.dtype).reshape(*lead, width)
        return run

    steps = [slab(z_ref, c, c, PROJ_CHUNK) for c in range(0, D_SSM, PROJ_CHUNK)]
    steps += [slab(xbc_ref, c, D_SSM + c, PROJ_CHUNK) for c in range(0, CONV_DIM, PROJ_CHUNK)]
    steps.append(slab(dt_ref, 0, D_SSM + CONV_DIM, DT_W))
    return steps


def _ssm_projections(h, w_ref, z_ref, xbc_ref, dt_ref):
    for run in _ssm_projection_steps(h, w_ref, z_ref, xbc_ref, dt_ref):
        run()


def _inproj_attn(x2d, gain, w_attn, tm, act_dtype):
    m = x2d.shape[0]
    tm = min(tm, m)
    row = lambda w: pl.BlockSpec((tm, w), lambda i: (i, 0))
    resident = lambda r, w: pl.BlockSpec((r, w), lambda i: (0, 0), pipeline_mode=pl.Buffered(1))
    return pl.pallas_call(
        functools.partial(_inproj_attn_body, row_chunk=min(128, tm)),
        name="inproj_attn",
        grid=(m // tm,),
        in_specs=[row(D_MODEL), resident(1, D_MODEL), resident(D_MODEL, WA_COLS)],
        out_specs=[row(D_MODEL), row(2 * D_ATTN), row(2 * D_KV)],
        out_shape=[jax.ShapeDtypeStruct((m, D_MODEL), BF16),
                   jax.ShapeDtypeStruct((m, 2 * D_ATTN), act_dtype),
                   jax.ShapeDtypeStruct((m, 2 * D_KV), F32)],
        compiler_params=_cparams(("parallel",)),
    )(x2d, gain, w_attn)


def _inproj_ssm(h, w_ssm, tm, act_dtype):
    m = h.shape[0]
    tm = min(tm, m)
    row = lambda w: pl.BlockSpec((tm, w), lambda i: (i, 0))
    return pl.pallas_call(
        _inproj_ssm_body,
        name="inproj_ssm",
        grid=(m // tm,),
        in_specs=[row(D_MODEL), pl.BlockSpec((D_MODEL, WS_COLS), lambda i: (0, 0), pipeline_mode=pl.Buffered(1))],
        out_specs=[row(D_SSM), row(CONV_DIM), row(DT_W)],
        out_shape=[jax.ShapeDtypeStruct((m, D_SSM), act_dtype),
                   jax.ShapeDtypeStruct((m, CONV_DIM), F32),
                   jax.ShapeDtypeStruct((m, DT_W), F32)],
        compiler_params=_cparams(("parallel",)),
    )(h, w_ssm)


def _outproj_body(a_ref, s_ref, x_ref, wa_ref, ws_ref, g_ref, y_ref):
    o = jnp.dot(a_ref[...].astype(BF16), wa_ref[...], preferred_element_type=F32)
    o = o + jnp.dot(s_ref[...].astype(BF16), ws_ref[...], preferred_element_type=F32)
    o = o * lax.rsqrt(jnp.mean(o * o, axis=-1, keepdims=True) + EPS)
    y_ref[...] = x_ref[...] + o * g_ref[...]


def _outproj(attn, ssm, x2d, w_out, gain, tm):
    m = x2d.shape[0]
    tm = min(tm, m)
    row = lambda i: (i, 0)
    once = dict(pipeline_mode=pl.Buffered(1))
    return pl.pallas_call(
        _outproj_body,
        name="outproj",
        grid=(m // tm,),
        in_specs=[
            pl.BlockSpec((tm, D_ATTN), row),
            pl.BlockSpec((tm, D_SSM), row),
            pl.BlockSpec((tm, D_MODEL), row),
            pl.BlockSpec((D_ATTN, D_MODEL), lambda i: (0, 0), **once),
            pl.BlockSpec((D_SSM, D_MODEL), lambda i: (1, 0), **once),
            pl.BlockSpec((1, D_MODEL), lambda i: (0, 0), **once),
        ],
        out_specs=pl.BlockSpec((tm, D_MODEL), row),
        out_shape=jax.ShapeDtypeStruct((m, D_MODEL), F32),
        compiler_params=_cparams(("parallel",)),
    )(attn, ssm, x2d, w_out, w_out, gain)


def _head_pair_blockdiag(slab):
    lo = lax.broadcasted_iota(jnp.int32, slab.shape, 1) < HEAD_DIM
    rolled = pltpu.roll(slab, HEAD_DIM, axis=1)
    zero = jnp.zeros_like(slab)
    bd_a = jnp.concatenate([jnp.where(lo, slab, zero), jnp.where(lo, zero, rolled)], axis=0)
    bd_b = jnp.concatenate([jnp.where(lo, rolled, zero), jnp.where(lo, zero, slab)], axis=0)
    return bd_a.astype(BF16), bd_b.astype(BF16)


def _attend(q, kk, vv, mask, sink_ref, o_scr, after_group=None):
    tq, tk = mask.shape
    kk = kk * (HEAD_DIM ** -0.5 * LOG2E)
    kbd, vbd = [], []
    for c in range(D_KV // LANES):
        kbd.extend(_head_pair_blockdiag(kk[:, c * LANES:(c + 1) * LANES]))
        vbd.extend(_head_pair_blockdiag(vv[:, c * LANES:(c + 1) * LANES]))
    lo = lax.broadcasted_iota(jnp.int32, (tq, LANES), 1) < HEAD_DIM
    pairs_per_group = (D_ATTN // N_KV_HEADS) // LANES
    for g in range(N_KV_HEADS):
        cols = [g * pairs_per_group + j for j in range(pairs_per_group)]
        q_stack = jnp.concatenate([q[:, c * LANES:(c + 1) * LANES] for c in cols], axis=0).astype(BF16)
        s = lax.dot_general(q_stack, kbd[g], (((1,), (1,)), ((), ())), preferred_element_type=F32)
        p_rows, inv_rows = [], []
        for j, col in enumerate(cols):
            ps, invs = [], []
            for e in range(2):
                se = jnp.where(mask, s[j * tq:(j + 1) * tq, e * tk:(e + 1) * tk], -jnp.inf)
                snk = sink_ref[2 * col + e] * LOG2E
                mx = jnp.maximum(jnp.max(se, axis=-1, keepdims=True), snk)
                p = jnp.exp2(se - mx)
                den = jnp.sum(p, axis=-1, keepdims=True) + jnp.exp2(snk - mx)
                ps.append(p)
                invs.append(1.0 / den)
            p_rows.append(jnp.concatenate(ps, axis=1).astype(BF16))
            inv_rows.append(jnp.where(lo, invs[0], invs[1]))
        o = jnp.dot(jnp.concatenate(p_rows, axis=0), vbd[g], preferred_element_type=F32)
        for j, col in enumerate(cols):
            o_scr[:, col * LANES:(col + 1) * LANES] = o[j * tq:(j + 1) * tq, :] * inv_rows[j]
        if after_group is not None:
            after_group(g)


def _gate_norm(o, gate, gain):
    y = o * _silu(gate)
    y = y * lax.rsqrt(jnp.mean(y * y, axis=-1, keepdims=True) + EPS)
    return y * gain


def _attn_prompt_body(sink_ref, q_ref, g_ref, kp_ref, kc_ref, vp_ref, vc_ref, gain_ref, h_ref, w_ref,
                      out_ref, z_ref, xbc_ref, dt_ref, o_scr, *, blocks):
    i = pl.program_id(1)
    ql = lax.broadcasted_iota(jnp.int32, (WINDOW, 2 * WINDOW), 0)
    ks = lax.broadcasted_iota(jnp.int32, (WINDOW, 2 * WINDOW), 1)
    rel = ql + WINDOW - ks
    band = (rel >= 0) & (rel < WINDOW)
    steps = _ssm_projection_steps(h_ref[0], w_ref, z_ref, xbc_ref, dt_ref)
    slots = blocks * N_KV_HEADS

    def interleave(slot):
        for run in steps[slot * len(steps) // slots:(slot + 1) * len(steps) // slots]:
            run()

    for sub in range(blocks):
        rows = slice(sub * WINDOW, (sub + 1) * WINDOW)
        if sub == 0:
            k_prev, v_prev = kp_ref[0], vp_ref[0]
            first_key = jnp.where(i > 0, 0, WINDOW)
            mask = band & (ks >= first_key)
        else:
            before = slice((sub - 1) * WINDOW, sub * WINDOW)
            k_prev, v_prev = kc_ref[0, before, :], vc_ref[0, before, :]
            mask = band
        kk = jnp.concatenate([k_prev, kc_ref[0, rows, :]], axis=0)
        vv = jnp.concatenate([v_prev, vc_ref[0, rows, :]], axis=0)
        _attend(q_ref[0, rows, :], kk, vv, mask, sink_ref, o_scr.at[sub],
                after_group=lambda g, sub=sub: interleave(sub * N_KV_HEADS + g))
        out_ref[0, rows, :] = _gate_norm(o_scr[sub], g_ref[0, rows, :].astype(F32), gain_ref[...]).astype(BF16)


def _attn_prompt(qg3, kv3, h3, w_ssm, sink, gain, blocks):
    b, s, _ = qg3.shape
    tm = blocks * WINDOW
    blk = lambda w, col: pl.BlockSpec((1, tm, w), lambda bi, i: (bi, i, col))
    prev = lambda w, col: pl.BlockSpec((1, WINDOW, w), lambda bi, i: (bi, jnp.maximum(i * blocks - 1, 0), col))
    once = dict(pipeline_mode=pl.Buffered(1))
    return pl.pallas_call(
        functools.partial(_attn_prompt_body, blocks=blocks),
        name="attn_prompt",
        grid=(b, s // tm),
        in_specs=[
            pl.BlockSpec(memory_space=pltpu.SMEM),
            blk(D_ATTN, 0), blk(D_ATTN, 1),
            prev(D_KV, 0), blk(D_KV, 0),
            prev(D_KV, 1), blk(D_KV, 1),
            pl.BlockSpec((1, D_ATTN), lambda bi, i: (0, 0), **once),
            blk(D_MODEL, 0),
            pl.BlockSpec((D_MODEL, WS_COLS), lambda bi, i: (0, 0), **once),
        ],
        out_specs=[blk(D_ATTN, 0), blk(D_SSM, 0), blk(CONV_DIM, 0), blk(DT_W, 0)],
        out_shape=[jax.ShapeDtypeStruct((b, s, D_ATTN), BF16),
                   jax.ShapeDtypeStruct((b, s, D_SSM), BF16),
                   jax.ShapeDtypeStruct((b, s, CONV_DIM), F32),
                   jax.ShapeDtypeStruct((b, s, DT_W), F32)],
        scratch_shapes=[pltpu.VMEM((blocks, WINDOW, D_ATTN), F32)],
        compiler_params=_cparams(("parallel", "arbitrary")),
    )(sink, qg3, qg3, kv3, kv3, kv3, kv3, gain, h3, w_ssm)


Q_PAD = 16


def _attn_step_body(sink_ref, q_ref, g_ref, kn_ref, vn_ref, ck_ref, cv_ref, gain_ref,
                    out_ref, ko_ref, vo_ref, o_scr, *, nb, t):
    tk = 2 * WINDOW
    ql = lax.broadcasted_iota(jnp.int32, (Q_PAD, tk), 0)
    ks = lax.broadcasted_iota(jnp.int32, (Q_PAD, tk), 1)
    rel = ql + WINDOW - ks
    mask = (rel >= 0) & (rel < WINDOW) & (ql < t)
    pad_k = jnp.zeros((tk - WINDOW - t, D_KV), F32)
    pad_q = jnp.zeros((Q_PAD - t, D_ATTN), F32)

    def one_sequence(b, carry):
        kk = jnp.concatenate([ck_ref[b], kn_ref[b], pad_k], axis=0)
        vv = jnp.concatenate([cv_ref[b], vn_ref[b], pad_k], axis=0)
        q = jnp.concatenate([q_ref[b], pad_q], axis=0)
        _attend(q, kk, vv, mask, sink_ref, o_scr.at[b])
        out_ref[b] = _gate_norm(o_scr[b, 0:t, :], g_ref[b], gain_ref[...])
        ko_ref[b] = kk[t:t + WINDOW, :]
        vo_ref[b] = vv[t:t + WINDOW, :]
        return carry

    lax.fori_loop(0, nb, one_sequence, 0, unroll=2)


def _attn_step(qg3, kv3, cache_k, cache_v, sink, gain, nb):
    n, t, _ = qg3.shape
    blk = lambda w, col: pl.BlockSpec((nb, t, w), lambda i: (i, 0, col))
    cache = pl.BlockSpec((nb, WINDOW, D_KV), lambda i: (i, 0, 0))
    return pl.pallas_call(
        functools.partial(_attn_step_body, nb=nb, t=t),
        name="attn_step",
        grid=(n // nb,),
        in_specs=[
            pl.BlockSpec(memory_space=pltpu.SMEM),
            blk(D_ATTN, 0), blk(D_ATTN, 1),
            blk(D_KV, 0), blk(D_KV, 1),
            cache,
            cache,
            pl.BlockSpec((1, D_ATTN), lambda i: (0, 0)),
        ],
        out_specs=[pl.BlockSpec((nb, t, D_ATTN), lambda i: (i, 0, 0)), cache, cache],
        out_shape=[
            jax.ShapeDtypeStruct((n, t, D_ATTN), F32),
            jax.ShapeDtypeStruct((n, WINDOW, D_KV), F32),
            jax.ShapeDtypeStruct((n, WINDOW, D_KV), F32),
        ],
        scratch_shapes=[pltpu.VMEM((nb, Q_PAD, D_ATTN), F32)],
        compiler_params=_cparams(("parallel",)),
    )(sink, qg3, qg3, kv3, kv3, cache_k, cache_v, gain)


ROWS = 128


def _expand_heads(v):
    r = v.shape[0]
    lo = lax.broadcasted_iota(jnp.int32, (r, LANES), 1) < HEAD_DIM
    cols = []
    for c in range(D_SSM // LANES):
        va = jnp.broadcast_to(v[:, 2 * c:2 * c + 1], (r, LANES))
        vb = jnp.broadcast_to(v[:, 2 * c + 1:2 * c + 2], (r, LANES))
        cols.append(jnp.where(lo, va, vb))
    return jnp.concatenate(cols, axis=1)


def _conv_taps(load_rows, w_ref, b_ref):
    acc = b_ref[...] + w_ref[CONV_K - 1:CONV_K, :] * load_rows(CONV_K - 1)
    for k in range(CONV_K - 1):
        acc = acc + w_ref[k:k + 1, :] * load_rows(k)
    return _silu(acc)


def _conv_rolled(x, tail_scr, w_ref, b_ref):
    taps = CONV_K - 1
    width = x.shape[1]
    tiles = ROWS // SUBLANES
    with_prev = jnp.concatenate([tail_scr[...], x], axis=0).reshape(tiles + 1, SUBLANES, width)
    row = lax.broadcasted_iota(jnp.int32, (tiles, SUBLANES, width), 1)
    acc = b_ref[...] + w_ref[taps:taps + 1, :] * x
    for j in range(1, CONV_K):
        rot = pltpu.roll(with_prev, j, axis=1)
        shifted = jnp.where(row < j, rot[:tiles], rot[1:]).reshape(ROWS, width)
        acc = acc + w_ref[taps - j:taps - j + 1, :] * shifted
    tail_scr[...] = x[ROWS - SUBLANES:, :]
    return _silu(acc)


def _ssd_intra(xs, bm, cm, dt_raw, dtb_ref, alog_ref, causal, same, ybuf, after_group=None):
    dt = jax.nn.softplus(dt_raw + dtb_ref[...])
    a_neg = -jnp.exp(alog_ref[...])
    dta = dt * (a_neg * LOG2E)
    hi = lax.Precision.HIGHEST
    a = jnp.dot(causal.astype(F32), dta, precision=hi, preferred_element_type=F32)
    tot = jnp.dot(same.astype(F32), dta, precision=hi, preferred_element_type=F32)
    r_t = (a - jnp.log2(dt)).T
    lo = lax.broadcasted_iota(jnp.int32, (ROWS, LANES), 1) < HEAD_DIM
    heads_per_group = N_HEADS_SSM // N_GROUPS
    for g in range(N_GROUPS):
        cg = cm[:, g * D_STATE:(g + 1) * D_STATE].astype(BF16)
        bg = bm[:, g * D_STATE:(g + 1) * D_STATE].astype(BF16)
        cb = lax.dot_general(cg, bg, (((1,), (1,)), ((), ())), preferred_element_type=F32)
        for j in range(heads_per_group // 2):
            c = g * (heads_per_group // 2) + j
            ms = []
            for e in range(2):
                h = 2 * c + e
                diff = jnp.broadcast_to(a[:, h:h + 1], (ROWS, ROWS)) - r_t[h:h + 1, :]
                ms.append(cb * jnp.exp2(jnp.where(causal, diff, -jnp.inf)))
            lhs = jnp.concatenate(ms, axis=1).astype(BF16)
            xc = xs[:, c * LANES:(c + 1) * LANES]
            zero = jnp.zeros_like(xc)
            rhs = jnp.concatenate([jnp.where(lo, xc, zero), jnp.where(lo, zero, xc)], axis=0).astype(BF16)
            ybuf[:, c * LANES:(c + 1) * LANES] = jnp.dot(lhs, rhs, preferred_element_type=F32)
        if after_group is not None:
            after_group(g)
    e_expa = _expand_heads(jnp.exp2(a))
    e_w = _expand_heads(jnp.exp2(tot - a) * dt)
    return e_expa, e_w


def _gated_group_norm(y, z, gain_ref, out_ref_store):
    yz = y * _silu(z)
    for g in range(N_GROUPS):
        blk = yz[:, g * GROUP_W:(g + 1) * GROUP_W]
        nrm = blk * lax.rsqrt(jnp.mean(blk * blk, axis=-1, keepdims=True) + EPS)
        out_ref_store(g, nrm * gain_ref[:, g * GROUP_W:(g + 1) * GROUP_W])


OUT_CHUNK = 256


def _ssd_out_body(z_ref, xs_ref, b_ref, c_ref, dt_ref, attn_ref, x_ref,
                  cwx, cwb, cwc, cbx, cbb, cbc, dtb_ref, alog_ref, dskip_ref, gain_ref, w_ref, post_ref,
                  y_ref, cvx_ref, cvb_ref, cvc_ref, st_ref,
                  tlx, tlb, tlc, h_t, ybuf, mix, o_acc, *, chunks, steps_per_seq, total):
    s = pl.program_id(0)
    c2 = jnp.minimum(s, total - 1) % steps_per_seq

    @pl.when(s == 0)
    def _():
        mix[0] = jnp.zeros((chunks * ROWS, D_SSM), BF16)

    @pl.when(c2 == 0)
    def _():
        tlx[...] = jnp.zeros_like(tlx)
        tlb[...] = jnp.zeros_like(tlb)
        tlc[...] = jnp.zeros_like(tlc)
        h_t[...] = jnp.zeros_like(h_t)

    mix[1] = mix[0]
    slabs = []
    for n in range(0, D_MODEL, OUT_CHUNK):
        def slab(n=n):
            o_acc[:, n:n + OUT_CHUNK] = (
                jnp.dot(attn_ref[0], w_ref[0:D_ATTN, n:n + OUT_CHUNK], preferred_element_type=F32)
                + jnp.dot(mix[1], w_ref[D_ATTN:D_ATTN + D_SSM, n:n + OUT_CHUNK], preferred_element_type=F32))
        slabs.append(slab)
    points_per_chunk = 2 * N_GROUPS + 2
    slots = chunks * points_per_chunk
    issued = [0]

    def interleave():
        k = issued[0]
        for run in slabs[k * len(slabs) // slots:(k + 1) * len(slabs) // slots]:
            run()
        issued[0] = k + 1

    rl = lax.broadcasted_iota(jnp.int32, (ROWS, ROWS), 0)
    cs = lax.broadcasted_iota(jnp.int32, (ROWS, ROWS), 1)
    causal = cs <= rl
    same = cs >= 0
    live = {}

    def conv_phase(ch):
        rows = slice(ch * ROWS, (ch + 1) * ROWS)
        xs = _conv_rolled(xs_ref[0, rows, :], tlx, cwx, cbx)
        bm = _conv_rolled(b_ref[0, rows, :], tlb, cwb, cbb)
        cm = _conv_rolled(c_ref[0, rows, :], tlc, cwc, cbc)
        live[ch] = dict(xs=xs, bm=bm, cm=cm)
        interleave()

    def intra_phase(ch):
        v = live[ch]
        rows = slice(ch * ROWS, (ch + 1) * ROWS)
        v["e_expa"], v["e_w"] = _ssd_intra(v["xs"], v["bm"], v["cm"], dt_ref[0, rows, :], dtb_ref, alog_ref,
                                           causal, same, ybuf.at[ch], after_group=lambda g: interleave())

    def state_phase(ch):
        v = live[ch]
        yb = ybuf.at[ch]
        xw = (v["xs"] * v["e_w"]).astype(BF16)
        for g in range(N_GROUPS):
            gs = slice(g * GROUP_W, (g + 1) * GROUP_W)
            ds = slice(g * D_STATE, (g + 1) * D_STATE)
            h_prev = h_t[:, gs]
            y_inter = jnp.dot(v["cm"][:, ds].astype(BF16), h_prev.astype(BF16), preferred_element_type=F32)
            yb[:, gs] = yb[:, gs] + y_inter * v["e_expa"][:, gs]
            s_t = jnp.dot(v["bm"][:, ds].T.astype(BF16), xw[:, gs], preferred_element_type=F32)
            h_t[:, gs] = h_prev * v["e_expa"][ROWS - 1:ROWS, gs] + s_t
            interleave()

    def norm_phase(ch):
        v = live.pop(ch)
        rows = slice(ch * ROWS, (ch + 1) * ROWS)
        y = ybuf[ch] + dskip_ref[...] * v["xs"]

        def store(g, val):
            mix[0, rows, g * GROUP_W:(g + 1) * GROUP_W] = val.astype(BF16)
        _gated_group_norm(y, z_ref[0, rows, :].astype(F32), gain_ref, store)
        interleave()

    conv_phase(0)
    intra_phase(0)
    for ch in range(chunks):
        if ch + 1 < chunks:
            conv_phase(ch + 1)
        state_phase(ch)
        if ch + 1 < chunks:
            intra_phase(ch + 1)
        norm_phase(ch)

    o = o_acc[...]
    o = o * lax.rsqrt(jnp.mean(o * o, axis=-1, keepdims=True) + EPS)
    y_ref[0] = x_ref[0] + o * post_ref[...]

    @pl.when((c2 == steps_per_seq - 1) & (s < total))
    def _():
        first = SUBLANES - (CONV_K - 1)
        cvx_ref[0] = tlx[first:SUBLANES, :]
        cvb_ref[0] = tlb[first:SUBLANES, :]
        cvc_ref[0] = tlc[first:SUBLANES, :]
        st_ref[0] = h_t[...].T


def _ssd_outproj(z3, xbc3, dt3, attn3, x3, p, chunks):
    b, s, _ = z3.shape
    tm = chunks * ROWS
    per_seq = s // tm
    total = b * per_seq
    cur = lambda i: jnp.minimum(i, total - 1)
    prv = lambda i: jnp.maximum(i - 1, 0)
    blk = lambda w, col: pl.BlockSpec((1, tm, w), lambda i: (cur(i) // per_seq, cur(i) % per_seq, col))
    lag = lambda w: pl.BlockSpec((1, tm, w), lambda i: (prv(i) // per_seq, prv(i) % per_seq, 0))
    par = lambda r, w: pl.BlockSpec((r, w), lambda i: (0, 0), pipeline_mode=pl.Buffered(1))
    per_b = lambda r, w: pl.BlockSpec((1, r, w), lambda i: (cur(i) // per_seq, 0, 0))
    tail = CONV_K - 1
    return pl.pallas_call(
        functools.partial(_ssd_out_body, chunks=chunks, steps_per_seq=per_seq, total=total),
        name="ssd_outproj",
        grid=(total + 1,),
        in_specs=[
            blk(D_SSM, 0), blk(D_SSM, 0),
            blk(BC_W, D_SSM // BC_W), blk(BC_W, D_SSM // BC_W + 1),
            blk(DT_W, 0),
            lag(D_ATTN), lag(D_MODEL),
            par(CONV_K, D_SSM), par(CONV_K, BC_W), par(CONV_K, BC_W),
            par(1, D_SSM), par(1, BC_W), par(1, BC_W),
            par(1, DT_W), par(1, DT_W), par(1, D_SSM), par(1, D_SSM),
            par(D_ATTN + D_SSM, D_MODEL), par(1, D_MODEL),
        ],
        out_specs=[
            lag(D_MODEL),
            per_b(tail, D_SSM), per_b(tail, BC_W), per_b(tail, BC_W),
            per_b(D_SSM, D_STATE),
        ],
        out_shape=[
            jax.ShapeDtypeStruct((b, s, D_MODEL), F32),
            jax.ShapeDtypeStruct((b, tail, D_SSM), F32),
            jax.ShapeDtypeStruct((b, tail, BC_W), F32),
            jax.ShapeDtypeStruct((b, tail, BC_W), F32),
            jax.ShapeDtypeStruct((b, D_SSM, D_STATE), F32),
        ],
        scratch_shapes=[
            pltpu.VMEM((SUBLANES, D_SSM), F32),
            pltpu.VMEM((SUBLANES, BC_W), F32),
            pltpu.VMEM((SUBLANES, BC_W), F32),
            pltpu.VMEM((D_STATE, D_SSM), F32),
            pltpu.VMEM((chunks, ROWS, D_SSM), F32),
            pltpu.VMEM((2, tm, D_SSM), BF16),
            pltpu.VMEM((tm, D_MODEL), F32),
        ],
        compiler_params=_cparams(("arbitrary",)),
    )(z3, xbc3, xbc3, xbc3, dt3, attn3, x3,
      p["cwx"], p["cwb"], p["cwc"], p["cbx"], p["cbb"], p["cbc"],
      p["dtb"], p["alog"], p["dskip"], p["ssm_gain"], p["w_out"], p["norm_post"])


def _ssd_step_body(z_ref, xs_ref, b_ref, c_ref, dt_ref, sx_ref, sb_ref, sc_ref, h0_ref,
                   cwx, cwb, cwc, cbx, cbb, cbc, dtb_ref, alog_ref, dskip_ref, gain_ref,
                   out_ref, cvx_ref, cvb_ref, cvc_ref, st_ref,
                   xpx, xpb, xpc, ybuf, cm_scr, ea_scr, ea_t_scr, xw_t_scr, *, nb, t):
    real = nb * t
    tail = CONV_K - 1
    lo_row = SUBLANES - tail

    def conv(xp, new_ref, st_in_ref, st_out_ref, w_ref, b_ref):
        xp[:, lo_row:SUBLANES, :] = st_in_ref[...]
        xp[:, SUBLANES:SUBLANES + t, :] = new_ref[...]
        st_out_ref[...] = xp[:, SUBLANES + t - tail:SUBLANES + t, :]
        w = new_ref.shape[-1]
        act = _conv_taps(lambda k: xp[:, lo_row + k:lo_row + k + t, :].reshape(real, w), w_ref, b_ref)
        return jnp.concatenate([act, jnp.zeros((ROWS - real, w), F32)], axis=0)

    xs = conv(xpx, xs_ref, sx_ref, cvx_ref, cwx, cbx)
    bm = conv(xpb, b_ref, sb_ref, cvb_ref, cwb, cbb)
    cm = conv(xpc, c_ref, sc_ref, cvc_ref, cwc, cbc)
    dt_raw = jnp.concatenate([dt_ref[...].reshape(real, DT_W), jnp.zeros((ROWS - real, DT_W), F32)], axis=0)

    rl = lax.broadcasted_iota(jnp.int32, (ROWS, ROWS), 0)
    cs = lax.broadcasted_iota(jnp.int32, (ROWS, ROWS), 1)
    same = (rl // t) == (cs // t)
    causal = same & (cs <= rl)
    e_expa, e_w = _ssd_intra(xs, bm, cm, dt_raw, dtb_ref, alog_ref, causal, same, ybuf)

    cm_scr[...] = cm
    ea_scr[...] = e_expa
    ea_t_scr[...] = e_expa.T
    xw_t_scr[...] = (xs * e_w).T.astype(BF16)
    row_id = lax.broadcasted_iota(jnp.int32, (ROWS, D_STATE), 0)
    pair = 2 * SUBLANES
    for b in range(nb):
        in_seq = (row_id >= b * t) & (row_id < (b + 1) * t)
        r0 = b * t
        last_col = r0 + t - 1
        for g in range(N_GROUPS):
            gs = slice(g * GROUP_W, (g + 1) * GROUP_W)
            ds = slice(g * D_STATE, (g + 1) * D_STATE)
            h0 = h0_ref[b, gs, :]
            c_rows = cm_scr[r0:r0 + pair, ds].astype(BF16)
            y_inter = lax.dot_general(c_rows, h0.astype(BF16), (((1,), (1,)), ((), ())),
                                      preferred_element_type=F32)
            ybuf[r0:r0 + t, gs] = ybuf[r0:r0 + t, gs] + y_inter[0:t] * ea_scr[r0:r0 + t, gs]
            b_rows = jnp.where(in_seq, bm[:, ds], 0.0).astype(BF16)
            s_new = jnp.dot(xw_t_scr[gs, :], b_rows, preferred_element_type=F32)
            decay = jnp.broadcast_to(ea_t_scr[gs, last_col:last_col + 1], (GROUP_W, D_STATE))
            st_ref[b, gs, :] = h0 * decay + s_new

    y = ybuf[0:real, :] + dskip_ref[...] * xs[0:real, :]
    z = z_ref[...].reshape(real, D_SSM)

    def store(g, v):
        out_ref[:, :, g * GROUP_W:(g + 1) * GROUP_W] = v.reshape(nb, t, GROUP_W)
    _gated_group_norm(y, z, gain_ref, store)


def _ssd_step(z3, xbc3, dt3, conv_state, h0, p, nb):
    n, t, _ = z3.shape
    tail = CONV_K - 1
    blk = lambda w, col: pl.BlockSpec((nb, t, w), lambda i: (i, 0, col))
    cst = lambda w, col: pl.BlockSpec((nb, tail, w), lambda i: (i, 0, col))
    par = lambda r, w: pl.BlockSpec((r, w), lambda i: (0, 0))
    state = pl.BlockSpec((nb, D_SSM, D_STATE), lambda i: (i, 0, 0))
    return pl.pallas_call(
        functools.partial(_ssd_step_body, nb=nb, t=t),
        name="ssd_step",
        grid=(n // nb,),
        in_specs=[
            blk(D_SSM, 0), blk(D_SSM, 0),
            blk(BC_W, D_SSM // BC_W), blk(BC_W, D_SSM // BC_W + 1),
            blk(DT_W, 0),
            cst(D_SSM, 0), cst(BC_W, D_SSM // BC_W), cst(BC_W, D_SSM // BC_W + 1),
            state,
            par(CONV_K, D_SSM), par(CONV_K, BC_W), par(CONV_K, BC_W),
            par(1, D_SSM), par(1, BC_W), par(1, BC_W),
            par(1, DT_W), par(1, DT_W), par(1, D_SSM), par(1, D_SSM),
        ],
        out_specs=[
            blk(D_SSM, 0),
            cst(D_SSM, 0), cst(BC_W, 0), cst(BC_W, 0),
            state,
        ],
        out_shape=[
            jax.ShapeDtypeStruct((n, t, D_SSM), F32),
            jax.ShapeDtypeStruct((n, tail, D_SSM), F32),
            jax.ShapeDtypeStruct((n, tail, BC_W), F32),
            jax.ShapeDtypeStruct((n, tail, BC_W), F32),
            jax.ShapeDtypeStruct((n, D_SSM, D_STATE), F32),
        ],
        scratch_shapes=[
            pltpu.VMEM((nb, SUBLANES + t, D_SSM), F32),
            pltpu.VMEM((nb, SUBLANES + t, BC_W), F32),
            pltpu.VMEM((nb, SUBLANES + t, BC_W), F32),
            pltpu.VMEM((ROWS, D_SSM), F32),
            pltpu.VMEM((ROWS, BC_W), F32),
            pltpu.VMEM((ROWS, D_SSM), F32),
            pltpu.VMEM((D_SSM, ROWS), F32),
            pltpu.VMEM((D_SSM, ROWS), BF16),
        ],
        compiler_params=_cparams(("parallel",)),
    )(z3, xbc3, xbc3, xbc3, dt3, conv_state, conv_state, conv_state, h0,
      p["cwx"], p["cwb"], p["cwc"], p["cbx"], p["cbb"], p["cbc"],
      p["dtb"], p["alog"], p["dskip"], p["ssm_gain"])


def _layer_params(norm_pre, w_in, attn_sink, attn_norm, conv_w, conv_b, dt_bias, a_log, d_skip,
                  ssm_norm, w_out, norm_post):
    w_attn = w_in[:, :WA_COLS].astype(BF16)
    w_ssm = jnp.pad(w_in[:, WA_COLS:], ((0, 0), (0, DT_W - N_HEADS_SSM))).astype(BF16)
    pad_h = lambda v: jnp.pad(v, (0, DT_W - N_HEADS_SSM)).reshape(1, DT_W)
    return dict(
        norm_pre=norm_pre.reshape(1, D_MODEL), w_attn=w_attn, w_ssm=w_ssm,
        sink=attn_sink, attn_gain=attn_norm.reshape(1, D_ATTN),
        cwx=conv_w[:, :D_SSM], cwb=conv_w[:, D_SSM:D_SSM + BC_W], cwc=conv_w[:, D_SSM + BC_W:],
        cbx=conv_b[:D_SSM].reshape(1, -1), cbb=conv_b[D_SSM:D_SSM + BC_W].reshape(1, -1),
        cbc=conv_b[D_SSM + BC_W:].reshape(1, -1),
        dtb=pad_h(dt_bias), alog=pad_h(a_log),
        dskip=jnp.broadcast_to(d_skip[:, None], (N_HEADS_SSM, HEAD_DIM)).reshape(1, D_SSM),
        ssm_gain=ssm_norm.reshape(1, D_SSM),
        w_out=w_out.astype(BF16),
        norm_post=norm_post.reshape(1, D_MODEL),
    )


IN_TM, OUT_TM = 512, 512
ATTN_BLOCKS = 2
SSD_CHUNKS = 2
STEP_NB_ATTN, STEP_NB_SSD = 8, 8


def _layer(x, caches, p):
    n, t, _ = x.shape
    x2d = x.reshape(n * t, D_MODEL)
    act_dtype = BF16 if caches is None else F32
    h, qg, kv = _inproj_attn(x2d, p["norm_pre"], p["w_attn"], IN_TM, act_dtype)
    qg3, kv3 = qg.reshape(n, t, 2 * D_ATTN), kv.reshape(n, t, 2 * D_KV)
    if caches is None:
        attn, z3, xbc3, dt3 = _attn_prompt(qg3, kv3, h.reshape(n, t, D_MODEL), p["w_ssm"],
                                           p["sink"], p["attn_gain"], ATTN_BLOCKS)
        new_k = kv3[:, t - WINDOW:, :D_KV]
        new_v = kv3[:, t - WINDOW:, D_KV:]
        y, cvx, cvb, cvc, h_new = _ssd_outproj(z3, xbc3, dt3, attn, x, p, SSD_CHUNKS)
    else:
        z, xbc, dt = _inproj_ssm(h, p["w_ssm"], IN_TM, act_dtype)
        z3, xbc3, dt3 = z.reshape(n, t, D_SSM), xbc.reshape(n, t, CONV_DIM), dt.reshape(n, t, DT_W)
        kbuf, vbuf, conv_buf, h0 = caches
        attn, new_k, new_v = _attn_step(qg3, kv3, kbuf.reshape(n, WINDOW, D_KV), vbuf.reshape(n, WINDOW, D_KV),
                                        p["sink"], p["attn_gain"], STEP_NB_ATTN)
        ssm, cvx, cvb, cvc, h_new = _ssd_step(z3, xbc3, dt3, conv_buf, h0.reshape(n, D_SSM, D_STATE), p,
                                              STEP_NB_SSD)
        y = _outproj(attn.reshape(n * t, D_ATTN), ssm.reshape(n * t, D_SSM), x2d,
                     p["w_out"], p["norm_post"], OUT_TM)
    return (y.reshape(n, t, D_MODEL),
            new_k.reshape(n, WINDOW, N_KV_HEADS, HEAD_DIM), new_v.reshape(n, WINDOW, N_KV_HEADS, HEAD_DIM),
            jnp.concatenate([cvx, cvb, cvc], axis=-1),
            h_new.reshape(n, N_HEADS_SSM, HEAD_DIM, D_STATE))


def kernel(x_prompt, x_sample, cache_k, cache_v, state_conv, state_ssm, norm_pre, w_in, attn_sink,
           attn_norm, conv_w, conv_b, dt_bias, a_log, d_skip, ssm_norm, w_out, norm_post):
    depth = w_in.shape[0]
    yp, ys = x_prompt, x_sample
    outs_p, outs_s = [], []
    for l in range(depth):
        p = _layer_params(norm_pre[l], w_in[l], attn_sink[l], attn_norm[l], conv_w[l], conv_b[l],
                          dt_bias[l], a_log[l], d_skip[l], ssm_norm[l], w_out[l], norm_post[l])
        yp, *rest_p = _layer(yp, None, p)
        ys, *rest_s = _layer(ys, (cache_k[l], cache_v[l], state_conv[l], state_ssm[l]), p)
        outs_p.append(rest_p)
        outs_s.append(rest_s)
    stack = lambda outs, i: jnp.stack([o[i] for o in outs])
    return (yp, ys,
            stack(outs_p, 0), stack(outs_p, 1), stack(outs_p, 2), stack(outs_p, 3),
            stack(outs_s, 0), stack(outs_s, 1), stack(outs_s, 2), stack(outs_s, 3))
```

```python
import functools

import jax
import jax.numpy as jnp
from jax import lax
from jax.experimental import pallas as pl
from jax.experimental.pallas import tpu as pltpu

F32 = jnp.float32
BF16 = jnp.bfloat16

D_MODEL = 2048
HEAD_DIM = 64
N_KV_HEADS = 4
D_ATTN = 2048
D_KV = 256
WINDOW = 128
D_SSM = 2048
N_HEADS_SSM = 32
N_GROUPS = 4
D_STATE = 128
CONV_K = 4
GROUP_W = D_SSM // N_GROUPS
BC_W = N_GROUPS * D_STATE
CONV_DIM = D_SSM + 2 * BC_W
EPS = 1e-6
LOG2E = 1.4426950408889634
LANES = 128
SUBLANES = 8

DT_W = LANES
WA_COLS = 2 * D_ATTN + 2 * D_KV
WS_COLS = D_SSM + CONV_DIM + DT_W
WT_COLS = WS_COLS - WA_COLS

VMEM_LIMIT = 56 * 1024 * 1024


def _silu(x):
    h = 0.5 * x
    return h + h * jnp.tanh(h)


def _cparams(sem):
    return pltpu.CompilerParams(dimension_semantics=sem, vmem_limit_bytes=VMEM_LIMIT)


def _inproj_attn_body(x_ref, g_ref, w_ref, h_ref, qg_ref, kv_ref, *, row_chunk):
    tm = x_ref.shape[0]

    def norm_rows(r, carry):
        rows = pl.ds(pl.multiple_of(r * row_chunk, row_chunk), row_chunk)
        x = x_ref[rows, :]
        y = x * lax.rsqrt(jnp.mean(x * x, axis=-1, keepdims=True) + EPS)
        h_ref[rows, :] = (y * g_ref[...]).astype(BF16)
        return carry
    lax.fori_loop(0, tm // row_chunk, norm_rows, 0)

    h = h_ref[...]
    dot = lambda lo, hi: jnp.dot(h, w_ref[:, lo:hi], preferred_element_type=F32)
    qg_ref[:, 0:D_ATTN] = dot(0, D_ATTN).astype(qg_ref.dtype)
    kv_ref[...] = dot(D_ATTN, D_ATTN + 2 * D_KV)
    qg_ref[:, D_ATTN:2 * D_ATTN] = dot(D_ATTN + 2 * D_KV, WA_COLS).astype(qg_ref.dtype)


def _inproj_ssm_body(h_ref, w_ref, wt_ref, z_ref, xbc_ref, dt_ref):
    for run in _ssm_projection_steps(h_ref[...], w_ref, wt_ref, z_ref, xbc_ref, dt_ref):
        run()


PROJ_CHUNK = 512


def _ssm_projection_steps(h, w_ref, wt_ref, z_ref, xbc_ref, dt_ref):
    lead = z_ref.shape[:-1]
    rows = (slice(None),) * len(lead)

    def slab(out_ref, out_lo, w_lo, width):
        def run():
            w = w_ref[:, w_lo:w_lo + width] if w_lo < WA_COLS else wt_ref[:, w_lo - WA_COLS:w_lo - WA_COLS + width]
            r = jnp.dot(h, w, preferred_element_type=F32)
            out_ref[rows + (slice(out_lo, out_lo + width),)] = r.astype(out_ref.dtype).reshape(*lead, width)
        return run

    steps = [slab(z_ref, c, c, PROJ_CHUNK) for c in range(0, D_SSM, PROJ_CHUNK)]
    steps += [slab(xbc_ref, c, D_SSM + c, PROJ_CHUNK) for c in range(0, CONV_DIM, PROJ_CHUNK)]
    steps.append(slab(dt_ref, 0, D_SSM + CONV_DIM, DT_W))
    return steps


def _inproj_attn(x2d, gain, w_in, tm, act_dtype):
    m = x2d.shape[0]
    tm = min(tm, m)
    row = lambda w: pl.BlockSpec((tm, w), lambda i: (i, 0))
    resident = lambda r, w: pl.BlockSpec((r, w), lambda i: (0, 0), pipeline_mode=pl.Buffered(1))
    return pl.pallas_call(
        functools.partial(_inproj_attn_body, row_chunk=min(128, tm)),
        name="inproj_attn",
        grid=(m // tm,),
        in_specs=[row(D_MODEL), resident(1, D_MODEL), resident(D_MODEL, WA_COLS)],
        out_specs=[row(D_MODEL), row(2 * D_ATTN), row(2 * D_KV)],
        out_shape=[jax.ShapeDtypeStruct((m, D_MODEL), BF16),
                   jax.ShapeDtypeStruct((m, 2 * D_ATTN), act_dtype),
                   jax.ShapeDtypeStruct((m, 2 * D_KV), F32)],
        compiler_params=_cparams(("parallel",)),
    )(x2d, gain, w_in)


def _inproj_ssm(h, w_in, w_tail, tm, act_dtype):
    m = h.shape[0]
    tm = min(tm, m)
    row = lambda w: pl.BlockSpec((tm, w), lambda i: (i, 0))
    once = dict(pipeline_mode=pl.Buffered(1))
    return pl.pallas_call(
        _inproj_ssm_body,
        name="inproj_ssm",
        grid=(m // tm,),
        in_specs=[row(D_MODEL),
                  pl.BlockSpec((D_MODEL, WA_COLS), lambda i: (0, 1), **once),
                  pl.BlockSpec((D_MODEL, WT_COLS), lambda i: (0, 0), **once)],
        out_specs=[row(D_SSM), row(CONV_DIM), row(DT_W)],
        out_shape=[jax.ShapeDtypeStruct((m, D_SSM), act_dtype),
                   jax.ShapeDtypeStruct((m, CONV_DIM), F32),
                   jax.ShapeDtypeStruct((m, DT_W), F32)],
        compiler_params=_cparams(("parallel",)),
    )(h, w_in, w_tail)


def _outproj_body(a_ref, s_ref, x_ref, wa_ref, ws_ref, g_ref, y_ref):
    o = jnp.dot(a_ref[...].astype(BF16), wa_ref[...], preferred_element_type=F32)
    o = o + jnp.dot(s_ref[...].astype(BF16), ws_ref[...], preferred_element_type=F32)
    o = o * lax.rsqrt(jnp.mean(o * o, axis=-1, keepdims=True) + EPS)
    y_ref[...] = x_ref[...] + o * g_ref[...]


def _outproj(attn, ssm, x2d, w_out, gain, tm):
    m = x2d.shape[0]
    tm = min(tm, m)
    row = lambda i: (i, 0)
    once = dict(pipeline_mode=pl.Buffered(1))
    return pl.pallas_call(
        _outproj_body,
        name="outproj",
        grid=(m // tm,),
        in_specs=[
            pl.BlockSpec((tm, D_ATTN), row),
            pl.BlockSpec((tm, D_SSM), row),
            pl.BlockSpec((tm, D_MODEL), row),
            pl.BlockSpec((D_ATTN, D_MODEL), lambda i: (0, 0), **once),
            pl.BlockSpec((D_SSM, D_MODEL), lambda i: (1, 0), **once),
            pl.BlockSpec((1, D_MODEL), lambda i: (0, 0), **once),
        ],
        out_specs=pl.BlockSpec((tm, D_MODEL), row),
        out_shape=jax.ShapeDtypeStruct((m, D_MODEL), F32),
        compiler_params=_cparams(("parallel",)),
    )(attn, ssm, x2d, w_out, w_out, gain)


def _head_pair_blockdiag(slab):
    lo = lax.broadcasted_iota(jnp.int32, slab.shape, 1) < HEAD_DIM
    rolled = pltpu.roll(slab, HEAD_DIM, axis=1)
    zero = jnp.zeros_like(slab)
    bd_a = jnp.concatenate([jnp.where(lo, slab, zero), jnp.where(lo, zero, rolled)], axis=0)
    bd_b = jnp.concatenate([jnp.where(lo, rolled, zero), jnp.where(lo, zero, slab)], axis=0)
    return bd_a.astype(BF16), bd_b.astype(BF16)


def _attend(q, kk, vv, mask, sink_ref, o_scr, after_group=None):
    tq, tk = mask.shape
    kk = kk * (HEAD_DIM ** -0.5 * LOG2E)
    kbd, vbd = [], []
    for c in range(D_KV // LANES):
        kbd.extend(_head_pair_blockdiag(kk[:, c * LANES:(c + 1) * LANES]))
        vbd.extend(_head_pair_blockdiag(vv[:, c * LANES:(c + 1) * LANES]))
    lo = lax.broadcasted_iota(jnp.int32, (tq, LANES), 1) < HEAD_DIM
    pairs_per_group = (D_ATTN // N_KV_HEADS) // LANES
    for g in range(N_KV_HEADS):
        cols = [g * pairs_per_group + j for j in range(pairs_per_group)]
        q_stack = jnp.concatenate([q[:, c * LANES:(c + 1) * LANES] for c in cols], axis=0).astype(BF16)
        s = lax.dot_general(q_stack, kbd[g], (((1,), (1,)), ((), ())), preferred_element_type=F32)
        p_rows, inv_rows = [], []
        for j, col in enumerate(cols):
            ps, invs = [], []
            for e in range(2):
                se = jnp.where(mask, s[j * tq:(j + 1) * tq, e * tk:(e + 1) * tk], -jnp.inf)
                snk = sink_ref[2 * col + e] * LOG2E
                mx = jnp.maximum(jnp.max(se, axis=-1, keepdims=True), snk)
                p = jnp.exp2(se - mx)
                den = jnp.sum(p, axis=-1, keepdims=True) + jnp.exp2(snk - mx)
                ps.append(p)
                invs.append(1.0 / den)
            p_rows.append(jnp.concatenate(ps, axis=1).astype(BF16))
            inv_rows.append(jnp.where(lo, invs[0], invs[1]))
        o = jnp.dot(jnp.concatenate(p_rows, axis=0), vbd[g], preferred_element_type=F32)
        for j, col in enumerate(cols):
            o_scr[:, col * LANES:(col + 1) * LANES] = o[j * tq:(j + 1) * tq, :] * inv_rows[j]
        if after_group is not None:
            after_group(g)


def _gate_norm(o, gate, gain):
    y = o * _silu(gate)
    y = y * lax.rsqrt(jnp.mean(y * y, axis=-1, keepdims=True) + EPS)
    return y * gain


def _attn_prompt_body(sink_ref, q_ref, g_ref, kp_ref, kc_ref, vp_ref, vc_ref, gain_ref, h_ref, w_ref, wt_ref,
                      out_ref, z_ref, xbc_ref, dt_ref, o_scr, *, blocks):
    i = pl.program_id(1)
    ql = lax.broadcasted_iota(jnp.int32, (WINDOW, 2 * WINDOW), 0)
    ks = lax.broadcasted_iota(jnp.int32, (WINDOW, 2 * WINDOW), 1)
    rel = ql + WINDOW - ks
    band = (rel >= 0) & (rel < WINDOW)
    steps = _ssm_projection_steps(h_ref[0], w_ref, wt_ref, z_ref, xbc_ref, dt_ref)
    slots = blocks * N_KV_HEADS

    def interleave(slot):
        for run in steps[slot * len(steps) // slots:(slot + 1) * len(steps) // slots]:
            run()

    for sub in range(blocks):
        rows = slice(sub * WINDOW, (sub + 1) * WINDOW)
        if sub == 0:
            k_prev, v_prev = kp_ref[0], vp_ref[0]
            first_key = jnp.where(i > 0, 0, WINDOW)
            mask = band & (ks >= first_key)
        else:
            before = slice((sub - 1) * WINDOW, sub * WINDOW)
            k_prev, v_prev = kc_ref[0, before, :], vc_ref[0, before, :]
            mask = band
        kk = jnp.concatenate([k_prev, kc_ref[0, rows, :]], axis=0)
        vv = jnp.concatenate([v_prev, vc_ref[0, rows, :]], axis=0)
        _attend(q_ref[0, rows, :], kk, vv, mask, sink_ref, o_scr.at[sub],
                after_group=lambda g, sub=sub: interleave(sub * N_KV_HEADS + g))
        out_ref[0, rows, :] = _gate_norm(o_scr[sub], g_ref[0, rows, :].astype(F32), gain_ref[...]).astype(BF16)


def _attn_prompt(qg3, kv3, h3, w_in, w_tail, sink, gain, blocks):
    b, s, _ = qg3.shape
    tm = blocks * WINDOW
    blk = lambda w, col: pl.BlockSpec((1, tm, w), lambda bi, i: (bi, i, col))
    prev = lambda w, col: pl.BlockSpec((1, WINDOW, w), lambda bi, i: (bi, jnp.maximum(i * blocks - 1, 0), col))
    once = dict(pipeline_mode=pl.Buffered(1))
    return pl.pallas_call(
        functools.partial(_attn_prompt_body, blocks=blocks),
        name="attn_prompt",
        grid=(b, s // tm),
        in_specs=[
            pl.BlockSpec(memory_space=pltpu.SMEM),
            blk(D_ATTN, 0), blk(D_ATTN, 1),
            prev(D_KV, 0), blk(D_KV, 0),
            prev(D_KV, 1), blk(D_KV, 1),
            pl.BlockSpec((1, D_ATTN), lambda bi, i: (0, 0), **once),
            blk(D_MODEL, 0),
            pl.BlockSpec((D_MODEL, WA_COLS), lambda bi, i: (0, 1), **once),
            pl.BlockSpec((D_MODEL, WT_COLS), lambda bi, i: (0, 0), **once),
        ],
        out_specs=[blk(D_ATTN, 0), blk(D_SSM, 0), blk(CONV_DIM, 0), blk(DT_W, 0)],
        out_shape=[jax.ShapeDtypeStruct((b, s, D_ATTN), BF16),
                   jax.ShapeDtypeStruct((b, s, D_SSM), BF16),
                   jax.ShapeDtypeStruct((b, s, CONV_DIM), F32),
                   jax.ShapeDtypeStruct((b, s, DT_W), F32)],
        scratch_shapes=[pltpu.VMEM((blocks, WINDOW, D_ATTN), F32)],
        compiler_params=_cparams(("parallel", "arbitrary")),
    )(sink, qg3, qg3, kv3, kv3, kv3, kv3, gain, h3, w_in, w_tail)


Q_PAD = 16


def _attn_step_body(sink_ref, q_ref, g_ref, kn_ref, vn_ref, ck_ref, cv_ref, gain_ref,
                    out_ref, ko_ref, vo_ref, o_scr, *, nb, t):
    tk = 2 * WINDOW
    ql = lax.broadcasted_iota(jnp.int32, (Q_PAD, tk), 0)
    ks = lax.broadcasted_iota(jnp.int32, (Q_PAD, tk), 1)
    rel = ql + WINDOW - ks
    mask = (rel >= 0) & (rel < WINDOW) & (ql < t)
    pad_k = jnp.zeros((tk - WINDOW - t, D_KV), F32)
    pad_q = jnp.zeros((Q_PAD - t, D_ATTN), F32)

    def one_sequence(b, carry):
        kk = jnp.concatenate([ck_ref[b], kn_ref[b], pad_k], axis=0)
        vv = jnp.concatenate([cv_ref[b], vn_ref[b], pad_k], axis=0)
        q = jnp.concatenate([q_ref[b], pad_q], axis=0)
        _attend(q, kk, vv, mask, sink_ref, o_scr.at[b])
        out_ref[b] = _gate_norm(o_scr[b, 0:t, :], g_ref[b], gain_ref[...])
        ko_ref[b] = kk[t:t + WINDOW, :]
        vo_ref[b] = vv[t:t + WINDOW, :]
        return carry

    lax.fori_loop(0, nb, one_sequence, 0, unroll=2)


def _attn_step(qg3, kv3, cache_k, cache_v, sink, gain, nb):
    n, t, _ = qg3.shape
    blk = lambda w, col: pl.BlockSpec((nb, t, w), lambda i: (i, 0, col))
    cache = pl.BlockSpec((nb, WINDOW, D_KV), lambda i: (i, 0, 0))
    return pl.pallas_call(
        functools.partial(_attn_step_body, nb=nb, t=t),
        name="attn_step",
        grid=(n // nb,),
        in_specs=[
            pl.BlockSpec(memory_space=pltpu.SMEM),
            blk(D_ATTN, 0), blk(D_ATTN, 1),
            blk(D_KV, 0), blk(D_KV, 1),
            cache,
            cache,
            pl.BlockSpec((1, D_ATTN), lambda i: (0, 0)),
        ],
        out_specs=[pl.BlockSpec((nb, t, D_ATTN), lambda i: (i, 0, 0)), cache, cache],
        out_shape=[
            jax.ShapeDtypeStruct((n, t, D_ATTN), F32),
            jax.ShapeDtypeStruct((n, WINDOW, D_KV), F32),
            jax.ShapeDtypeStruct((n, WINDOW, D_KV), F32),
        ],
        scratch_shapes=[pltpu.VMEM((nb, Q_PAD, D_ATTN), F32)],
        compiler_params=_cparams(("parallel",)),
    )(sink, qg3, qg3, kv3, kv3, cache_k, cache_v, gain)


ROWS = 128


def _expand_heads(v):
    r = v.shape[0]
    lo = lax.broadcasted_iota(jnp.int32, (r, LANES), 1) < HEAD_DIM
    cols = []
    for c in range(D_SSM // LANES):
        va = jnp.broadcast_to(v[:, 2 * c:2 * c + 1], (r, LANES))
        vb = jnp.broadcast_to(v[:, 2 * c + 1:2 * c + 2], (r, LANES))
        cols.append(jnp.where(lo, va, vb))
    return jnp.concatenate(cols, axis=1)


def _conv_taps(load_rows, w_ref, b_ref):
    acc = b_ref[...] + w_ref[CONV_K - 1:CONV_K, :] * load_rows(CONV_K - 1)
    for k in range(CONV_K - 1):
        acc = acc + w_ref[k:k + 1, :] * load_rows(k)
    return _silu(acc)


def _conv_rolled(x, tail_scr, w_ref, b_ref):
    taps = CONV_K - 1
    width = x.shape[1]
    tiles = ROWS // SUBLANES
    with_prev = jnp.concatenate([tail_scr[...], x], axis=0).reshape(tiles + 1, SUBLANES, width)
    row = lax.broadcasted_iota(jnp.int32, (tiles, SUBLANES, width), 1)
    acc = b_ref[...] + w_ref[taps:taps + 1, :] * x
    for j in range(1, CONV_K):
        rot = pltpu.roll(with_prev, j, axis=1)
        shifted = jnp.where(row < j, rot[:tiles], rot[1:]).reshape(ROWS, width)
        acc = acc + w_ref[taps - j:taps - j + 1, :] * shifted
    tail_scr[...] = x[ROWS - SUBLANES:, :]
    return _silu(acc)


def _ssd_intra(xs, bm, cm, dt_raw, dtb_ref, alog_ref, causal, same, ybuf, after_pair=None):
    dt = jax.nn.softplus(dt_raw + dtb_ref[...])
    a_neg = -jnp.exp(alog_ref[...])
    dta = dt * (a_neg * LOG2E)
    hi = lax.Precision.HIGHEST
    a = jnp.dot(causal.astype(F32), dta, precision=hi, preferred_element_type=F32)
    tot = jnp.dot(same.astype(F32), dta, precision=hi, preferred_element_type=F32)
    r_t = (a - jnp.log2(dt)).T
    lo = lax.broadcasted_iota(jnp.int32, (ROWS, LANES), 1) < HEAD_DIM
    heads_per_group = N_HEADS_SSM // N_GROUPS
    for g in range(N_GROUPS):
        cg = cm[:, g * D_STATE:(g + 1) * D_STATE].astype(BF16)
        bg = bm[:, g * D_STATE:(g + 1) * D_STATE].astype(BF16)
        cb = lax.dot_general(cg, bg, (((1,), (1,)), ((), ())), preferred_element_type=F32)
        for j in range(heads_per_group // 2):
            c = g * (heads_per_group // 2) + j
            ms = []
            for e in range(2):
                h = 2 * c + e
                diff = jnp.broadcast_to(a[:, h:h + 1], (ROWS, ROWS)) - r_t[h:h + 1, :]
                ms.append(cb * jnp.exp2(jnp.where(causal, diff, -jnp.inf)))
            lhs = jnp.concatenate(ms, axis=1).astype(BF16)
            xc = xs[:, c * LANES:(c + 1) * LANES]
            zero = jnp.zeros_like(xc)
            rhs = jnp.concatenate([jnp.where(lo, xc, zero), jnp.where(lo, zero, xc)], axis=0).astype(BF16)
            ybuf[:, c * LANES:(c + 1) * LANES] = jnp.dot(lhs, rhs, preferred_element_type=F32)
            if after_pair is not None:
                after_pair()
    e_expa = _expand_heads(jnp.exp2(a))
    e_w = _expand_heads(jnp.exp2(tot - a) * dt)
    return e_expa, e_w


def _gated_group_norm(y, z, gain_ref, out_ref_store):
    yz = y * _silu(z)
    for g in range(N_GROUPS):
        blk = yz[:, g * GROUP_W:(g + 1) * GROUP_W]
        nrm = blk * lax.rsqrt(jnp.mean(blk * blk, axis=-1, keepdims=True) + EPS)
        out_ref_store(g, nrm * gain_ref[:, g * GROUP_W:(g + 1) * GROUP_W])


OUT_CHUNK = 256


def _ssd_out_body(z_ref, xs_ref, b_ref, c_ref, dt_ref, attn_ref, x_ref,
                  cwx, cwb, cwc, cbx, cbb, cbc, dtb_ref, alog_ref, dskip_ref, gain_ref, w_ref, post_ref,
                  y_ref, cvx_ref, cvb_ref, cvc_ref, st_ref,
                  tlx, tlb, tlc, h_t, ybuf, mix, o_acc, *, chunks, steps_per_seq, total):
    s = pl.program_id(0)
    c2 = jnp.minimum(s, total - 1) % steps_per_seq

    @pl.when(s == 0)
    def _():
        mix[0] = jnp.zeros((chunks * ROWS, D_SSM), BF16)

    @pl.when(c2 == 0)
    def _():
        tlx[...] = jnp.zeros_like(tlx)
        tlb[...] = jnp.zeros_like(tlb)
        tlc[...] = jnp.zeros_like(tlc)
        h_t[...] = jnp.zeros_like(h_t)

    mix[1] = mix[0]
    slabs = []
    a_prev = attn_ref[0]
    m_prev = mix[1]
    for n in range(0, D_MODEL, OUT_CHUNK):
        def attn_half(n=n):
            o_acc[:, n:n + OUT_CHUNK] = jnp.dot(a_prev, w_ref[0:D_ATTN, n:n + OUT_CHUNK],
                                                preferred_element_type=F32)

        def ssm_half(n=n):
            o_acc[:, n:n + OUT_CHUNK] += jnp.dot(m_prev, w_ref[D_ATTN:D_ATTN + D_SSM, n:n + OUT_CHUNK],
                                                 preferred_element_type=F32)
        slabs += [attn_half, ssm_half]
    points_per_chunk = N_HEADS_SSM // 2 + N_GROUPS + 2
    slots = chunks * points_per_chunk
    issued = [0]

    def interleave():
        k = issued[0]
        for run in slabs[k * len(slabs) // slots:(k + 1) * len(slabs) // slots]:
            run()
        issued[0] = k + 1

    rl = lax.broadcasted_iota(jnp.int32, (ROWS, ROWS), 0)
    cs = lax.broadcasted_iota(jnp.int32, (ROWS, ROWS), 1)
    causal = cs <= rl
    same = cs >= 0
    live = {}

    def conv_phase(ch):
        rows = slice(ch * ROWS, (ch + 1) * ROWS)
        xs = _conv_rolled(xs_ref[0, rows, :], tlx, cwx, cbx)
        bm = _conv_rolled(b_ref[0, rows, :], tlb, cwb, cbb)
        cm = _conv_rolled(c_ref[0, rows, :], tlc, cwc, cbc)
        live[ch] = dict(xs=xs, bm=bm, cm=cm)
        interleave()

    def intra_phase(ch):
        v = live[ch]
        rows = slice(ch * ROWS, (ch + 1) * ROWS)
        v["e_expa"], v["e_w"] = _ssd_intra(v["xs"], v["bm"], v["cm"], dt_ref[0, rows, :], dtb_ref, alog_ref,
                                           causal, same, ybuf.at[ch], after_pair=interleave)

    def state_phase(ch):
        v = live[ch]
        yb = ybuf.at[ch]
        xw = (v["xs"] * v["e_w"]).astype(BF16)
        for g in range(N_GROUPS):
            gs = slice(g * GROUP_W, (g + 1) * GROUP_W)
            ds = slice(g * D_STATE, (g + 1) * D_STATE)
            h_prev = h_t[:, gs]
            y_inter = jnp.dot(v["cm"][:, ds].astype(BF16), h_prev.astype(BF16), preferred_element_type=F32)
            yb[:, gs] = yb[:, gs] + y_inter * v["e_expa"][:, gs]
            s_t = jnp.dot(v["bm"][:, ds].T.astype(BF16), xw[:, gs], preferred_element_type=F32)
            h_t[:, gs] = h_prev * v["e_expa"][ROWS - 1:ROWS, gs] + s_t
            interleave()

    def norm_phase(ch):
        v = live.pop(ch)
        rows = slice(ch * ROWS, (ch + 1) * ROWS)
        y = ybuf[ch] + dskip_ref[...] * v["xs"]

        def store(g, val):
            mix[0, rows, g * GROUP_W:(g + 1) * GROUP_W] = val.astype(BF16)
        _gated_group_norm(y, z_ref[0, rows, :].astype(F32), gain_ref, store)
        interleave()

    conv_phase(0)
    intra_phase(0)
    for ch in range(chunks):
        if ch + 1 < chunks:
            conv_phase(ch + 1)
        state_phase(ch)
        if ch + 1 < chunks:
            intra_phase(ch + 1)
        norm_phase(ch)

    o = o_acc[...]
    o = o * lax.rsqrt(jnp.mean(o * o, axis=-1, keepdims=True) + EPS)
    y_ref[0] = x_ref[0] + o * post_ref[...]

    @pl.when((c2 == steps_per_seq - 1) & (s < total))
    def _():
        first = SUBLANES - (CONV_K - 1)
        cvx_ref[0] = tlx[first:SUBLANES, :]
        cvb_ref[0] = tlb[first:SUBLANES, :]
        cvc_ref[0] = tlc[first:SUBLANES, :]
        st_ref[0] = h_t[...].T


def _ssd_outproj(z3, xbc3, dt3, attn3, x3, p, chunks):
    b, s, _ = z3.shape
    tm = chunks * ROWS
    per_seq = s // tm
    total = b * per_seq
    cur = lambda i: jnp.minimum(i, total - 1)
    prv = lambda i: jnp.maximum(i - 1, 0)
    blk = lambda w, col: pl.BlockSpec((1, tm, w), lambda i: (cur(i) // per_seq, cur(i) % per_seq, col))
    lag = lambda w: pl.BlockSpec((1, tm, w), lambda i: (prv(i) // per_seq, prv(i) % per_seq, 0))
    par = lambda r, w: pl.BlockSpec((r, w), lambda i: (0, 0), pipeline_mode=pl.Buffered(1))
    per_b = lambda r, w: pl.BlockSpec((1, r, w), lambda i: (cur(i) // per_seq, 0, 0))
    tail = CONV_K - 1
    return pl.pallas_call(
        functools.partial(_ssd_out_body, chunks=chunks, steps_per_seq=per_seq, total=total),
        name="ssd_outproj",
        grid=(total + 1,),
        in_specs=[
            blk(D_SSM, 0), blk(D_SSM, 0),
            blk(BC_W, D_SSM // BC_W), blk(BC_W, D_SSM // BC_W + 1),
            blk(DT_W, 0),
            lag(D_ATTN), lag(D_MODEL),
            par(CONV_K, D_SSM), par(CONV_K, BC_W), par(CONV_K, BC_W),
            par(1, D_SSM), par(1, BC_W), par(1, BC_W),
            par(1, DT_W), par(1, DT_W), par(1, D_SSM), par(1, D_SSM),
            par(D_ATTN + D_SSM, D_MODEL), par(1, D_MODEL),
        ],
        out_specs=[
            lag(D_MODEL),
            per_b(tail, D_SSM), per_b(tail, BC_W), per_b(tail, BC_W),
            per_b(D_SSM, D_STATE),
        ],
        out_shape=[
            jax.ShapeDtypeStruct((b, s, D_MODEL), F32),
            jax.ShapeDtypeStruct((b, tail, D_SSM), F32),
            jax.ShapeDtypeStruct((b, tail, BC_W), F32),
            jax.ShapeDtypeStruct((b, tail, BC_W), F32),
            jax.ShapeDtypeStruct((b, D_SSM, D_STATE), F32),
        ],
        scratch_shapes=[
            pltpu.VMEM((SUBLANES, D_SSM), F32),
            pltpu.VMEM((SUBLANES, BC_W), F32),
            pltpu.VMEM((SUBLANES, BC_W), F32),
            pltpu.VMEM((D_STATE, D_SSM), F32),
            pltpu.VMEM((chunks, ROWS, D_SSM), F32),
            pltpu.VMEM((2, tm, D_SSM), BF16),
            pltpu.VMEM((tm, D_MODEL), F32),
        ],
        compiler_params=_cparams(("arbitrary",)),
    )(z3, xbc3, xbc3, xbc3, dt3, attn3, x3,
      p["cwx"], p["cwb"], p["cwc"], p["cbx"], p["cbb"], p["cbc"],
      p["dtb"], p["alog"], p["dskip"], p["ssm_gain"], p["w_out"], p["norm_post"])


def _ssd_step_body(z_ref, xs_ref, b_ref, c_ref, dt_ref, sx_ref, sb_ref, sc_ref, h0_ref,
                   cwx, cwb, cwc, cbx, cbb, cbc, dtb_ref, alog_ref, dskip_ref, gain_ref,
                   out_ref, cvx_ref, cvb_ref, cvc_ref, st_ref,
                   xpx, xpb, xpc, ybuf, cm_scr, ea_scr, ea_t_scr, xw_t_scr, *, nb, t):
    real = nb * t
    tail = CONV_K - 1
    lo_row = SUBLANES - tail

    def conv(xp, new_ref, st_in_ref, st_out_ref, w_ref, b_ref):
        xp[:, lo_row:SUBLANES, :] = st_in_ref[...]
        xp[:, SUBLANES:SUBLANES + t, :] = new_ref[...]
        st_out_ref[...] = xp[:, SUBLANES + t - tail:SUBLANES + t, :]
        w = new_ref.shape[-1]
        act = _conv_taps(lambda k: xp[:, lo_row + k:lo_row + k + t, :].reshape(real, w), w_ref, b_ref)
        return jnp.concatenate([act, jnp.zeros((ROWS - real, w), F32)], axis=0)

    xs = conv(xpx, xs_ref, sx_ref, cvx_ref, cwx, cbx)
    bm = conv(xpb, b_ref, sb_ref, cvb_ref, cwb, cbb)
    cm = conv(xpc, c_ref, sc_ref, cvc_ref, cwc, cbc)
    dt_raw = jnp.concatenate([dt_ref[...].reshape(real, DT_W), jnp.zeros((ROWS - real, DT_W), F32)], axis=0)

    rl = lax.broadcasted_iota(jnp.int32, (ROWS, ROWS), 0)
    cs = lax.broadcasted_iota(jnp.int32, (ROWS, ROWS), 1)
    same = (rl // t) == (cs // t)
    causal = same & (cs <= rl)
    e_expa, e_w = _ssd_intra(xs, bm, cm, dt_raw, dtb_ref, alog_ref, causal, same, ybuf)

    cm_scr[...] = cm
    ea_scr[...] = e_expa
    ea_t_scr[...] = e_expa.T
    xw_t_scr[...] = (xs * e_w).T.astype(BF16)
    row_id = lax.broadcasted_iota(jnp.int32, (ROWS, D_STATE), 0)
    pair = 2 * SUBLANES
    for b in range(nb):
        in_seq = (row_id >= b * t) & (row_id < (b + 1) * t)
        r0 = b * t
        last_col = r0 + t - 1
        for g in range(N_GROUPS):
            gs = slice(g * GROUP_W, (g + 1) * GROUP_W)
            ds = slice(g * D_STATE, (g + 1) * D_STATE)
            h0 = h0_ref[b, gs, :]
            c_rows = cm_scr[r0:r0 + pair, ds].astype(BF16)
            y_inter = lax.dot_general(c_rows, h0.astype(BF16), (((1,), (1,)), ((), ())),
                                      preferred_element_type=F32)
            ybuf[r0:r0 + t, gs] = ybuf[r0:r0 + t, gs] + y_inter[0:t] * ea_scr[r0:r0 + t, gs]
            b_rows = jnp.where(in_seq, bm[:, ds], 0.0).astype(BF16)
            s_new = jnp.dot(xw_t_scr[gs, :], b_rows, preferred_element_type=F32)
            decay = jnp.broadcast_to(ea_t_scr[gs, last_col:last_col + 1], (GROUP_W, D_STATE))
            st_ref[b, gs, :] = h0 * decay + s_new

    y = ybuf[0:real, :] + dskip_ref[...] * xs[0:real, :]
    z = z_ref[...].reshape(real, D_SSM)

    def store(g, v):
        out_ref[:, :, g * GROUP_W:(g + 1) * GROUP_W] = v.reshape(nb, t, GROUP_W)
    _gated_group_norm(y, z, gain_ref, store)


def _ssd_step(z3, xbc3, dt3, conv_state, h0, p, nb):
    n, t, _ = z3.shape
    tail = CONV_K - 1
    blk = lambda w, col: pl.BlockSpec((nb, t, w), lambda i: (i, 0, col))
    cst = lambda w, col: pl.BlockSpec((nb, tail, w), lambda i: (i, 0, col))
    par = lambda r, w: pl.BlockSpec((r, w), lambda i: (0, 0))
    state = pl.BlockSpec((nb, D_SSM, D_STATE), lambda i: (i, 0, 0))
    return pl.pallas_call(
        functools.partial(_ssd_step_body, nb=nb, t=t),
        name="ssd_step",
        grid=(n // nb,),
        in_specs=[
            blk(D_SSM, 0), blk(D_SSM, 0),
            blk(BC_W, D_SSM // BC_W), blk(BC_W, D_SSM // BC_W + 1),
            blk(DT_W, 0),
            cst(D_SSM, 0), cst(BC_W, D_SSM // BC_W), cst(BC_W, D_SSM // BC_W + 1),
            state,
            par(CONV_K, D_SSM), par(CONV_K, BC_W), par(CONV_K, BC_W),
            par(1, D_SSM), par(1, BC_W), par(1, BC_W),
            par(1, DT_W), par(1, DT_W), par(1, D_SSM), par(1, D_SSM),
        ],
        out_specs=[
            blk(D_SSM, 0),
            cst(D_SSM, 0), cst(BC_W, 0), cst(BC_W, 0),
            state,
        ],
        out_shape=[
            jax.ShapeDtypeStruct((n, t, D_SSM), F32),
            jax.ShapeDtypeStruct((n, tail, D_SSM), F32),
            jax.ShapeDtypeStruct((n, tail, BC_W), F32),
            jax.ShapeDtypeStruct((n, tail, BC_W), F32),
            jax.ShapeDtypeStruct((n, D_SSM, D_STATE), F32),
        ],
        scratch_shapes=[
            pltpu.VMEM((nb, SUBLANES + t, D_SSM), F32),
            pltpu.VMEM((nb, SUBLANES + t, BC_W), F32),
            pltpu.VMEM((nb, SUBLANES + t, BC_W), F32),
            pltpu.VMEM((ROWS, D_SSM), F32),
            pltpu.VMEM((ROWS, BC_W), F32),
            pltpu.VMEM((ROWS, D_SSM), F32),
            pltpu.VMEM((D_SSM, ROWS), F32),
            pltpu.VMEM((D_SSM, ROWS), BF16),
        ],
        compiler_params=_cparams(("parallel",)),
    )(z3, xbc3, xbc3, xbc3, dt3, conv_state, conv_state, conv_state, h0,
      p["cwx"], p["cwb"], p["cwc"], p["cbx"], p["cbb"], p["cbc"],
      p["dtb"], p["alog"], p["dskip"], p["ssm_gain"])


def _layer_params(norm_pre, w_in, attn_sink, attn_norm, conv_w, conv_b, dt_bias, a_log, d_skip,
                  ssm_norm, w_out, norm_post):
    w_bf16 = w_in.astype(BF16)
    w_tail = jnp.pad(w_bf16[:, 2 * WA_COLS:], ((0, 0), (0, DT_W - N_HEADS_SSM)))
    pad_h = lambda v: jnp.pad(v, (0, DT_W - N_HEADS_SSM)).reshape(1, DT_W)
    return dict(
        norm_pre=norm_pre.reshape(1, D_MODEL), w_in=w_bf16, w_tail=w_tail,
        sink=attn_sink, attn_gain=attn_norm.reshape(1, D_ATTN),
        cwx=conv_w[:, :D_SSM], cwb=conv_w[:, D_SSM:D_SSM + BC_W], cwc=conv_w[:, D_SSM + BC_W:],
        cbx=conv_b[:D_SSM].reshape(1, -1), cbb=conv_b[D_SSM:D_SSM + BC_W].reshape(1, -1),
        cbc=conv_b[D_SSM + BC_W:].reshape(1, -1),
        dtb=pad_h(dt_bias), alog=pad_h(a_log),
        dskip=jnp.broadcast_to(d_skip[:, None], (N_HEADS_SSM, HEAD_DIM)).reshape(1, D_SSM),
        ssm_gain=ssm_norm.reshape(1, D_SSM),
        w_out=w_out.astype(BF16),
        norm_post=norm_post.reshape(1, D_MODEL),
    )


IN_TM, OUT_TM = 512, 512
ATTN_BLOCKS = 2
SSD_CHUNKS = 2
STEP_NB_ATTN, STEP_NB_SSD = 8, 8


def _layer(x, caches, p):
    n, t, _ = x.shape
    x2d = x.reshape(n * t, D_MODEL)
    act_dtype = BF16 if caches is None else F32
    h, qg, kv = _inproj_attn(x2d, p["norm_pre"], p["w_in"], IN_TM, act_dtype)
    qg3, kv3 = qg.reshape(n, t, 2 * D_ATTN), kv.reshape(n, t, 2 * D_KV)
    if caches is None:
        attn, z3, xbc3, dt3 = _attn_prompt(qg3, kv3, h.reshape(n, t, D_MODEL), p["w_in"], p["w_tail"],
                                           p["sink"], p["attn_gain"], ATTN_BLOCKS)
        new_k = kv3[:, t - WINDOW:, :D_KV]
        new_v = kv3[:, t - WINDOW:, D_KV:]
        y, cvx, cvb, cvc, h_new = _ssd_outproj(z3, xbc3, dt3, attn, x, p, SSD_CHUNKS)
    else:
        z, xbc, dt = _inproj_ssm(h, p["w_in"], p["w_tail"], IN_TM, act_dtype)
        z3, xbc3, dt3 = z.reshape(n, t, D_SSM), xbc.reshape(n, t, CONV_DIM), dt.reshape(n, t, DT_W)
        kbuf, vbuf, conv_buf, h0 = caches
        attn, new_k, new_v = _attn_step(qg3, kv3, kbuf.reshape(n, WINDOW, D_KV), vbuf.reshape(n, WINDOW, D_KV),
                                        p["sink"], p["attn_gain"], STEP_NB_ATTN)
        ssm, cvx, cvb, cvc, h_new = _ssd_step(z3, xbc3, dt3, conv_buf, h0.reshape(n, D_SSM, D_STATE), p,
                                              STEP_NB_SSD)
        y = _outproj(attn.reshape(n * t, D_ATTN), ssm.reshape(n * t, D_SSM), x2d,
                     p["w_out"], p["norm_post"], OUT_TM)
    return (y.reshape(n, t, D_MODEL),
            new_k.reshape(n, WINDOW, N_KV_HEADS, HEAD_DIM), new_v.reshape(n, WINDOW, N_KV_HEADS, HEAD_DIM),
            jnp.concatenate([cvx, cvb, cvc], axis=-1),
            h_new.reshape(n, N_HEADS_SSM, HEAD_DIM, D_STATE))


def kernel(x_prompt, x_sample, cache_k, cache_v, state_conv, state_ssm, norm_pre, w_in, attn_sink,
           attn_norm, conv_w, conv_b, dt_bias, a_log, d_skip, ssm_norm, w_out, norm_post):
    depth = w_in.shape[0]
    yp, ys = x_prompt, x_sample
    outs_p, outs_s = [], []
    for l in range(depth):
        p = _layer_params(norm_pre[l], w_in[l], attn_sink[l], attn_norm[l], conv_w[l], conv_b[l],
                          dt_bias[l], a_log[l], d_skip[l], ssm_norm[l], w_out[l], norm_post[l])
        yp, *rest_p = _layer(yp, None, p)
        ys, *rest_s = _layer(ys, (cache_k[l], cache_v[l], state_conv[l], state_ssm[l]), p)
        outs_p.append(rest_p)
        outs_s.append(rest_s)
    stack = lambda outs, i: jnp.stack([o[i] for o in outs])
    return (yp, ys,
            stack(outs_p, 0), stack(outs_p, 1), stack(outs_p, 2), stack(outs_p, 3),
            stack(outs_s, 0), stack(outs_s, 1), stack(outs_s, 2), stack(outs_s, 3))
```

```python
import functools

import jax
import jax.numpy as jnp
from jax import lax
from jax.experimental import pallas as pl
from jax.experimental.pallas import tpu as pltpu

F32 = jnp.float32
BF16 = jnp.bfloat16

D_MODEL = 2048
HEAD_DIM = 64
N_KV_HEADS = 4
D_ATTN = 2048
D_KV = 256
WINDOW = 128
D_SSM = 2048
N_HEADS_SSM = 32
N_GROUPS = 4
D_STATE = 128
CONV_K = 4
GROUP_W = D_SSM // N_GROUPS
BC_W = N_GROUPS * D_STATE
CONV_DIM = D_SSM + 2 * BC_W
EPS = 1e-6
LOG2E = 1.4426950408889634
LANES = 128
SUBLANES = 8

DT_W = LANES
WA_COLS = 2 * D_ATTN + 2 * D_KV
WS_COLS = D_SSM + CONV_DIM + DT_W
WT_COLS = WS_COLS - WA_COLS

VMEM_LIMIT = 56 * 1024 * 1024


def _silu(x):
    h = 0.5 * x
    return h + h * jnp.tanh(h)


def _cparams(sem):
    return pltpu.CompilerParams(dimension_semantics=sem, vmem_limit_bytes=VMEM_LIMIT)


def _spread(slabs, points):
    ticks, issued = [0], [0]

    def run_until(stop):
        for run in slabs[issued[0]:stop]:
            run()
        issued[0] = max(issued[0], stop)

    def tick():
        ticks[0] += 1
        run_until(min(ticks[0], points) * len(slabs) // points)

    return tick, lambda: run_until(len(slabs))


def _inproj_attn_body(x_ref, g_ref, w_ref, h_ref, qg_ref, kv_ref, *, row_chunk):
    tm = x_ref.shape[0]

    def norm_rows(r, carry):
        rows = pl.ds(pl.multiple_of(r * row_chunk, row_chunk), row_chunk)
        x = x_ref[rows, :]
        y = x * lax.rsqrt(jnp.mean(x * x, axis=-1, keepdims=True) + EPS)
        h_ref[rows, :] = (y * g_ref[...]).astype(BF16)
        return carry
    lax.fori_loop(0, tm // row_chunk, norm_rows, 0)

    h = h_ref[...]
    dot = lambda lo, hi: jnp.dot(h, w_ref[:, lo:hi], preferred_element_type=F32)
    qg_ref[:, 0:D_ATTN] = dot(0, D_ATTN).astype(qg_ref.dtype)
    kv_ref[...] = dot(D_ATTN, D_ATTN + 2 * D_KV)
    qg_ref[:, D_ATTN:2 * D_ATTN] = dot(D_ATTN + 2 * D_KV, WA_COLS).astype(qg_ref.dtype)


def _inproj_ssm_body(h_ref, w_ref, wt_ref, z_ref, xbc_ref, dt_ref):
    for run in _ssm_projection_steps(h_ref[...], w_ref, wt_ref, z_ref, xbc_ref, dt_ref):
        run()


PROJ_CHUNK = 256


def _ssm_projection_steps(h, w_ref, wt_ref, z_ref, xbc_ref, dt_ref):
    lead = z_ref.shape[:-1]
    rows = (slice(None),) * len(lead)

    def slab(out_ref, out_lo, w_lo, width):
        def run():
            w = w_ref[:, w_lo:w_lo + width] if w_lo < WA_COLS else wt_ref[:, w_lo - WA_COLS:w_lo - WA_COLS + width]
            r = jnp.dot(h, w, preferred_element_type=F32)
            out_ref[rows + (slice(out_lo, out_lo + width),)] = r.astype(out_ref.dtype).reshape(*lead, width)
        return run

    steps = [slab(z_ref, c, c, PROJ_CHUNK) for c in range(0, D_SSM, PROJ_CHUNK)]
    steps += [slab(xbc_ref, c, D_SSM + c, PROJ_CHUNK) for c in range(0, CONV_DIM, PROJ_CHUNK)]
    steps.append(slab(dt_ref, 0, D_SSM + CONV_DIM, DT_W))
    return steps


def _inproj_attn(x2d, gain, w_in, tm, act_dtype):
    m = x2d.shape[0]
    tm = min(tm, m)
    row = lambda w: pl.BlockSpec((tm, w), lambda i: (i, 0))
    resident = lambda r, w: pl.BlockSpec((r, w), lambda i: (0, 0), pipeline_mode=pl.Buffered(1))
    return pl.pallas_call(
        functools.partial(_inproj_attn_body, row_chunk=min(128, tm)),
        name="inproj_attn",
        grid=(m // tm,),
        in_specs=[row(D_MODEL), resident(1, D_MODEL), resident(D_MODEL, WA_COLS)],
        out_specs=[row(D_MODEL), row(2 * D_ATTN), row(2 * D_KV)],
        out_shape=[jax.ShapeDtypeStruct((m, D_MODEL), BF16),
                   jax.ShapeDtypeStruct((m, 2 * D_ATTN), act_dtype),
                   jax.ShapeDtypeStruct((m, 2 * D_KV), F32)],
        compiler_params=_cparams(("parallel",)),
    )(x2d, gain, w_in)


def _inproj_ssm(h, w_in, w_tail, tm, act_dtype):
    m = h.shape[0]
    tm = min(tm, m)
    row = lambda w: pl.BlockSpec((tm, w), lambda i: (i, 0))
    once = dict(pipeline_mode=pl.Buffered(1))
    return pl.pallas_call(
        _inproj_ssm_body,
        name="inproj_ssm",
        grid=(m // tm,),
        in_specs=[row(D_MODEL),
                  pl.BlockSpec((D_MODEL, WA_COLS), lambda i: (0, 1), **once),
                  pl.BlockSpec((D_MODEL, WT_COLS), lambda i: (0, 0), **once)],
        out_specs=[row(D_SSM), row(CONV_DIM), row(DT_W)],
        out_shape=[jax.ShapeDtypeStruct((m, D_SSM), act_dtype),
                   jax.ShapeDtypeStruct((m, CONV_DIM), F32),
                   jax.ShapeDtypeStruct((m, DT_W), F32)],
        compiler_params=_cparams(("parallel",)),
    )(h, w_in, w_tail)


def _outproj_body(a_ref, s_ref, x_ref, wa_ref, ws_ref, g_ref, y_ref):
    o = jnp.dot(a_ref[...].astype(BF16), wa_ref[...], preferred_element_type=F32)
    o = o + jnp.dot(s_ref[...].astype(BF16), ws_ref[...], preferred_element_type=F32)
    o = o * lax.rsqrt(jnp.mean(o * o, axis=-1, keepdims=True) + EPS)
    y_ref[...] = x_ref[...] + o * g_ref[...]


def _outproj(attn, ssm, x2d, w_out, gain, tm):
    m = x2d.shape[0]
    tm = min(tm, m)
    row = lambda i: (i, 0)
    once = dict(pipeline_mode=pl.Buffered(1))
    return pl.pallas_call(
        _outproj_body,
        name="outproj",
        grid=(m // tm,),
        in_specs=[
            pl.BlockSpec((tm, D_ATTN), row),
            pl.BlockSpec((tm, D_SSM), row),
            pl.BlockSpec((tm, D_MODEL), row),
            pl.BlockSpec((D_ATTN, D_MODEL), lambda i: (0, 0), **once),
            pl.BlockSpec((D_SSM, D_MODEL), lambda i: (1, 0), **once),
            pl.BlockSpec((1, D_MODEL), lambda i: (0, 0), **once),
        ],
        out_specs=pl.BlockSpec((tm, D_MODEL), row),
        out_shape=jax.ShapeDtypeStruct((m, D_MODEL), F32),
        compiler_params=_cparams(("parallel",)),
    )(attn, ssm, x2d, w_out, w_out, gain)


def _head_pair_blockdiag(slab):
    lo = lax.broadcasted_iota(jnp.int32, slab.shape, 1) < HEAD_DIM
    rolled = pltpu.roll(slab, HEAD_DIM, axis=1)
    zero = jnp.zeros_like(slab)
    bd_a = jnp.concatenate([jnp.where(lo, slab, zero), jnp.where(lo, zero, rolled)], axis=0)
    bd_b = jnp.concatenate([jnp.where(lo, rolled, zero), jnp.where(lo, zero, slab)], axis=0)
    return bd_a.astype(BF16), bd_b.astype(BF16)


ATTEND_TICKS = N_KV_HEADS * ((D_ATTN // N_KV_HEADS) // LANES + 2)


def _attend(qs, kks, vvs, masks, sink_ref, o_scrs, tick=lambda: None):
    tq, tk = masks[0].shape
    seqs = range(len(qs))
    kbd, vbd = [], []
    for b in seqs:
        kk = kks[b] * (HEAD_DIM ** -0.5 * LOG2E)
        kb, vb = [], []
        for c in range(D_KV // LANES):
            kb.extend(_head_pair_blockdiag(kk[:, c * LANES:(c + 1) * LANES]))
            vb.extend(_head_pair_blockdiag(vvs[b][:, c * LANES:(c + 1) * LANES]))
        kbd.append(kb)
        vbd.append(vb)
    lo = lax.broadcasted_iota(jnp.int32, (tq, LANES), 1) < HEAD_DIM
    pairs_per_group = (D_ATTN // N_KV_HEADS) // LANES
    for g in range(N_KV_HEADS):
        cols = [g * pairs_per_group + j for j in range(pairs_per_group)]
        s = []
        for b in seqs:
            q_stack = jnp.concatenate([qs[b][:, c * LANES:(c + 1) * LANES] for c in cols], axis=0).astype(BF16)
            s.append(lax.dot_general(q_stack, kbd[b][g], (((1,), (1,)), ((), ())), preferred_element_type=F32))
        tick()
        p_rows = [[] for _ in seqs]
        inv_rows = [[] for _ in seqs]
        for j, col in enumerate(cols):
            ps = [[] for _ in seqs]
            invs = [[] for _ in seqs]
            for e in range(2):
                snk = sink_ref[2 * col + e] * LOG2E
                for b in seqs:
                    se = jnp.where(masks[b], s[b][j * tq:(j + 1) * tq, e * tk:(e + 1) * tk], -jnp.inf)
                    mx = jnp.maximum(jnp.max(se, axis=-1, keepdims=True), snk)
                    p = jnp.exp2(se - mx)
                    den = jnp.sum(p, axis=-1, keepdims=True) + jnp.exp2(snk - mx)
                    ps[b].append(p)
                    invs[b].append(1.0 / den)
            for b in seqs:
                p_rows[b].append(jnp.concatenate(ps[b], axis=1).astype(BF16))
                inv_rows[b].append(jnp.where(lo, invs[b][0], invs[b][1]))
            tick()
        o = [jnp.dot(jnp.concatenate(p_rows[b], axis=0), vbd[b][g], preferred_element_type=F32) for b in seqs]
        for b in seqs:
            for j, col in enumerate(cols):
                o_scrs[b][:, col * LANES:(col + 1) * LANES] = o[b][j * tq:(j + 1) * tq, :] * inv_rows[b][j]
        tick()


def _gate_norm(o, gate, gain):
    y = o * _silu(gate)
    y = y * lax.rsqrt(jnp.mean(y * y, axis=-1, keepdims=True) + EPS)
    return y * gain


def _attn_prompt_body(sink_ref, q_ref, g_ref, kp_ref, kc_ref, vp_ref, vc_ref, gain_ref, h_ref, w_ref, wt_ref,
                      out_ref, z_ref, xbc_ref, dt_ref, o_scr, *, blocks):
    i = pl.program_id(1)
    ql = lax.broadcasted_iota(jnp.int32, (WINDOW, 2 * WINDOW), 0)
    ks = lax.broadcasted_iota(jnp.int32, (WINDOW, 2 * WINDOW), 1)
    rel = ql + WINDOW - ks
    band = (rel >= 0) & (rel < WINDOW)
    steps = _ssm_projection_steps(h_ref[0], w_ref, wt_ref, z_ref, xbc_ref, dt_ref)
    tick, flush = _spread(steps, ATTEND_TICKS + blocks)

    qs, kks, vvs, masks = [], [], [], []
    for sub in range(blocks):
        rows = slice(sub * WINDOW, (sub + 1) * WINDOW)
        if sub == 0:
            k_prev, v_prev = kp_ref[0], vp_ref[0]
            first_key = jnp.where(i > 0, 0, WINDOW)
            masks.append(band & (ks >= first_key))
        else:
            before = slice((sub - 1) * WINDOW, sub * WINDOW)
            k_prev, v_prev = kc_ref[0, before, :], vc_ref[0, before, :]
            masks.append(band)
        qs.append(q_ref[0, rows, :])
        kks.append(jnp.concatenate([k_prev, kc_ref[0, rows, :]], axis=0))
        vvs.append(jnp.concatenate([v_prev, vc_ref[0, rows, :]], axis=0))
    _attend(qs, kks, vvs, masks, sink_ref, [o_scr.at[sub] for sub in range(blocks)], tick)
    for sub in range(blocks):
        rows = slice(sub * WINDOW, (sub + 1) * WINDOW)
        out_ref[0, rows, :] = _gate_norm(o_scr[sub], g_ref[0, rows, :].astype(F32), gain_ref[...]).astype(BF16)
        tick()
    flush()


def _attn_prompt(qg3, kv3, h3, w_in, w_tail, sink, gain, blocks):
    b, s, _ = qg3.shape
    tm = blocks * WINDOW
    blk = lambda w, col: pl.BlockSpec((1, tm, w), lambda bi, i: (bi, i, col))
    prev = lambda w, col: pl.BlockSpec((1, WINDOW, w), lambda bi, i: (bi, jnp.maximum(i * blocks - 1, 0), col))
    once = dict(pipeline_mode=pl.Buffered(1))
    return pl.pallas_call(
        functools.partial(_attn_prompt_body, blocks=blocks),
        name="attn_prompt",
        grid=(b, s // tm),
        in_specs=[
            pl.BlockSpec(memory_space=pltpu.SMEM),
            blk(D_ATTN, 0), blk(D_ATTN, 1),
            prev(D_KV, 0), blk(D_KV, 0),
            prev(D_KV, 1), blk(D_KV, 1),
            pl.BlockSpec((1, D_ATTN), lambda bi, i: (0, 0), **once),
            blk(D_MODEL, 0),
            pl.BlockSpec((D_MODEL, WA_COLS), lambda bi, i: (0, 1), **once),
            pl.BlockSpec((D_MODEL, WT_COLS), lambda bi, i: (0, 0), **once),
        ],
        out_specs=[blk(D_ATTN, 0), blk(D_SSM, 0), blk(CONV_DIM, 0), blk(DT_W, 0)],
        out_shape=[jax.ShapeDtypeStruct((b, s, D_ATTN), BF16),
                   jax.ShapeDtypeStruct((b, s, D_SSM), BF16),
                   jax.ShapeDtypeStruct((b, s, CONV_DIM), F32),
                   jax.ShapeDtypeStruct((b, s, DT_W), F32)],
        scratch_shapes=[pltpu.VMEM((blocks, WINDOW, D_ATTN), F32)],
        compiler_params=_cparams(("parallel", "arbitrary")),
    )(sink, qg3, qg3, kv3, kv3, kv3, kv3, gain, h3, w_in, w_tail)


Q_PAD = 16


def _attn_step_body(sink_ref, q_ref, g_ref, kn_ref, vn_ref, ck_ref, cv_ref, gain_ref,
                    out_ref, ko_ref, vo_ref, o_scr, *, nb, t):
    tk = 2 * WINDOW
    ql = lax.broadcasted_iota(jnp.int32, (Q_PAD, tk), 0)
    ks = lax.broadcasted_iota(jnp.int32, (Q_PAD, tk), 1)
    rel = ql + WINDOW - ks
    mask = (rel >= 0) & (rel < WINDOW) & (ql < t)
    pad_k = jnp.zeros((tk - WINDOW - t, D_KV), F32)
    pad_q = jnp.zeros((Q_PAD - t, D_ATTN), F32)

    kks = [jnp.concatenate([ck_ref[b], kn_ref[b], pad_k], axis=0) for b in range(nb)]
    vvs = [jnp.concatenate([cv_ref[b], vn_ref[b], pad_k], axis=0) for b in range(nb)]
    qs = [jnp.concatenate([q_ref[b], pad_q], axis=0) for b in range(nb)]
    for b in range(nb):
        ko_ref[b] = kks[b][t:t + WINDOW, :]
        vo_ref[b] = vvs[b][t:t + WINDOW, :]
    _attend(qs, kks, vvs, [mask] * nb, sink_ref, [o_scr.at[b] for b in range(nb)])
    for b in range(nb):
        out_ref[b] = _gate_norm(o_scr[b, 0:t, :], g_ref[b], gain_ref[...])


def _attn_step(qg3, kv3, cache_k, cache_v, sink, gain, nb):
    n, t, _ = qg3.shape
    blk = lambda w, col: pl.BlockSpec((nb, t, w), lambda i: (i, 0, col))
    cache = pl.BlockSpec((nb, WINDOW, D_KV), lambda i: (i, 0, 0))
    return pl.pallas_call(
        functools.partial(_attn_step_body, nb=nb, t=t),
        name="attn_step",
        grid=(n // nb,),
        in_specs=[
            pl.BlockSpec(memory_space=pltpu.SMEM),
            blk(D_ATTN, 0), blk(D_ATTN, 1),
            blk(D_KV, 0), blk(D_KV, 1),
            cache,
            cache,
            pl.BlockSpec((1, D_ATTN), lambda i: (0, 0)),
        ],
        out_specs=[pl.BlockSpec((nb, t, D_ATTN), lambda i: (i, 0, 0)), cache, cache],
        out_shape=[
            jax.ShapeDtypeStruct((n, t, D_ATTN), F32),
            jax.ShapeDtypeStruct((n, WINDOW, D_KV), F32),
            jax.ShapeDtypeStruct((n, WINDOW, D_KV), F32),
        ],
        scratch_shapes=[pltpu.VMEM((nb, Q_PAD, D_ATTN), F32)],
        compiler_params=_cparams(("parallel",)),
    )(sink, qg3, qg3, kv3, kv3, cache_k, cache_v, gain)


ROWS = 128


def _expand_heads(v):
    r = v.shape[0]
    lo = lax.broadcasted_iota(jnp.int32, (r, LANES), 1) < HEAD_DIM
    cols = []
    for c in range(D_SSM // LANES):
        va = jnp.broadcast_to(v[:, 2 * c:2 * c + 1], (r, LANES))
        vb = jnp.broadcast_to(v[:, 2 * c + 1:2 * c + 2], (r, LANES))
        cols.append(jnp.where(lo, va, vb))
    return jnp.concatenate(cols, axis=1)


def _conv_taps(load_rows, w_ref, b_ref):
    acc = b_ref[...] + w_ref[CONV_K - 1:CONV_K, :] * load_rows(CONV_K - 1)
    for k in range(CONV_K - 1):
        acc = acc + w_ref[k:k + 1, :] * load_rows(k)
    return _silu(acc)


def _conv_rolled(x, tail_scr, w_ref, b_ref):
    taps = CONV_K - 1
    width = x.shape[1]
    tiles = ROWS // SUBLANES
    with_prev = jnp.concatenate([tail_scr[...], x], axis=0).reshape(tiles + 1, SUBLANES, width)
    row = lax.broadcasted_iota(jnp.int32, (tiles, SUBLANES, width), 1)
    acc = b_ref[...] + w_ref[taps:taps + 1, :] * x
    for j in range(1, CONV_K):
        rot = pltpu.roll(with_prev, j, axis=1)
        shifted = jnp.where(row < j, rot[:tiles], rot[1:]).reshape(ROWS, width)
        acc = acc + w_ref[taps - j:taps - j + 1, :] * shifted
    tail_scr[...] = x[ROWS - SUBLANES:, :]
    return _silu(acc)


def _masked_row_sums(mask, x):
    m = jnp.where(mask, 1.0, 0.0).astype(BF16)
    hi = x.astype(BF16)
    rest = x - hi.astype(F32)
    mid = rest.astype(BF16)
    lo = (rest - mid.astype(F32)).astype(BF16)
    dot = lambda piece: jnp.dot(m, piece, preferred_element_type=F32)
    return dot(hi) + dot(mid) + dot(lo)


def _ssd_intra(xs, bm, cm, dt_raw, dtb_ref, alog_ref, causal, same, ybuf, after_pair=None):
    dt = jax.nn.softplus(dt_raw + dtb_ref[...])
    a_neg = -jnp.exp(alog_ref[...])
    dta = dt * (a_neg * LOG2E)
    a = _masked_row_sums(causal, dta)
    tot = a[ROWS - 1:ROWS, :] if same is None else _masked_row_sums(same, dta)
    r_t = (a - jnp.log2(dt)).T
    lo = lax.broadcasted_iota(jnp.int32, (ROWS, LANES), 1) < HEAD_DIM
    heads_per_group = N_HEADS_SSM // N_GROUPS
    for g in range(N_GROUPS):
        cg = cm[:, g * D_STATE:(g + 1) * D_STATE].astype(BF16)
        bg = bm[:, g * D_STATE:(g + 1) * D_STATE].astype(BF16)
        cb = lax.dot_general(cg, bg, (((1,), (1,)), ((), ())), preferred_element_type=F32)
        for j in range(heads_per_group // 2):
            c = g * (heads_per_group // 2) + j
            ms = []
            for e in range(2):
                h = 2 * c + e
                diff = jnp.broadcast_to(a[:, h:h + 1], (ROWS, ROWS)) - r_t[h:h + 1, :]
                ms.append(cb * jnp.exp2(jnp.where(causal, diff, -jnp.inf)))
            lhs = jnp.concatenate(ms, axis=1).astype(BF16)
            xc = xs[:, c * LANES:(c + 1) * LANES]
            zero = jnp.zeros_like(xc)
            rhs = jnp.concatenate([jnp.where(lo, xc, zero), jnp.where(lo, zero, xc)], axis=0).astype(BF16)
            ybuf[:, c * LANES:(c + 1) * LANES] = jnp.dot(lhs, rhs, preferred_element_type=F32)
            if after_pair is not None:
                after_pair()
    e_expa = _expand_heads(jnp.exp2(a))
    e_w = _expand_heads(jnp.exp2(tot - a) * dt)
    return e_expa, e_w


def _gated_group_norm(y, z, gain_ref, out_ref_store):
    yz = y * _silu(z)
    for g in range(N_GROUPS):
        blk = yz[:, g * GROUP_W:(g + 1) * GROUP_W]
        nrm = blk * lax.rsqrt(jnp.mean(blk * blk, axis=-1, keepdims=True) + EPS)
        out_ref_store(g, nrm * gain_ref[:, g * GROUP_W:(g + 1) * GROUP_W])


OUT_CHUNK = 256
OUT_DEPTH = 1024


def _ssd_out_body(z_ref, xs_ref, b_ref, c_ref, dt_ref, attn_ref, x_ref,
                  cwx, cwb, cwc, cbx, cbb, cbc, dtb_ref, alog_ref, dskip_ref, gain_ref, w_ref, post_ref,
                  y_ref, cvx_ref, cvb_ref, cvc_ref, st_ref,
                  tlx, tlb, tlc, h_t, ybuf, mix, o_acc, *, chunks, steps_per_seq, total):
    s = pl.program_id(0)
    c2 = jnp.minimum(s, total - 1) % steps_per_seq

    @pl.when(s == 0)
    def _():
        mix[0] = jnp.zeros((chunks * ROWS, D_SSM), BF16)

    @pl.when(c2 == 0)
    def _():
        tlx[...] = jnp.zeros_like(tlx)
        tlb[...] = jnp.zeros_like(tlb)
        tlc[...] = jnp.zeros_like(tlc)
        h_t[...] = jnp.zeros_like(h_t)

    mix[1] = mix[0]
    slabs = []
    for n in range(0, D_MODEL, OUT_CHUNK):
        for k in range(0, D_ATTN + D_SSM, OUT_DEPTH):
            def slab(n=n, k=k):
                lhs = attn_ref[0, :, k:k + OUT_DEPTH] if k < D_ATTN else mix[1, :, k - D_ATTN:k - D_ATTN + OUT_DEPTH]
                part = jnp.dot(lhs, w_ref[k:k + OUT_DEPTH, n:n + OUT_CHUNK], preferred_element_type=F32)
                if k == 0:
                    o_acc[:, n:n + OUT_CHUNK] = part
                else:
                    o_acc[:, n:n + OUT_CHUNK] += part
            slabs.append(slab)
    points_per_chunk = N_HEADS_SSM // 2 + N_GROUPS + 2
    interleave, flush = _spread(slabs, chunks * points_per_chunk)

    rl = lax.broadcasted_iota(jnp.int32, (ROWS, ROWS), 0)
    cs = lax.broadcasted_iota(jnp.int32, (ROWS, ROWS), 1)
    causal = cs <= rl
    live = {}

    def conv_phase(ch):
        rows = slice(ch * ROWS, (ch + 1) * ROWS)
        xs = _conv_rolled(xs_ref[0, rows, :], tlx, cwx, cbx)
        bm = _conv_rolled(b_ref[0, rows, :], tlb, cwb, cbb)
        cm = _conv_rolled(c_ref[0, rows, :], tlc, cwc, cbc)
        live[ch] = dict(xs=xs, bm=bm, cm=cm)
        interleave()

    def intra_phase(ch):
        v = live[ch]
        rows = slice(ch * ROWS, (ch + 1) * ROWS)
        v["e_expa"], v["e_w"] = _ssd_intra(v["xs"], v["bm"], v["cm"], dt_ref[0, rows, :], dtb_ref, alog_ref,
                                           causal, None, ybuf.at[ch], after_pair=interleave)

    def state_phase(ch):
        v = live[ch]
        yb = ybuf.at[ch]
        xw = (v["xs"] * v["e_w"]).astype(BF16)
        for g in range(N_GROUPS):
            gs = slice(g * GROUP_W, (g + 1) * GROUP_W)
            ds = slice(g * D_STATE, (g + 1) * D_STATE)
            h_prev = h_t[:, gs]
            y_inter = jnp.dot(v["cm"][:, ds].astype(BF16), h_prev.astype(BF16), preferred_element_type=F32)
            yb[:, gs] = yb[:, gs] + y_inter * v["e_expa"][:, gs]
            s_t = jnp.dot(v["bm"][:, ds].T.astype(BF16), xw[:, gs], preferred_element_type=F32)
            h_t[:, gs] = h_prev * v["e_expa"][ROWS - 1:ROWS, gs] + s_t
            interleave()

    def norm_phase(ch):
        v = live.pop(ch)
        rows = slice(ch * ROWS, (ch + 1) * ROWS)
        y = ybuf[ch] + dskip_ref[...] * v["xs"]

        def store(g, val):
            mix[0, rows, g * GROUP_W:(g + 1) * GROUP_W] = val.astype(BF16)
        _gated_group_norm(y, z_ref[0, rows, :].astype(F32), gain_ref, store)
        interleave()

    conv_phase(0)
    intra_phase(0)
    for ch in range(chunks):
        if ch + 1 < chunks:
            conv_phase(ch + 1)
        state_phase(ch)
        if ch + 1 < chunks:
            intra_phase(ch + 1)
        norm_phase(ch)

    flush()
    o = o_acc[...]
    o = o * lax.rsqrt(jnp.mean(o * o, axis=-1, keepdims=True) + EPS)
    y_ref[0] = x_ref[0] + o * post_ref[...]

    @pl.when((c2 == steps_per_seq - 1) & (s < total))
    def _():
        first = SUBLANES - (CONV_K - 1)
        cvx_ref[0] = tlx[first:SUBLANES, :]
        cvb_ref[0] = tlb[first:SUBLANES, :]
        cvc_ref[0] = tlc[first:SUBLANES, :]
        st_ref[0] = h_t[...].T


def _ssd_outproj(z3, xbc3, dt3, attn3, x3, p, chunks):
    b, s, _ = z3.shape
    tm = chunks * ROWS
    per_seq = s // tm
    total = b * per_seq
    cur = lambda i: jnp.minimum(i, total - 1)
    prv = lambda i: jnp.maximum(i - 1, 0)
    blk = lambda w, col: pl.BlockSpec((1, tm, w), lambda i: (cur(i) // per_seq, cur(i) % per_seq, col))
    lag = lambda w: pl.BlockSpec((1, tm, w), lambda i: (prv(i) // per_seq, prv(i) % per_seq, 0))
    par = lambda r, w: pl.BlockSpec((r, w), lambda i: (0, 0), pipeline_mode=pl.Buffered(1))
    per_b = lambda r, w: pl.BlockSpec((1, r, w), lambda i: (cur(i) // per_seq, 0, 0))
    tail = CONV_K - 1
    return pl.pallas_call(
        functools.partial(_ssd_out_body, chunks=chunks, steps_per_seq=per_seq, total=total),
        name="ssd_outproj",
        grid=(total + 1,),
        in_specs=[
            blk(D_SSM, 0), blk(D_SSM, 0),
            blk(BC_W, D_SSM // BC_W), blk(BC_W, D_SSM // BC_W + 1),
            blk(DT_W, 0),
            lag(D_ATTN), lag(D_MODEL),
            par(CONV_K, D_SSM), par(CONV_K, BC_W), par(CONV_K, BC_W),
            par(1, D_SSM), par(1, BC_W), par(1, BC_W),
            par(1, DT_W), par(1, DT_W), par(1, D_SSM), par(1, D_SSM),
            par(D_ATTN + D_SSM, D_MODEL), par(1, D_MODEL),
        ],
        out_specs=[
            lag(D_MODEL),
            per_b(tail, D_SSM), per_b(tail, BC_W), per_b(tail, BC_W),
            per_b(D_SSM, D_STATE),
        ],
        out_shape=[
            jax.ShapeDtypeStruct((b, s, D_MODEL), F32),
            jax.ShapeDtypeStruct((b, tail, D_SSM), F32),
            jax.ShapeDtypeStruct((b, tail, BC_W), F32),
            jax.ShapeDtypeStruct((b, tail, BC_W), F32),
            jax.ShapeDtypeStruct((b, D_SSM, D_STATE), F32),
        ],
        scratch_shapes=[
            pltpu.VMEM((SUBLANES, D_SSM), F32),
            pltpu.VMEM((SUBLANES, BC_W), F32),
            pltpu.VMEM((SUBLANES, BC_W), F32),
            pltpu.VMEM((D_STATE, D_SSM), F32),
            pltpu.VMEM((chunks, ROWS, D_SSM), F32),
            pltpu.VMEM((2, tm, D_SSM), BF16),
            pltpu.VMEM((tm, D_MODEL), F32),
        ],
        compiler_params=_cparams(("arbitrary",)),
    )(z3, xbc3, xbc3, xbc3, dt3, attn3, x3,
      p["cwx"], p["cwb"], p["cwc"], p["cbx"], p["cbb"], p["cbc"],
      p["dtb"], p["alog"], p["dskip"], p["ssm_gain"], p["w_out"], p["norm_post"])


def _ssd_step_body(z_ref, xs_ref, b_ref, c_ref, dt_ref, sx_ref, sb_ref, sc_ref, h0_ref,
                   cwx, cwb, cwc, cbx, cbb, cbc, dtb_ref, alog_ref, dskip_ref, gain_ref,
                   out_ref, cvx_ref, cvb_ref, cvc_ref, st_ref,
                   xpx, xpb, xpc, ybuf, cm_scr, ea_scr, ea_t_scr, xw_t_scr, *, nb, t):
    real = nb * t
    tail = CONV_K - 1
    lo_row = SUBLANES - tail

    def conv(xp, new_ref, st_in_ref, st_out_ref, w_ref, b_ref):
        xp[:, lo_row:SUBLANES, :] = st_in_ref[...]
        xp[:, SUBLANES:SUBLANES + t, :] = new_ref[...]
        st_out_ref[...] = xp[:, SUBLANES + t - tail:SUBLANES + t, :]
        w = new_ref.shape[-1]
        act = _conv_taps(lambda k: xp[:, lo_row + k:lo_row + k + t, :].reshape(real, w), w_ref, b_ref)
        return jnp.concatenate([act, jnp.zeros((ROWS - real, w), F32)], axis=0)

    xs = conv(xpx, xs_ref, sx_ref, cvx_ref, cwx, cbx)
    bm = conv(xpb, b_ref, sb_ref, cvb_ref, cwb, cbb)
    cm = conv(xpc, c_ref, sc_ref, cvc_ref, cwc, cbc)
    dt_raw = jnp.concatenate([dt_ref[...].reshape(real, DT_W), jnp.zeros((ROWS - real, DT_W), F32)], axis=0)

    rl = lax.broadcasted_iota(jnp.int32, (ROWS, ROWS), 0)
    cs = lax.broadcasted_iota(jnp.int32, (ROWS, ROWS), 1)
    same = (rl // t) == (cs // t)
    causal = same & (cs <= rl)
    e_expa, e_w = _ssd_intra(xs, bm, cm, dt_raw, dtb_ref, alog_ref, causal, same, ybuf)

    cm_scr[...] = cm
    ea_scr[...] = e_expa
    ea_t_scr[...] = e_expa.T
    xw_t_scr[...] = (xs * e_w).T.astype(BF16)
    row_id = lax.broadcasted_iota(jnp.int32, (ROWS, D_STATE), 0)
    pair = 2 * SUBLANES
    for b in range(nb):
        in_seq = (row_id >= b * t) & (row_id < (b + 1) * t)
        r0 = b * t
        last_col = r0 + t - 1
        for g in range(N_GROUPS):
            gs = slice(g * GROUP_W, (g + 1) * GROUP_W)
            ds = slice(g * D_STATE, (g + 1) * D_STATE)
            h0 = h0_ref[b, gs, :]
            c_rows = cm_scr[r0:r0 + pair, ds].astype(BF16)
            y_inter = lax.dot_general(c_rows, h0.astype(BF16), (((1,), (1,)), ((), ())),
                                      preferred_element_type=F32)
            ybuf[r0:r0 + t, gs] = ybuf[r0:r0 + t, gs] + y_inter[0:t] * ea_scr[r0:r0 + t, gs]
            b_rows = jnp.where(in_seq, bm[:, ds], 0.0).astype(BF16)
            s_new = jnp.dot(xw_t_scr[gs, :], b_rows, preferred_element_type=F32)
            decay = jnp.broadcast_to(ea_t_scr[gs, last_col:last_col + 1], (GROUP_W, D_STATE))
            st_ref[b, gs, :] = h0 * decay + s_new

    y = ybuf[0:real, :] + dskip_ref[...] * xs[0:real, :]
    z = z_ref[...].reshape(real, D_SSM)

    def store(g, v):
        out_ref[:, :, g * GROUP_W:(g + 1) * GROUP_W] = v.reshape(nb, t, GROUP_W)
    _gated_group_norm(y, z, gain_ref, store)


def _ssd_step(z3, xbc3, dt3, conv_state, h0, p, nb):
    n, t, _ = z3.shape
    tail = CONV_K - 1
    blk = lambda w, col: pl.BlockSpec((nb, t, w), lambda i: (i, 0, col))
    cst = lambda w, col: pl.BlockSpec((nb, tail, w), lambda i: (i, 0, col))
    par = lambda r, w: pl.BlockSpec((r, w), lambda i: (0, 0))
    state = pl.BlockSpec((nb, D_SSM, D_STATE), lambda i: (i, 0, 0))
    return pl.pallas_call(
        functools.partial(_ssd_step_body, nb=nb, t=t),
        name="ssd_step",
        grid=(n // nb,),
        in_specs=[
            blk(D_SSM, 0), blk(D_SSM, 0),
            blk(BC_W, D_SSM // BC_W), blk(BC_W, D_SSM // BC_W + 1),
            blk(DT_W, 0),
            cst(D_SSM, 0), cst(BC_W, D_SSM // BC_W), cst(BC_W, D_SSM // BC_W + 1),
            state,
            par(CONV_K, D_SSM), par(CONV_K, BC_W), par(CONV_K, BC_W),
            par(1, D_SSM), par(1, BC_W), par(1, BC_W),
            par(1, DT_W), par(1, DT_W), par(1, D_SSM), par(1, D_SSM),
        ],
        out_specs=[
            blk(D_SSM, 0),
            cst(D_SSM, 0), cst(BC_W, 0), cst(BC_W, 0),
            state,
        ],
        out_shape=[
            jax.ShapeDtypeStruct((n, t, D_SSM), F32),
            jax.ShapeDtypeStruct((n, tail, D_SSM), F32),
            jax.ShapeDtypeStruct((n, tail, BC_W), F32),
            jax.ShapeDtypeStruct((n, tail, BC_W), F32),
            jax.ShapeDtypeStruct((n, D_SSM, D_STATE), F32),
        ],
        scratch_shapes=[
            pltpu.VMEM((nb, SUBLANES + t, D_SSM), F32),
            pltpu.VMEM((nb, SUBLANES + t, BC_W), F32),
            pltpu.VMEM((nb, SUBLANES + t, BC_W), F32),
            pltpu.VMEM((ROWS, D_SSM), F32),
            pltpu.VMEM((ROWS, BC_W), F32),
            pltpu.VMEM((ROWS, D_SSM), F32),
            pltpu.VMEM((D_SSM, ROWS), F32),
            pltpu.VMEM((D_SSM, ROWS), BF16),
        ],
        compiler_params=_cparams(("parallel",)),
    )(z3, xbc3, xbc3, xbc3, dt3, conv_state, conv_state, conv_state, h0,
      p["cwx"], p["cwb"], p["cwc"], p["cbx"], p["cbb"], p["cbc"],
      p["dtb"], p["alog"], p["dskip"], p["ssm_gain"])


def _layer_params(norm_pre, w_in, attn_sink, attn_norm, conv_w, conv_b, dt_bias, a_log, d_skip,
                  ssm_norm, w_out, norm_post):
    w_bf16 = w_in.astype(BF16)
    w_tail = jnp.pad(w_bf16[:, 2 * WA_COLS:], ((0, 0), (0, DT_W - N_HEADS_SSM)))
    pad_h = lambda v: jnp.pad(v, (0, DT_W - N_HEADS_SSM)).reshape(1, DT_W)
    return dict(
        norm_pre=norm_pre.reshape(1, D_MODEL), w_in=w_bf16, w_tail=w_tail,
        sink=attn_sink, attn_gain=attn_norm.reshape(1, D_ATTN),
        cwx=conv_w[:, :D_SSM], cwb=conv_w[:, D_SSM:D_SSM + BC_W], cwc=conv_w[:, D_SSM + BC_W:],
        cbx=conv_b[:D_SSM].reshape(1, -1), cbb=conv_b[D_SSM:D_SSM + BC_W].reshape(1, -1),
        cbc=conv_b[D_SSM + BC_W:].reshape(1, -1),
        dtb=pad_h(dt_bias), alog=pad_h(a_log),
        dskip=jnp.broadcast_to(d_skip[:, None], (N_HEADS_SSM, HEAD_DIM)).reshape(1, D_SSM),
        ssm_gain=ssm_norm.reshape(1, D_SSM),
        w_out=w_out.astype(BF16),
        norm_post=norm_post.reshape(1, D_MODEL),
    )


IN_TM, OUT_TM = 512, 512
ATTN_BLOCKS = 2
SSD_CHUNKS = 2
STEP_NB_ATTN, STEP_NB_SSD = 8, 8


def _layer(x, caches, p):
    n, t, _ = x.shape
    x2d = x.reshape(n * t, D_MODEL)
    act_dtype = BF16 if caches is None else F32
    h, qg, kv = _inproj_attn(x2d, p["norm_pre"], p["w_in"], IN_TM, act_dtype)
    qg3, kv3 = qg.reshape(n, t, 2 * D_ATTN), kv.reshape(n, t, 2 * D_KV)
    if caches is None:
        attn, z3, xbc3, dt3 = _attn_prompt(qg3, kv3, h.reshape(n, t, D_MODEL), p["w_in"], p["w_tail"],
                                           p["sink"], p["attn_gain"], ATTN_BLOCKS)
        new_k = kv3[:, t - WINDOW:, :D_KV]
        new_v = kv3[:, t - WINDOW:, D_KV:]
        y, cvx, cvb, cvc, h_new = _ssd_outproj(z3, xbc3, dt3, attn, x, p, SSD_CHUNKS)
    else:
        z, xbc, dt = _inproj_ssm(h, p["w_in"], p["w_tail"], IN_TM, act_dtype)
        z3, xbc3, dt3 = z.reshape(n, t, D_SSM), xbc.reshape(n, t, CONV_DIM), dt.reshape(n, t, DT_W)
        kbuf, vbuf, conv_buf, h0 = caches
        attn, new_k, new_v = _attn_step(qg3, kv3, kbuf.reshape(n, WINDOW, D_KV), vbuf.reshape(n, WINDOW, D_KV),
                                        p["sink"], p["attn_gain"], STEP_NB_ATTN)
        ssm, cvx, cvb, cvc, h_new = _ssd_step(z3, xbc3, dt3, conv_buf, h0.reshape(n, D_SSM, D_STATE), p,
                                              STEP_NB_SSD)
        y = _outproj(attn.reshape(n * t, D_ATTN), ssm.reshape(n * t, D_SSM), x2d,
                     p["w_out"], p["norm_post"], OUT_TM)
    return (y.reshape(n, t, D_MODEL),
            new_k.reshape(n, WINDOW, N_KV_HEADS, HEAD_DIM), new_v.reshape(n, WINDOW, N_KV_HEADS, HEAD_DIM),
            jnp.concatenate([cvx, cvb, cvc], axis=-1),
            h_new.reshape(n, N_HEADS_SSM, HEAD_DIM, D_STATE))


def kernel(x_prompt, x_sample, cache_k, cache_v, state_conv, state_ssm, norm_pre, w_in, attn_sink,
           attn_norm, conv_w, conv_b, dt_bias, a_log, d_skip, ssm_norm, w_out, norm_post):
    depth = w_in.shape[0]
    yp, ys = x_prompt, x_sample
    outs_p, outs_s = [], []
    for l in range(depth):
        p = _layer_params(norm_pre[l], w_in[l], attn_sink[l], attn_norm[l], conv_w[l], conv_b[l],
                          dt_bias[l], a_log[l], d_skip[l], ssm_norm[l], w_out[l], norm_post[l])
        yp, *rest_p = _layer(yp, None, p)
        ys, *rest_s = _layer(ys, (cache_k[l], cache_v[l], state_conv[l], state_ssm[l]), p)
        outs_p.append(rest_p)
        outs_s.append(rest_s)
    stack = lambda outs, i: jnp.stack([o[i] for o in outs])
    return (yp, ys,
            stack(outs_p, 0), stack(outs_p, 1), stack(outs_p, 2), stack(outs_p, 3),
            stack(outs_s, 0), stack(outs_s, 1), stack(outs_s, 2), stack(outs_s, 3))
```

```python
import functools

import jax
import jax.numpy as jnp
from jax import lax
from jax.experimental import pallas as pl
from jax.experimental.pallas import tpu as pltpu

F32 = jnp.float32
BF16 = jnp.bfloat16

D_MODEL = 2048
HEAD_DIM = 64
N_KV_HEADS = 4
D_ATTN = 2048
D_KV = 256
WINDOW = 128
D_SSM = 2048
N_HEADS_SSM = 32
N_GROUPS = 4
D_STATE = 128
CONV_K = 4
GROUP_W = D_SSM // N_GROUPS
BC_W = N_GROUPS * D_STATE
CONV_DIM = D_SSM + 2 * BC_W
EPS = 1e-6
LOG2E = 1.4426950408889634
LANES = 128
SUBLANES = 8

DT_W = LANES
WA_COLS = 2 * D_ATTN + 2 * D_KV
WS_COLS = D_SSM + CONV_DIM + DT_W
WT_COLS = WS_COLS - WA_COLS

VMEM_LIMIT = 56 * 1024 * 1024


def _silu(x):
    h = 0.5 * x
    return h + h * jnp.tanh(h)


def _cparams(sem):
    return pltpu.CompilerParams(dimension_semantics=sem, vmem_limit_bytes=VMEM_LIMIT)


def _spread(slabs, points):
    ticks, issued = [0], [0]

    def run_until(stop):
        for run in slabs[issued[0]:stop]:
            run()
        issued[0] = max(issued[0], stop)

    def tick():
        ticks[0] += 1
        run_until(min(ticks[0], points) * len(slabs) // points)

    return tick, lambda: run_until(len(slabs))


def _inproj_attn_body(x_ref, xn_ref, g_ref, w_ref, h_ref, qg_ref, kv_ref, hbuf, *, row_chunk):
    i = pl.program_id(0)
    tm = x_ref.shape[0]
    slot = i % 2

    def norm_rows(src_ref, dst_slot, r):
        rows = slice(r, r + row_chunk)
        x = src_ref[rows, :]
        y = x * lax.rsqrt(jnp.mean(x * x, axis=-1, keepdims=True) + EPS)
        hbuf[dst_slot, rows, :] = (y * g_ref[...]).astype(BF16)

    @pl.when(i == 0)
    def _():
        for r in range(0, tm, row_chunk):
            norm_rows(x_ref, 0, r)

    h = hbuf[slot]
    h_ref[...] = h
    ahead = [functools.partial(norm_rows, xn_ref, 1 - slot, r) for r in range(0, tm, row_chunk)]
    tick, flush = _spread(ahead, WA_COLS // PROJ_WIDE)
    for lo in range(0, WA_COLS, PROJ_WIDE):
        r = jnp.dot(h, w_ref[:, lo:lo + PROJ_WIDE], preferred_element_type=F32)
        if lo < D_ATTN:
            qg_ref[:, lo:lo + PROJ_WIDE] = r.astype(qg_ref.dtype)
        elif lo < D_ATTN + 2 * D_KV:
            kv_ref[...] = r
        else:
            qg_ref[:, lo - 2 * D_KV:lo - 2 * D_KV + PROJ_WIDE] = r.astype(qg_ref.dtype)
        tick()
    flush()


def _inproj_ssm_body(h_ref, w_ref, wt_ref, z_ref, xbc_ref, dt_ref):
    for run in _ssm_projection_steps(h_ref[...], w_ref, wt_ref, z_ref, xbc_ref, dt_ref):
        run()


PROJ_CHUNK = 256
PROJ_WIDE = 512


def _ssm_projection_steps(h, w_ref, wt_ref, z_ref, xbc_ref, dt_ref):
    lead = z_ref.shape[:-1]
    rows = (slice(None),) * len(lead)

    def slab(out_ref, out_lo, w_lo, width):
        def run():
            w = w_ref[:, w_lo:w_lo + width] if w_lo < WA_COLS else wt_ref[:, w_lo - WA_COLS:w_lo - WA_COLS + width]
            r = jnp.dot(h, w, preferred_element_type=F32)
            out_ref[rows + (slice(out_lo, out_lo + width),)] = r.astype(out_ref.dtype).reshape(*lead, width)
        return run

    steps = [slab(z_ref, c, c, PROJ_CHUNK) for c in range(0, D_SSM, PROJ_CHUNK)]
    steps += [slab(xbc_ref, c, D_SSM + c, PROJ_CHUNK) for c in range(0, CONV_DIM, PROJ_CHUNK)]
    steps.append(slab(dt_ref, 0, D_SSM + CONV_DIM, DT_W))
    return steps


def _inproj_attn(x2d, gain, w_in, tm, act_dtype):
    m = x2d.shape[0]
    tm = min(tm, m)
    steps = m // tm
    row = lambda w: pl.BlockSpec((tm, w), lambda i: (i, 0))
    next_row = pl.BlockSpec((tm, D_MODEL), lambda i: (jnp.minimum(i + 1, steps - 1), 0))
    resident = lambda r, w: pl.BlockSpec((r, w), lambda i: (0, 0), pipeline_mode=pl.Buffered(1))
    return pl.pallas_call(
        functools.partial(_inproj_attn_body, row_chunk=min(128, tm)),
        name="inproj_attn",
        grid=(steps,),
        in_specs=[row(D_MODEL), next_row, resident(1, D_MODEL), resident(D_MODEL, WA_COLS)],
        out_specs=[row(D_MODEL), row(2 * D_ATTN), row(2 * D_KV)],
        out_shape=[jax.ShapeDtypeStruct((m, D_MODEL), BF16),
                   jax.ShapeDtypeStruct((m, 2 * D_ATTN), act_dtype),
                   jax.ShapeDtypeStruct((m, 2 * D_KV), F32)],
        scratch_shapes=[pltpu.VMEM((2, tm, D_MODEL), BF16)],
        compiler_params=_cparams(("arbitrary",)),
    )(x2d, x2d, gain, w_in)


def _inproj_ssm(h, w_in, w_tail, tm, act_dtype):
    m = h.shape[0]
    tm = min(tm, m)
    row = lambda w: pl.BlockSpec((tm, w), lambda i: (i, 0))
    once = dict(pipeline_mode=pl.Buffered(1))
    return pl.pallas_call(
        _inproj_ssm_body,
        name="inproj_ssm",
        grid=(m // tm,),
        in_specs=[row(D_MODEL),
                  pl.BlockSpec((D_MODEL, WA_COLS), lambda i: (0, 1), **once),
                  pl.BlockSpec((D_MODEL, WT_COLS), lambda i: (0, 0), **once)],
        out_specs=[row(D_SSM), row(CONV_DIM), row(DT_W)],
        out_shape=[jax.ShapeDtypeStruct((m, D_SSM), act_dtype),
                   jax.ShapeDtypeStruct((m, CONV_DIM), F32),
                   jax.ShapeDtypeStruct((m, DT_W), F32)],
        compiler_params=_cparams(("parallel",)),
    )(h, w_in, w_tail)


def _outproj_body(a_ref, s_ref, x_ref, wa_ref, ws_ref, g_ref, y_ref):
    o = jnp.dot(a_ref[...].astype(BF16), pltpu.bitcast(wa_ref[...], BF16), preferred_element_type=F32)
    o = o + jnp.dot(s_ref[...].astype(BF16), pltpu.bitcast(ws_ref[...], BF16), preferred_element_type=F32)
    o = o * lax.rsqrt(jnp.mean(o * o, axis=-1, keepdims=True) + EPS)
    y_ref[...] = x_ref[...] + o * g_ref[...]


def _outproj(attn, ssm, x2d, w_out, gain, tm):
    m = x2d.shape[0]
    tm = min(tm, m)
    row = lambda i: (i, 0)
    once = dict(pipeline_mode=pl.Buffered(1))
    return pl.pallas_call(
        _outproj_body,
        name="outproj",
        grid=(m // tm,),
        in_specs=[
            pl.BlockSpec((tm, D_ATTN), row),
            pl.BlockSpec((tm, D_SSM), row),
            pl.BlockSpec((tm, D_MODEL), row),
            pl.BlockSpec((D_ATTN // 2, D_MODEL), lambda i: (0, 0), **once),
            pl.BlockSpec((D_SSM // 2, D_MODEL), lambda i: (1, 0), **once),
            pl.BlockSpec((1, D_MODEL), lambda i: (0, 0), **once),
        ],
        out_specs=pl.BlockSpec((tm, D_MODEL), row),
        out_shape=jax.ShapeDtypeStruct((m, D_MODEL), F32),
        compiler_params=_cparams(("parallel",)),
    )(attn, ssm, x2d, w_out, w_out, gain)


def _head_pair_blockdiag(slab):
    lo = lax.broadcasted_iota(jnp.int32, slab.shape, 1) < HEAD_DIM
    rolled = pltpu.roll(slab, HEAD_DIM, axis=1)
    zero = jnp.zeros_like(slab)
    bd_a = jnp.concatenate([jnp.where(lo, slab, zero), jnp.where(lo, zero, rolled)], axis=0)
    bd_b = jnp.concatenate([jnp.where(lo, rolled, zero), jnp.where(lo, zero, slab)], axis=0)
    return bd_a.astype(BF16), bd_b.astype(BF16)


ATTEND_TICKS = N_KV_HEADS * ((D_ATTN // N_KV_HEADS) // LANES + 2)


def _attend(qs, kks, vvs, masks, sink_ref, o_scrs, tick=lambda: None):
    tq, tk = masks[0].shape
    seqs = range(len(qs))
    kbd, vbd = [], []
    for b in seqs:
        kk = kks[b] * (HEAD_DIM ** -0.5 * LOG2E)
        kb, vb = [], []
        for c in range(D_KV // LANES):
            kb.extend(_head_pair_blockdiag(kk[:, c * LANES:(c + 1) * LANES]))
            vb.extend(_head_pair_blockdiag(vvs[b][:, c * LANES:(c + 1) * LANES]))
        kbd.append(kb)
        vbd.append(vb)
    lo = lax.broadcasted_iota(jnp.int32, (tq, LANES), 1) < HEAD_DIM
    pairs_per_group = (D_ATTN // N_KV_HEADS) // LANES
    for g in range(N_KV_HEADS):
        cols = [g * pairs_per_group + j for j in range(pairs_per_group)]
        s = []
        for b in seqs:
            q_stack = jnp.concatenate([qs[b][:, c * LANES:(c + 1) * LANES] for c in cols], axis=0).astype(BF16)
            s.append(lax.dot_general(q_stack, kbd[b][g], (((1,), (1,)), ((), ())), preferred_element_type=F32))
        tick()
        p_rows = [[] for _ in seqs]
        inv_rows = [[] for _ in seqs]
        for j, col in enumerate(cols):
            ps = [[] for _ in seqs]
            invs = [[] for _ in seqs]
            for e in range(2):
                snk = sink_ref[2 * col + e] * LOG2E
                for b in seqs:
                    se = jnp.where(masks[b], s[b][j * tq:(j + 1) * tq, e * tk:(e + 1) * tk], -jnp.inf)
                    mx = jnp.maximum(jnp.max(se, axis=-1, keepdims=True), snk)
                    p = jnp.exp2(se - mx)
                    den = jnp.sum(p, axis=-1, keepdims=True) + jnp.exp2(snk - mx)
                    ps[b].append(p)
                    invs[b].append(1.0 / den)
            for b in seqs:
                p_rows[b].append(jnp.concatenate(ps[b], axis=1).astype(BF16))
                inv_rows[b].append(jnp.where(lo, invs[b][0], invs[b][1]))
            tick()
        o = [jnp.dot(jnp.concatenate(p_rows[b], axis=0), vbd[b][g], preferred_element_type=F32) for b in seqs]
        for b in seqs:
            for j, col in enumerate(cols):
                o_scrs[b][:, col * LANES:(col + 1) * LANES] = o[b][j * tq:(j + 1) * tq, :] * inv_rows[b][j]
        tick()


def _gate_norm(o, gate, gain):
    y = o * _silu(gate)
    y = y * lax.rsqrt(jnp.mean(y * y, axis=-1, keepdims=True) + EPS)
    return y * gain


def _attn_prompt_body(sink_ref, q_ref, g_ref, kp_ref, kc_ref, vp_ref, vc_ref, gain_ref, h_ref, w_ref, wt_ref,
                      out_ref, z_ref, xbc_ref, dt_ref, o_scr, *, blocks):
    i = pl.program_id(1)
    ql = lax.broadcasted_iota(jnp.int32, (WINDOW, 2 * WINDOW), 0)
    ks = lax.broadcasted_iota(jnp.int32, (WINDOW, 2 * WINDOW), 1)
    rel = ql + WINDOW - ks
    band = (rel >= 0) & (rel < WINDOW)
    steps = _ssm_projection_steps(h_ref[0], w_ref, wt_ref, z_ref, xbc_ref, dt_ref)
    tick, flush = _spread(steps, ATTEND_TICKS + blocks)

    qs, kks, vvs, masks = [], [], [], []
    for sub in range(blocks):
        rows = slice(sub * WINDOW, (sub + 1) * WINDOW)
        if sub == 0:
            k_prev, v_prev = kp_ref[0], vp_ref[0]
            first_key = jnp.where(i > 0, 0, WINDOW)
            masks.append(band & (ks >= first_key))
        else:
            before = slice((sub - 1) * WINDOW, sub * WINDOW)
            k_prev, v_prev = kc_ref[0, before, :], vc_ref[0, before, :]
            masks.append(band)
        qs.append(q_ref[0, rows, :])
        kks.append(jnp.concatenate([k_prev, kc_ref[0, rows, :]], axis=0))
        vvs.append(jnp.concatenate([v_prev, vc_ref[0, rows, :]], axis=0))
    _attend(qs, kks, vvs, masks, sink_ref, [o_scr.at[sub] for sub in range(blocks)], tick)
    for sub in range(blocks):
        rows = slice(sub * WINDOW, (sub + 1) * WINDOW)
        y = _gate_norm(o_scr[sub], g_ref[0, rows, :].astype(F32), gain_ref[...]).astype(BF16)
        out_ref[0, sub * (WINDOW // 2):(sub + 1) * (WINDOW // 2), :] = pltpu.bitcast(y, jnp.uint32)
        tick()
    flush()


def _attn_prompt(qg3, kv3, h3, w_in, w_tail, sink, gain, blocks):
    b, s, _ = qg3.shape
    tm = blocks * WINDOW
    blk = lambda w, col: pl.BlockSpec((1, tm, w), lambda bi, i: (bi, i, col))
    prev = lambda w, col: pl.BlockSpec((1, WINDOW, w), lambda bi, i: (bi, jnp.maximum(i * blocks - 1, 0), col))
    once = dict(pipeline_mode=pl.Buffered(1))
    return pl.pallas_call(
        functools.partial(_attn_prompt_body, blocks=blocks),
        name="attn_prompt",
        grid=(b, s // tm),
        in_specs=[
            pl.BlockSpec(memory_space=pltpu.SMEM),
            blk(D_ATTN, 0), blk(D_ATTN, 1),
            prev(D_KV, 0), blk(D_KV, 0),
            prev(D_KV, 1), blk(D_KV, 1),
            pl.BlockSpec((1, D_ATTN), lambda bi, i: (0, 0), **once),
            blk(D_MODEL, 0),
            pl.BlockSpec((D_MODEL, WA_COLS), lambda bi, i: (0, 1), **once),
            pl.BlockSpec((D_MODEL, WT_COLS), lambda bi, i: (0, 0), **once),
        ],
        out_specs=[pl.BlockSpec((1, tm // 2, D_ATTN), lambda bi, i: (bi, i, 0)),
                   blk(D_SSM, 0), blk(CONV_DIM, 0), blk(DT_W, 0)],
        out_shape=[jax.ShapeDtypeStruct((b, s // 2, D_ATTN), jnp.uint32),
                   jax.ShapeDtypeStruct((b, s, D_SSM), BF16),
                   jax.ShapeDtypeStruct((b, s, CONV_DIM), F32),
                   jax.ShapeDtypeStruct((b, s, DT_W), F32)],
        scratch_shapes=[pltpu.VMEM((blocks, WINDOW, D_ATTN), F32)],
        compiler_params=_cparams(("parallel", "arbitrary")),
    )(sink, qg3, qg3, kv3, kv3, kv3, kv3, gain, h3, w_in, w_tail)


Q_PAD = 16


def _attn_step_body(sink_ref, q_ref, g_ref, kn_ref, vn_ref, ck_ref, cv_ref, gain_ref,
                    out_ref, ko_ref, vo_ref, o_scr, *, nb, t):
    tk = 2 * WINDOW
    ql = lax.broadcasted_iota(jnp.int32, (Q_PAD, tk), 0)
    ks = lax.broadcasted_iota(jnp.int32, (Q_PAD, tk), 1)
    rel = ql + WINDOW - ks
    mask = (rel >= 0) & (rel < WINDOW) & (ql < t)
    pad_k = jnp.zeros((tk - WINDOW - t, D_KV), F32)
    pad_q = jnp.zeros((Q_PAD - t, D_ATTN), F32)

    kks = [jnp.concatenate([ck_ref[b], kn_ref[b], pad_k], axis=0) for b in range(nb)]
    vvs = [jnp.concatenate([cv_ref[b], vn_ref[b], pad_k], axis=0) for b in range(nb)]
    qs = [jnp.concatenate([q_ref[b], pad_q], axis=0) for b in range(nb)]
    for b in range(nb):
        ko_ref[b] = kks[b][t:t + WINDOW, :]
        vo_ref[b] = vvs[b][t:t + WINDOW, :]
    _attend(qs, kks, vvs, [mask] * nb, sink_ref, [o_scr.at[b] for b in range(nb)])
    for b in range(nb):
        out_ref[b] = _gate_norm(o_scr[b, 0:t, :], g_ref[b], gain_ref[...])


def _attn_step(qg3, kv3, cache_k, cache_v, sink, gain, nb):
    n, t, _ = qg3.shape
    blk = lambda w, col: pl.BlockSpec((nb, t, w), lambda i: (i, 0, col))
    cache = pl.BlockSpec((nb, WINDOW, D_KV), lambda i: (i, 0, 0))
    return pl.pallas_call(
        functools.partial(_attn_step_body, nb=nb, t=t),
        name="attn_step",
        grid=(n // nb,),
        in_specs=[
            pl.BlockSpec(memory_space=pltpu.SMEM),
            blk(D_ATTN, 0), blk(D_ATTN, 1),
            blk(D_KV, 0), blk(D_KV, 1),
            cache,
            cache,
            pl.BlockSpec((1, D_ATTN), lambda i: (0, 0)),
        ],
        out_specs=[pl.BlockSpec((nb, t, D_ATTN), lambda i: (i, 0, 0)), cache, cache],
        out_shape=[
            jax.ShapeDtypeStruct((n, t, D_ATTN), F32),
            jax.ShapeDtypeStruct((n, WINDOW, D_KV), F32),
            jax.ShapeDtypeStruct((n, WINDOW, D_KV), F32),
        ],
        scratch_shapes=[pltpu.VMEM((nb, Q_PAD, D_ATTN), F32)],
        compiler_params=_cparams(("parallel",)),
    )(sink, qg3, qg3, kv3, kv3, cache_k, cache_v, gain)


ROWS = 128


def _expand_heads(v):
    r = v.shape[0]
    lo = lax.broadcasted_iota(jnp.int32, (r, LANES), 1) < HEAD_DIM
    cols = []
    for c in range(D_SSM // LANES):
        va = jnp.broadcast_to(v[:, 2 * c:2 * c + 1], (r, LANES))
        vb = jnp.broadcast_to(v[:, 2 * c + 1:2 * c + 2], (r, LANES))
        cols.append(jnp.where(lo, va, vb))
    return jnp.concatenate(cols, axis=1)


def _conv_taps(load_rows, w_ref, b_ref):
    acc = b_ref[...] + w_ref[CONV_K - 1:CONV_K, :] * load_rows(CONV_K - 1)
    for k in range(CONV_K - 1):
        acc = acc + w_ref[k:k + 1, :] * load_rows(k)
    return _silu(acc)


def _conv_rolled(x, tail_scr, w_ref, b_ref):
    taps = CONV_K - 1
    width = x.shape[1]
    tiles = ROWS // SUBLANES
    with_prev = jnp.concatenate([tail_scr[...], x], axis=0).reshape(tiles + 1, SUBLANES, width)
    row = lax.broadcasted_iota(jnp.int32, (tiles, SUBLANES, width), 1)
    acc = b_ref[...] + w_ref[taps:taps + 1, :] * x
    for j in range(1, CONV_K):
        rot = pltpu.roll(with_prev, j, axis=1)
        shifted = jnp.where(row < j, rot[:tiles], rot[1:]).reshape(ROWS, width)
        acc = acc + w_ref[taps - j:taps - j + 1, :] * shifted
    tail_scr[...] = x[ROWS - SUBLANES:, :]
    return _silu(acc)


def _masked_row_sums(mask, x):
    m = jnp.where(mask, 1.0, 0.0).astype(BF16)
    hi = x.astype(BF16)
    rest = x - hi.astype(F32)
    mid = rest.astype(BF16)
    lo = (rest - mid.astype(F32)).astype(BF16)
    dot = lambda piece: jnp.dot(m, piece, preferred_element_type=F32)
    return dot(hi) + dot(mid) + dot(lo)


def _ssd_intra(xs, bm, cm, dt_raw, dtb_ref, alog_ref, causal, same, ybuf, after_pair=None):
    dt = jax.nn.softplus(dt_raw + dtb_ref[...])
    a_neg = -jnp.exp(alog_ref[...])
    dta = dt * (a_neg * LOG2E)
    a = _masked_row_sums(causal, dta)
    tot = a[ROWS - 1:ROWS, :] if same is None else _masked_row_sums(same, dta)
    r_t = (a - jnp.log2(dt)).T
    lo = lax.broadcasted_iota(jnp.int32, (ROWS, LANES), 1) < HEAD_DIM
    heads_per_group = N_HEADS_SSM // N_GROUPS
    for g in range(N_GROUPS):
        cg = cm[:, g * D_STATE:(g + 1) * D_STATE].astype(BF16)
        bg = bm[:, g * D_STATE:(g + 1) * D_STATE].astype(BF16)
        cb = lax.dot_general(cg, bg, (((1,), (1,)), ((), ())), preferred_element_type=F32)
        for j in range(heads_per_group // 2):
            c = g * (heads_per_group // 2) + j
            ms = []
            for e in range(2):
                h = 2 * c + e
                diff = jnp.broadcast_to(a[:, h:h + 1], (ROWS, ROWS)) - r_t[h:h + 1, :]
                ms.append(cb * jnp.exp2(jnp.where(causal, diff, -jnp.inf)))
            lhs = jnp.concatenate(ms, axis=1).astype(BF16)
            xc = xs[:, c * LANES:(c + 1) * LANES]
            zero = jnp.zeros_like(xc)
            rhs = jnp.concatenate([jnp.where(lo, xc, zero), jnp.where(lo, zero, xc)], axis=0).astype(BF16)
            ybuf[:, c * LANES:(c + 1) * LANES] = jnp.dot(lhs, rhs, preferred_element_type=F32)
            if after_pair is not None:
                after_pair()
    e_expa = _expand_heads(jnp.exp2(a))
    e_w = _expand_heads(jnp.exp2(tot - a) * dt)
    return e_expa, e_w


def _gated_group_norm(y, z, gain_ref, out_ref_store):
    yz = y * _silu(z)
    for g in range(N_GROUPS):
        blk = yz[:, g * GROUP_W:(g + 1) * GROUP_W]
        nrm = blk * lax.rsqrt(jnp.mean(blk * blk, axis=-1, keepdims=True) + EPS)
        out_ref_store(g, nrm * gain_ref[:, g * GROUP_W:(g + 1) * GROUP_W])


OUT_CHUNK = 256
OUT_DEPTH = 1024


def _ssd_out_body(z_ref, xs_ref, b_ref, c_ref, dt_ref, attn_ref, x_ref,
                  cwx, cwb, cwc, cbx, cbb, cbc, dtb_ref, alog_ref, dskip_ref, gain_ref, w_ref, post_ref,
                  y_ref, cvx_ref, cvb_ref, cvc_ref, st_ref,
                  tlx, tlb, tlc, h_t, ybuf, mix, o_acc, *, chunks, steps_per_seq, total):
    s = pl.program_id(0)
    c2 = jnp.minimum(s, total - 1) % steps_per_seq

    @pl.when(s == 0)
    def _():
        mix[0] = jnp.zeros((chunks * ROWS, D_SSM), BF16)

    @pl.when(c2 == 0)
    def _():
        tlx[...] = jnp.zeros_like(tlx)
        tlb[...] = jnp.zeros_like(tlb)
        tlc[...] = jnp.zeros_like(tlc)
        h_t[...] = jnp.zeros_like(h_t)

    mix[1] = mix[0]
    slabs = []
    for n in range(0, D_MODEL, OUT_CHUNK):
        for k in range(0, D_ATTN + D_SSM, OUT_DEPTH):
            def slab(n=n, k=k):
                if k < D_ATTN:
                    lhs = pltpu.bitcast(attn_ref[0, :, k:k + OUT_DEPTH], BF16)
                else:
                    lhs = mix[1, :, k - D_ATTN:k - D_ATTN + OUT_DEPTH]
                w = pltpu.bitcast(w_ref[k // 2:(k + OUT_DEPTH) // 2, n:n + OUT_CHUNK], BF16)
                part = jnp.dot(lhs, w, preferred_element_type=F32)
                if k == 0:
                    o_acc[:, n:n + OUT_CHUNK] = part
                else:
                    o_acc[:, n:n + OUT_CHUNK] += part
            slabs.append(slab)
    points_per_chunk = N_HEADS_SSM // 2 + N_GROUPS + 2
    interleave, flush = _spread(slabs, chunks * points_per_chunk)

    rl = lax.broadcasted_iota(jnp.int32, (ROWS, ROWS), 0)
    cs = lax.broadcasted_iota(jnp.int32, (ROWS, ROWS), 1)
    causal = cs <= rl
    live = {}

    def conv_phase(ch):
        rows = slice(ch * ROWS, (ch + 1) * ROWS)
        xs = _conv_rolled(xs_ref[0, rows, :], tlx, cwx, cbx)
        bm = _conv_rolled(b_ref[0, rows, :], tlb, cwb, cbb)
        cm = _conv_rolled(c_ref[0, rows, :], tlc, cwc, cbc)
        live[ch] = dict(xs=xs, bm=bm, cm=cm)
        interleave()

    def intra_phase(ch):
        v = live[ch]
        rows = slice(ch * ROWS, (ch + 1) * ROWS)
        v["e_expa"], v["e_w"] = _ssd_intra(v["xs"], v["bm"], v["cm"], dt_ref[0, rows, :], dtb_ref, alog_ref,
                                           causal, None, ybuf.at[ch], after_pair=interleave)

    def state_phase(ch):
        v = live[ch]
        yb = ybuf.at[ch]
        xw = (v["xs"] * v["e_w"]).astype(BF16)
        for g in range(N_GROUPS):
            gs = slice(g * GROUP_W, (g + 1) * GROUP_W)
            ds = slice(g * D_STATE, (g + 1) * D_STATE)
            h_prev = h_t[:, gs]
            y_inter = jnp.dot(v["cm"][:, ds].astype(BF16), h_prev.astype(BF16), preferred_element_type=F32)
            yb[:, gs] = yb[:, gs] + y_inter * v["e_expa"][:, gs]
            s_t = jnp.dot(v["bm"][:, ds].T.astype(BF16), xw[:, gs], preferred_element_type=F32)
            h_t[:, gs] = h_prev * v["e_expa"][ROWS - 1:ROWS, gs] + s_t
            interleave()

    def norm_phase(ch):
        v = live.pop(ch)
        rows = slice(ch * ROWS, (ch + 1) * ROWS)
        y = ybuf[ch] + dskip_ref[...] * v["xs"]

        def store(g, val):
            mix[0, rows, g * GROUP_W:(g + 1) * GROUP_W] = val.astype(BF16)
        _gated_group_norm(y, z_ref[0, rows, :].astype(F32), gain_ref, store)
        interleave()

    conv_phase(0)
    intra_phase(0)
    for ch in range(chunks):
        if ch + 1 < chunks:
            conv_phase(ch + 1)
        state_phase(ch)
        if ch + 1 < chunks:
            intra_phase(ch + 1)
        norm_phase(ch)

    flush()
    o = o_acc[...]
    o = o * lax.rsqrt(jnp.mean(o * o, axis=-1, keepdims=True) + EPS)
    y_ref[0] = x_ref[0] + o * post_ref[...]

    @pl.when((c2 == steps_per_seq - 1) & (s < total))
    def _():
        first = SUBLANES - (CONV_K - 1)
        cvx_ref[0] = tlx[first:SUBLANES, :]
        cvb_ref[0] = tlb[first:SUBLANES, :]
        cvc_ref[0] = tlc[first:SUBLANES, :]
        st_ref[0] = h_t[...].T


def _ssd_outproj(z3, xbc3, dt3, attn3, x3, p, chunks):
    b, s, _ = z3.shape
    tm = chunks * ROWS
    per_seq = s // tm
    total = b * per_seq
    cur = lambda i: jnp.minimum(i, total - 1)
    prv = lambda i: jnp.maximum(i - 1, 0)
    blk = lambda w, col: pl.BlockSpec((1, tm, w), lambda i: (cur(i) // per_seq, cur(i) % per_seq, col))
    lag = lambda w: pl.BlockSpec((1, tm, w), lambda i: (prv(i) // per_seq, prv(i) % per_seq, 0))
    par = lambda r, w: pl.BlockSpec((r, w), lambda i: (0, 0), pipeline_mode=pl.Buffered(1))
    per_b = lambda r, w: pl.BlockSpec((1, r, w), lambda i: (cur(i) // per_seq, 0, 0))
    tail = CONV_K - 1
    return pl.pallas_call(
        functools.partial(_ssd_out_body, chunks=chunks, steps_per_seq=per_seq, total=total),
        name="ssd_outproj",
        grid=(total + 1,),
        in_specs=[
            blk(D_SSM, 0), blk(D_SSM, 0),
            blk(BC_W, D_SSM // BC_W), blk(BC_W, D_SSM // BC_W + 1),
            blk(DT_W, 0),
            pl.BlockSpec((1, tm // 2, D_ATTN), lambda i: (prv(i) // per_seq, prv(i) % per_seq, 0)), lag(D_MODEL),
            par(CONV_K, D_SSM), par(CONV_K, BC_W), par(CONV_K, BC_W),
            par(1, D_SSM), par(1, BC_W), par(1, BC_W),
            par(1, DT_W), par(1, DT_W), par(1, D_SSM), par(1, D_SSM),
            par((D_ATTN + D_SSM) // 2, D_MODEL), par(1, D_MODEL),
        ],
        out_specs=[
            lag(D_MODEL),
            per_b(tail, D_SSM), per_b(tail, BC_W), per_b(tail, BC_W),
            per_b(D_SSM, D_STATE),
        ],
        out_shape=[
            jax.ShapeDtypeStruct((b, s, D_MODEL), F32),
            jax.ShapeDtypeStruct((b, tail, D_SSM), F32),
            jax.ShapeDtypeStruct((b, tail, BC_W), F32),
            jax.ShapeDtypeStruct((b, tail, BC_W), F32),
            jax.ShapeDtypeStruct((b, D_SSM, D_STATE), F32),
        ],
        scratch_shapes=[
            pltpu.VMEM((SUBLANES, D_SSM), F32),
            pltpu.VMEM((SUBLANES, BC_W), F32),
            pltpu.VMEM((SUBLANES, BC_W), F32),
            pltpu.VMEM((D_STATE, D_SSM), F32),
            pltpu.VMEM((chunks, ROWS, D_SSM), F32),
            pltpu.VMEM((2, tm, D_SSM), BF16),
            pltpu.VMEM((tm, D_MODEL), F32),
        ],
        compiler_params=_cparams(("arbitrary",)),
    )(z3, xbc3, xbc3, xbc3, dt3, attn3, x3,
      p["cwx"], p["cwb"], p["cwc"], p["cbx"], p["cbb"], p["cbc"],
      p["dtb"], p["alog"], p["dskip"], p["ssm_gain"], p["w_out_packed"], p["norm_post"])


def _ssd_step_body(z_ref, xs_ref, b_ref, c_ref, dt_ref, sx_ref, sb_ref, sc_ref, h0_ref,
                   cwx, cwb, cwc, cbx, cbb, cbc, dtb_ref, alog_ref, dskip_ref, gain_ref,
                   out_ref, cvx_ref, cvb_ref, cvc_ref, st_ref,
                   xpx, xpb, xpc, ybuf, cm_scr, ea_scr, ea_t_scr, xw_t_scr, *, nb, t):
    real = nb * t
    tail = CONV_K - 1
    lo_row = SUBLANES - tail

    def conv(xp, new_ref, st_in_ref, st_out_ref, w_ref, b_ref):
        xp[:, lo_row:SUBLANES, :] = st_in_ref[...]
        xp[:, SUBLANES:SUBLANES + t, :] = new_ref[...]
        st_out_ref[...] = xp[:, SUBLANES + t - tail:SUBLANES + t, :]
        w = new_ref.shape[-1]
        act = _conv_taps(lambda k: xp[:, lo_row + k:lo_row + k + t, :].reshape(real, w), w_ref, b_ref)
        return jnp.concatenate([act, jnp.zeros((ROWS - real, w), F32)], axis=0)

    xs = conv(xpx, xs_ref, sx_ref, cvx_ref, cwx, cbx)
    bm = conv(xpb, b_ref, sb_ref, cvb_ref, cwb, cbb)
    cm = conv(xpc, c_ref, sc_ref, cvc_ref, cwc, cbc)
    dt_raw = jnp.concatenate([dt_ref[...].reshape(real, DT_W), jnp.zeros((ROWS - real, DT_W), F32)], axis=0)

    rl = lax.broadcasted_iota(jnp.int32, (ROWS, ROWS), 0)
    cs = lax.broadcasted_iota(jnp.int32, (ROWS, ROWS), 1)
    same = (rl // t) == (cs // t)
    causal = same & (cs <= rl)
    e_expa, e_w = _ssd_intra(xs, bm, cm, dt_raw, dtb_ref, alog_ref, causal, same, ybuf)

    cm_scr[...] = cm
    ea_scr[...] = e_expa
    ea_t_scr[...] = e_expa.T
    xw_t_scr[...] = (xs * e_w).T.astype(BF16)
    row_id = lax.broadcasted_iota(jnp.int32, (ROWS, D_STATE), 0)
    pair = 2 * SUBLANES
    seqs = range(nb)
    for g in range(N_GROUPS):
        gs = slice(g * GROUP_W, (g + 1) * GROUP_W)
        ds = slice(g * D_STATE, (g + 1) * D_STATE)
        h0 = [h0_ref[b, gs, :] for b in seqs]
        y_inter = [lax.dot_general(cm_scr[b * t:b * t + pair, ds].astype(BF16), h0[b].astype(BF16),
                                   (((1,), (1,)), ((), ())), preferred_element_type=F32)
                   for b in seqs]
        for b in seqs:
            rows = slice(b * t, (b + 1) * t)
            ybuf[rows, gs] = ybuf[rows, gs] + y_inter[b][0:t] * ea_scr[rows, gs]
        s_new = []
        for b in seqs:
            in_seq = (row_id >= b * t) & (row_id < (b + 1) * t)
            b_rows = jnp.where(in_seq, bm[:, ds], 0.0).astype(BF16)
            s_new.append(jnp.dot(xw_t_scr[gs, :], b_rows, preferred_element_type=F32))
        for b in seqs:
            last_col = (b + 1) * t - 1
            decay = jnp.broadcast_to(ea_t_scr[gs, last_col:last_col + 1], (GROUP_W, D_STATE))
            st_ref[b, gs, :] = h0[b] * decay + s_new[b]

    y = ybuf[0:real, :] + dskip_ref[...] * xs[0:real, :]
    z = z_ref[...].reshape(real, D_SSM)

    def store(g, v):
        out_ref[:, :, g * GROUP_W:(g + 1) * GROUP_W] = v.reshape(nb, t, GROUP_W)
    _gated_group_norm(y, z, gain_ref, store)


def _ssd_step(z3, xbc3, dt3, conv_state, h0, p, nb):
    n, t, _ = z3.shape
    tail = CONV_K - 1
    blk = lambda w, col: pl.BlockSpec((nb, t, w), lambda i: (i, 0, col))
    cst = lambda w, col: pl.BlockSpec((nb, tail, w), lambda i: (i, 0, col))
    par = lambda r, w: pl.BlockSpec((r, w), lambda i: (0, 0))
    state = pl.BlockSpec((nb, D_SSM, D_STATE), lambda i: (i, 0, 0))
    return pl.pallas_call(
        functools.partial(_ssd_step_body, nb=nb, t=t),
        name="ssd_step",
        grid=(n // nb,),
        in_specs=[
            blk(D_SSM, 0), blk(D_SSM, 0),
            blk(BC_W, D_SSM // BC_W), blk(BC_W, D_SSM // BC_W + 1),
            blk(DT_W, 0),
            cst(D_SSM, 0), cst(BC_W, D_SSM // BC_W), cst(BC_W, D_SSM // BC_W + 1),
            state,
            par(CONV_K, D_SSM), par(CONV_K, BC_W), par(CONV_K, BC_W),
            par(1, D_SSM), par(1, BC_W), par(1, BC_W),
            par(1, DT_W), par(1, DT_W), par(1, D_SSM), par(1, D_SSM),
        ],
        out_specs=[
            blk(D_SSM, 0),
            cst(D_SSM, 0), cst(BC_W, 0), cst(BC_W, 0),
            state,
        ],
        out_shape=[
            jax.ShapeDtypeStruct((n, t, D_SSM), F32),
            jax.ShapeDtypeStruct((n, tail, D_SSM), F32),
            jax.ShapeDtypeStruct((n, tail, BC_W), F32),
            jax.ShapeDtypeStruct((n, tail, BC_W), F32),
            jax.ShapeDtypeStruct((n, D_SSM, D_STATE), F32),
        ],
        scratch_shapes=[
            pltpu.VMEM((nb, SUBLANES + t, D_SSM), F32),
            pltpu.VMEM((nb, SUBLANES + t, BC_W), F32),
            pltpu.VMEM((nb, SUBLANES + t, BC_W), F32),
            pltpu.VMEM((ROWS, D_SSM), F32),
            pltpu.VMEM((ROWS, BC_W), F32),
            pltpu.VMEM((ROWS, D_SSM), F32),
            pltpu.VMEM((D_SSM, ROWS), F32),
            pltpu.VMEM((D_SSM, ROWS), BF16),
        ],
        compiler_params=_cparams(("parallel",)),
    )(z3, xbc3, xbc3, xbc3, dt3, conv_state, conv_state, conv_state, h0,
      p["cwx"], p["cwb"], p["cwc"], p["cbx"], p["cbb"], p["cbc"],
      p["dtb"], p["alog"], p["dskip"], p["ssm_gain"])


def _layer_params(norm_pre, w_in, attn_sink, attn_norm, conv_w, conv_b, dt_bias, a_log, d_skip,
                  ssm_norm, w_out, norm_post):
    w_bf16 = w_in.astype(BF16)
    w_tail = jnp.pad(w_bf16[:, 2 * WA_COLS:], ((0, 0), (0, DT_W - N_HEADS_SSM)))
    pad_h = lambda v: jnp.pad(v, (0, DT_W - N_HEADS_SSM)).reshape(1, DT_W)
    return dict(
        norm_pre=norm_pre.reshape(1, D_MODEL), w_in=w_bf16, w_tail=w_tail,
        sink=attn_sink, attn_gain=attn_norm.reshape(1, D_ATTN),
        cwx=conv_w[:, :D_SSM], cwb=conv_w[:, D_SSM:D_SSM + BC_W], cwc=conv_w[:, D_SSM + BC_W:],
        cbx=conv_b[:D_SSM].reshape(1, -1), cbb=conv_b[D_SSM:D_SSM + BC_W].reshape(1, -1),
        cbc=conv_b[D_SSM + BC_W:].reshape(1, -1),
        dtb=pad_h(dt_bias), alog=pad_h(a_log),
        dskip=jnp.broadcast_to(d_skip[:, None], (N_HEADS_SSM, HEAD_DIM)).reshape(1, D_SSM),
        ssm_gain=ssm_norm.reshape(1, D_SSM),
        w_out_packed=lax.bitcast_convert_type(
            w_out.astype(BF16).reshape((D_ATTN + D_SSM) // 2, 2, D_MODEL).transpose(0, 2, 1), jnp.uint32),
        norm_post=norm_post.reshape(1, D_MODEL),
    )


IN_TM, OUT_TM = 512, 512
ATTN_BLOCKS = 2
SSD_CHUNKS = 2
STEP_NB_ATTN, STEP_NB_SSD = 8, 8


def _layer(x, caches, p):
    n, t, _ = x.shape
    x2d = x.reshape(n * t, D_MODEL)
    act_dtype = BF16 if caches is None else F32
    h, qg, kv = _inproj_attn(x2d, p["norm_pre"], p["w_in"], IN_TM if caches is None else IN_TM // 2, act_dtype)
    qg3, kv3 = qg.reshape(n, t, 2 * D_ATTN), kv.reshape(n, t, 2 * D_KV)
    if caches is None:
        attn, z3, xbc3, dt3 = _attn_prompt(qg3, kv3, h.reshape(n, t, D_MODEL), p["w_in"], p["w_tail"],
                                           p["sink"], p["attn_gain"], ATTN_BLOCKS)
        new_k = kv3[:, t - WINDOW:, :D_KV]
        new_v = kv3[:, t - WINDOW:, D_KV:]
        y, cvx, cvb, cvc, h_new = _ssd_outproj(z3, xbc3, dt3, attn, x, p, SSD_CHUNKS)
    else:
        z, xbc, dt = _inproj_ssm(h, p["w_in"], p["w_tail"], IN_TM, act_dtype)
        z3, xbc3, dt3 = z.reshape(n, t, D_SSM), xbc.reshape(n, t, CONV_DIM), dt.reshape(n, t, DT_W)
        kbuf, vbuf, conv_buf, h0 = caches
        attn, new_k, new_v = _attn_step(qg3, kv3, kbuf.reshape(n, WINDOW, D_KV), vbuf.reshape(n, WINDOW, D_KV),
                                        p["sink"], p["attn_gain"], STEP_NB_ATTN)
        ssm, cvx, cvb, cvc, h_new = _ssd_step(z3, xbc3, dt3, conv_buf, h0.reshape(n, D_SSM, D_STATE), p,
                                              STEP_NB_SSD)
        y = _outproj(attn.reshape(n * t, D_ATTN), ssm.reshape(n * t, D_SSM), x2d,
                     p["w_out_packed"], p["norm_post"], OUT_TM)
    return (y.reshape(n, t, D_MODEL),
            new_k.reshape(n, WINDOW, N_KV_HEADS, HEAD_DIM), new_v.reshape(n, WINDOW, N_KV_HEADS, HEAD_DIM),
            jnp.concatenate([cvx, cvb, cvc], axis=-1),
            h_new.reshape(n, N_HEADS_SSM, HEAD_DIM, D_STATE))


def kernel(x_prompt, x_sample, cache_k, cache_v, state_conv, state_ssm, norm_pre, w_in, attn_sink,
           attn_norm, conv_w, conv_b, dt_bias, a_log, d_skip, ssm_norm, w_out, norm_post):
    depth = w_in.shape[0]
    yp, ys = x_prompt, x_sample
    outs_p, outs_s = [], []
    for l in range(depth):
        p = _layer_params(norm_pre[l], w_in[l], attn_sink[l], attn_norm[l], conv_w[l], conv_b[l],
                          dt_bias[l], a_log[l], d_skip[l], ssm_norm[l], w_out[l], norm_post[l])
        yp, *rest_p = _layer(yp, None, p)
        ys, *rest_s = _layer(ys, (cache_k[l], cache_v[l], state_conv[l], state_ssm[l]), p)
        outs_p.append(rest_p)
        outs_s.append(rest_s)
    stack = lambda outs, i: jnp.stack([o[i] for o in outs])
    return (yp, ys,
            stack(outs_p, 0), stack(outs_p, 1), stack(outs_p, 2), stack(outs_p, 3),
            stack(outs_s, 0), stack(outs_s, 1), stack(outs_s, 2), stack(outs_s, 3))
```

```python
import functools

import jax
import jax.numpy as jnp
from jax import lax
from jax.experimental import pallas as pl
from jax.experimental.pallas import tpu as pltpu

F32 = jnp.float32
BF16 = jnp.bfloat16

D_MODEL = 2048
HEAD_DIM = 64
N_KV_HEADS = 4
D_ATTN = 2048
D_KV = 256
WINDOW = 128
D_SSM = 2048
N_HEADS_SSM = 32
N_GROUPS = 4
D_STATE = 128
CONV_K = 4
GROUP_W = D_SSM // N_GROUPS
BC_W = N_GROUPS * D_STATE
CONV_DIM = D_SSM + 2 * BC_W
EPS = 1e-6
LOG2E = 1.4426950408889634
LANES = 128
SUBLANES = 8

DT_W = LANES
WA_COLS = 2 * D_ATTN + 2 * D_KV
WS_COLS = D_SSM + CONV_DIM + DT_W
WT_COLS = WS_COLS - WA_COLS

VMEM_LIMIT = 56 * 1024 * 1024


def _silu(x):
    h = 0.5 * x
    return h + h * jnp.tanh(h)


def _cparams(sem):
    return pltpu.CompilerParams(dimension_semantics=sem, vmem_limit_bytes=VMEM_LIMIT)


def _spread(slabs, points):
    ticks, issued = [0], [0]

    def run_until(stop):
        for run in slabs[issued[0]:stop]:
            run()
        issued[0] = max(issued[0], stop)

    def tick():
        ticks[0] += 1
        run_until(min(ticks[0], points) * len(slabs) // points)

    return tick, lambda: run_until(len(slabs))


def _inproj_attn_body(x_ref, xn_ref, g_ref, w_ref, h_ref, qg_ref, kv_ref, hbuf, *, row_chunk):
    i = pl.program_id(0)
    tm = x_ref.shape[0]
    slot = i % 2

    def norm_rows(src_ref, dst_slot, r):
        rows = slice(r, r + row_chunk)
        x = src_ref[rows, :]
        y = x * lax.rsqrt(jnp.mean(x * x, axis=-1, keepdims=True) + EPS)
        hbuf[dst_slot, rows, :] = (y * g_ref[...]).astype(BF16)

    @pl.when(i == 0)
    def _():
        for r in range(0, tm, row_chunk):
            norm_rows(x_ref, 0, r)

    h = hbuf[slot]
    h_ref[...] = h
    ahead = [functools.partial(norm_rows, xn_ref, 1 - slot, r) for r in range(0, tm, row_chunk)]
    tick, flush = _spread(ahead, WA_COLS // PROJ_WIDE)
    for lo in range(0, WA_COLS, PROJ_WIDE):
        r = jnp.dot(h, w_ref[:, lo:lo + PROJ_WIDE], preferred_element_type=F32)
        if lo < D_ATTN:
            qg_ref[:, lo:lo + PROJ_WIDE] = r.astype(qg_ref.dtype)
        elif lo < D_ATTN + 2 * D_KV:
            kv_ref[...] = r
        else:
            qg_ref[:, lo - 2 * D_KV:lo - 2 * D_KV + PROJ_WIDE] = r.astype(qg_ref.dtype)
        tick()
    flush()


def _inproj_ssm_body(h_ref, w_ref, wt_ref, z_ref, xbc_ref, dt_ref):
    for run in _ssm_projection_steps(h_ref[...], w_ref, wt_ref, z_ref, xbc_ref, dt_ref):
        run()


PROJ_CHUNK = 256
PROJ_WIDE = 512


def _ssm_projection_steps(h, w_ref, wt_ref, z_ref, xbc_ref, dt_ref):
    lead = z_ref.shape[:-1]
    rows = (slice(None),) * len(lead)

    def slab(out_ref, out_lo, w_lo, width):
        def run():
            w = w_ref[:, w_lo:w_lo + width] if w_lo < WA_COLS else wt_ref[:, w_lo - WA_COLS:w_lo - WA_COLS + width]
            r = jnp.dot(h, w, preferred_element_type=F32)
            out_ref[rows + (slice(out_lo, out_lo + width),)] = r.astype(out_ref.dtype).reshape(*lead, width)
        return run

    steps = [slab(z_ref, c, c, PROJ_CHUNK) for c in range(0, D_SSM, PROJ_CHUNK)]
    steps += [slab(xbc_ref, c, D_SSM + c, PROJ_CHUNK) for c in range(0, CONV_DIM, PROJ_CHUNK)]
    steps.append(slab(dt_ref, 0, D_SSM + CONV_DIM, DT_W))
    return steps


def _inproj_attn(x2d, gain, w_in, tm, act_dtype):
    m = x2d.shape[0]
    tm = min(tm, m)
    steps = m // tm
    row = lambda w: pl.BlockSpec((tm, w), lambda i: (i, 0))
    next_row = pl.BlockSpec((tm, D_MODEL), lambda i: (jnp.minimum(i + 1, steps - 1), 0))
    resident = lambda r, w: pl.BlockSpec((r, w), lambda i: (0, 0), pipeline_mode=pl.Buffered(1))
    return pl.pallas_call(
        functools.partial(_inproj_attn_body, row_chunk=min(128, tm)),
        name="inproj_attn",
        grid=(steps,),
        in_specs=[row(D_MODEL), next_row, resident(1, D_MODEL), resident(D_MODEL, WA_COLS)],
        out_specs=[row(D_MODEL), row(2 * D_ATTN), row(2 * D_KV)],
        out_shape=[jax.ShapeDtypeStruct((m, D_MODEL), BF16),
                   jax.ShapeDtypeStruct((m, 2 * D_ATTN), act_dtype),
                   jax.ShapeDtypeStruct((m, 2 * D_KV), F32)],
        scratch_shapes=[pltpu.VMEM((2, tm, D_MODEL), BF16)],
        compiler_params=_cparams(("arbitrary",)),
    )(x2d, x2d, gain, w_in)


def _inproj_ssm(h, w_in, w_tail, tm, act_dtype):
    m = h.shape[0]
    tm = min(tm, m)
    row = lambda w: pl.BlockSpec((tm, w), lambda i: (i, 0))
    once = dict(pipeline_mode=pl.Buffered(1))
    return pl.pallas_call(
        _inproj_ssm_body,
        name="inproj_ssm",
        grid=(m // tm,),
        in_specs=[row(D_MODEL),
                  pl.BlockSpec((D_MODEL, WA_COLS), lambda i: (0, 1), **once),
                  pl.BlockSpec((D_MODEL, WT_COLS), lambda i: (0, 0), **once)],
        out_specs=[row(D_SSM), row(CONV_DIM), row(DT_W)],
        out_shape=[jax.ShapeDtypeStruct((m, D_SSM), act_dtype),
                   jax.ShapeDtypeStruct((m, CONV_DIM), F32),
                   jax.ShapeDtypeStruct((m, DT_W), F32)],
        compiler_params=_cparams(("parallel",)),
    )(h, w_in, w_tail)


PACK_TM = 512


def _pack_rows_body(x_ref, o_ref):
    o_ref[...] = pltpu.bitcast(x_ref[...].astype(BF16), jnp.uint32)


def _pack_rows(w, tm):
    r, c = w.shape
    return pl.pallas_call(
        _pack_rows_body,
        name="pack_rows",
        grid=(r // tm,),
        in_specs=[pl.BlockSpec((tm, c), lambda i: (i, 0))],
        out_specs=pl.BlockSpec((tm // 2, c), lambda i: (i, 0)),
        out_shape=jax.ShapeDtypeStruct((r // 2, c), jnp.uint32),
        compiler_params=_cparams(("parallel",)),
    )(w)


def _outproj_body(a_ref, s_ref, x_ref, wa_ref, ws_ref, g_ref, y_ref):
    o = jnp.dot(a_ref[...].astype(BF16), pltpu.bitcast(wa_ref[...], BF16), preferred_element_type=F32)
    o = o + jnp.dot(s_ref[...].astype(BF16), pltpu.bitcast(ws_ref[...], BF16), preferred_element_type=F32)
    o = o * lax.rsqrt(jnp.mean(o * o, axis=-1, keepdims=True) + EPS)
    y_ref[...] = x_ref[...] + o * g_ref[...]


def _outproj(attn, ssm, x2d, w_out, gain, tm):
    m = x2d.shape[0]
    tm = min(tm, m)
    row = lambda i: (i, 0)
    once = dict(pipeline_mode=pl.Buffered(1))
    return pl.pallas_call(
        _outproj_body,
        name="outproj",
        grid=(m // tm,),
        in_specs=[
            pl.BlockSpec((tm, D_ATTN), row),
            pl.BlockSpec((tm, D_SSM), row),
            pl.BlockSpec((tm, D_MODEL), row),
            pl.BlockSpec((D_ATTN // 2, D_MODEL), lambda i: (0, 0), **once),
            pl.BlockSpec((D_SSM // 2, D_MODEL), lambda i: (1, 0), **once),
            pl.BlockSpec((1, D_MODEL), lambda i: (0, 0), **once),
        ],
        out_specs=pl.BlockSpec((tm, D_MODEL), row),
        out_shape=jax.ShapeDtypeStruct((m, D_MODEL), F32),
        compiler_params=_cparams(("parallel",)),
    )(attn, ssm, x2d, w_out, w_out, gain)


def _head_pair_blockdiag(slab):
    lo = lax.broadcasted_iota(jnp.int32, slab.shape, 1) < HEAD_DIM
    rolled = pltpu.roll(slab, HEAD_DIM, axis=1)
    zero = jnp.zeros_like(slab)
    bd_a = jnp.concatenate([jnp.where(lo, slab, zero), jnp.where(lo, zero, rolled)], axis=0)
    bd_b = jnp.concatenate([jnp.where(lo, rolled, zero), jnp.where(lo, zero, slab)], axis=0)
    return bd_a.astype(BF16), bd_b.astype(BF16)


ATTEND_TICKS = N_KV_HEADS * ((D_ATTN // N_KV_HEADS) // LANES + 2)


def _attend(qs, kks, vvs, masks, sink_ref, o_scrs, tick=lambda: None):
    tq, tk = masks[0].shape
    seqs = range(len(qs))
    kbd, vbd = [], []
    for b in seqs:
        kk = kks[b] * (HEAD_DIM ** -0.5 * LOG2E)
        kb, vb = [], []
        for c in range(D_KV // LANES):
            kb.extend(_head_pair_blockdiag(kk[:, c * LANES:(c + 1) * LANES]))
            vb.extend(_head_pair_blockdiag(vvs[b][:, c * LANES:(c + 1) * LANES]))
        kbd.append(kb)
        vbd.append(vb)
    lo = lax.broadcasted_iota(jnp.int32, (tq, LANES), 1) < HEAD_DIM
    pairs_per_group = (D_ATTN // N_KV_HEADS) // LANES
    for g in range(N_KV_HEADS):
        cols = [g * pairs_per_group + j for j in range(pairs_per_group)]
        s = []
        for b in seqs:
            q_stack = jnp.concatenate([qs[b][:, c * LANES:(c + 1) * LANES] for c in cols], axis=0).astype(BF16)
            s.append(lax.dot_general(q_stack, kbd[b][g], (((1,), (1,)), ((), ())), preferred_element_type=F32))
        tick()
        p_rows = [[] for _ in seqs]
        inv_rows = [[] for _ in seqs]
        for j, col in enumerate(cols):
            ps = [[] for _ in seqs]
            invs = [[] for _ in seqs]
            for e in range(2):
                snk = sink_ref[2 * col + e] * LOG2E
                for b in seqs:
                    se = jnp.where(masks[b], s[b][j * tq:(j + 1) * tq, e * tk:(e + 1) * tk], -jnp.inf)
                    mx = jnp.maximum(jnp.max(se, axis=-1, keepdims=True), snk)
                    p = jnp.exp2(se - mx)
                    den = jnp.sum(p, axis=-1, keepdims=True) + jnp.exp2(snk - mx)
                    ps[b].append(p)
                    invs[b].append(1.0 / den)
            for b in seqs:
                p_rows[b].append(jnp.concatenate(ps[b], axis=1).astype(BF16))
                inv_rows[b].append(jnp.where(lo, invs[b][0], invs[b][1]))
            tick()
        o = [jnp.dot(jnp.concatenate(p_rows[b], axis=0), vbd[b][g], preferred_element_type=F32) for b in seqs]
        for b in seqs:
            for j, col in enumerate(cols):
                o_scrs[b][:, col * LANES:(col + 1) * LANES] = o[b][j * tq:(j + 1) * tq, :] * inv_rows[b][j]
        tick()


def _gate_norm(o, gate, gain):
    y = o * _silu(gate)
    y = y * lax.rsqrt(jnp.mean(y * y, axis=-1, keepdims=True) + EPS)
    return y * gain


def _attn_prompt_body(sink_ref, q_ref, g_ref, kp_ref, kc_ref, vp_ref, vc_ref, gain_ref, h_ref, w_ref, wt_ref,
                      out_ref, z_ref, xbc_ref, dt_ref, o_scr, *, blocks):
    i = pl.program_id(1)
    ql = lax.broadcasted_iota(jnp.int32, (WINDOW, 2 * WINDOW), 0)
    ks = lax.broadcasted_iota(jnp.int32, (WINDOW, 2 * WINDOW), 1)
    rel = ql + WINDOW - ks
    band = (rel >= 0) & (rel < WINDOW)
    steps = _ssm_projection_steps(h_ref[0], w_ref, wt_ref, z_ref, xbc_ref, dt_ref)
    tick, flush = _spread(steps, ATTEND_TICKS + blocks)

    qs, kks, vvs, masks = [], [], [], []
    for sub in range(blocks):
        rows = slice(sub * WINDOW, (sub + 1) * WINDOW)
        if sub == 0:
            k_prev, v_prev = kp_ref[0], vp_ref[0]
            first_key = jnp.where(i > 0, 0, WINDOW)
            masks.append(band & (ks >= first_key))
        else:
            before = slice((sub - 1) * WINDOW, sub * WINDOW)
            k_prev, v_prev = kc_ref[0, before, :], vc_ref[0, before, :]
            masks.append(band)
        qs.append(q_ref[0, rows, :])
        kks.append(jnp.concatenate([k_prev, kc_ref[0, rows, :]], axis=0))
        vvs.append(jnp.concatenate([v_prev, vc_ref[0, rows, :]], axis=0))
    _attend(qs, kks, vvs, masks, sink_ref, [o_scr.at[sub] for sub in range(blocks)], tick)
    for sub in range(blocks):
        rows = slice(sub * WINDOW, (sub + 1) * WINDOW)
        y = _gate_norm(o_scr[sub], g_ref[0, rows, :].astype(F32), gain_ref[...]).astype(BF16)
        out_ref[0, sub * (WINDOW // 2):(sub + 1) * (WINDOW // 2), :] = pltpu.bitcast(y, jnp.uint32)
        tick()
    flush()


def _attn_prompt(qg3, kv3, h3, w_in, w_tail, sink, gain, blocks):
    b, s, _ = qg3.shape
    tm = blocks * WINDOW
    blk = lambda w, col: pl.BlockSpec((1, tm, w), lambda bi, i: (bi, i, col))
    prev = lambda w, col: pl.BlockSpec((1, WINDOW, w), lambda bi, i: (bi, jnp.maximum(i * blocks - 1, 0), col))
    once = dict(pipeline_mode=pl.Buffered(1))
    return pl.pallas_call(
        functools.partial(_attn_prompt_body, blocks=blocks),
        name="attn_prompt",
        grid=(b, s // tm),
        in_specs=[
            pl.BlockSpec(memory_space=pltpu.SMEM),
            blk(D_ATTN, 0), blk(D_ATTN, 1),
            prev(D_KV, 0), blk(D_KV, 0),
            prev(D_KV, 1), blk(D_KV, 1),
            pl.BlockSpec((1, D_ATTN), lambda bi, i: (0, 0), **once),
            blk(D_MODEL, 0),
            pl.BlockSpec((D_MODEL, WA_COLS), lambda bi, i: (0, 1), **once),
            pl.BlockSpec((D_MODEL, WT_COLS), lambda bi, i: (0, 0), **once),
        ],
        out_specs=[pl.BlockSpec((1, tm // 2, D_ATTN), lambda bi, i: (bi, i, 0)),
                   blk(D_SSM, 0), blk(CONV_DIM, 0), blk(DT_W, 0)],
        out_shape=[jax.ShapeDtypeStruct((b, s // 2, D_ATTN), jnp.uint32),
                   jax.ShapeDtypeStruct((b, s, D_SSM), BF16),
                   jax.ShapeDtypeStruct((b, s, CONV_DIM), F32),
                   jax.ShapeDtypeStruct((b, s, DT_W), F32)],
        scratch_shapes=[pltpu.VMEM((blocks, WINDOW, D_ATTN), F32)],
        compiler_params=_cparams(("parallel", "arbitrary")),
    )(sink, qg3, qg3, kv3, kv3, kv3, kv3, gain, h3, w_in, w_tail)


Q_PAD = 16


def _attn_step_body(sink_ref, q_ref, g_ref, kn_ref, vn_ref, ck_ref, cv_ref, gain_ref,
                    out_ref, ko_ref, vo_ref, o_scr, *, nb, t):
    tk = 2 * WINDOW
    ql = lax.broadcasted_iota(jnp.int32, (Q_PAD, tk), 0)
    ks = lax.broadcasted_iota(jnp.int32, (Q_PAD, tk), 1)
    rel = ql + WINDOW - ks
    mask = (rel >= 0) & (rel < WINDOW) & (ql < t)
    pad_k = jnp.zeros((tk - WINDOW - t, D_KV), F32)
    pad_q = jnp.zeros((Q_PAD - t, D_ATTN), F32)

    kks = [jnp.concatenate([ck_ref[b], kn_ref[b], pad_k], axis=0) for b in range(nb)]
    vvs = [jnp.concatenate([cv_ref[b], vn_ref[b], pad_k], axis=0) for b in range(nb)]
    qs = [jnp.concatenate([q_ref[b], pad_q], axis=0) for b in range(nb)]
    for b in range(nb):
        ko_ref[b] = kks[b][t:t + WINDOW, :]
        vo_ref[b] = vvs[b][t:t + WINDOW, :]
    _attend(qs, kks, vvs, [mask] * nb, sink_ref, [o_scr.at[b] for b in range(nb)])
    for b in range(nb):
        out_ref[b] = _gate_norm(o_scr[b, 0:t, :], g_ref[b], gain_ref[...])


def _attn_step(qg3, kv3, cache_k, cache_v, sink, gain, nb):
    n, t, _ = qg3.shape
    blk = lambda w, col: pl.BlockSpec((nb, t, w), lambda i: (i, 0, col))
    cache = pl.BlockSpec((nb, WINDOW, D_KV), lambda i: (i, 0, 0))
    return pl.pallas_call(
        functools.partial(_attn_step_body, nb=nb, t=t),
        name="attn_step",
        grid=(n // nb,),
        in_specs=[
            pl.BlockSpec(memory_space=pltpu.SMEM),
            blk(D_ATTN, 0), blk(D_ATTN, 1),
            blk(D_KV, 0), blk(D_KV, 1),
            cache,
            cache,
            pl.BlockSpec((1, D_ATTN), lambda i: (0, 0)),
        ],
        out_specs=[pl.BlockSpec((nb, t, D_ATTN), lambda i: (i, 0, 0)), cache, cache],
        out_shape=[
            jax.ShapeDtypeStruct((n, t, D_ATTN), F32),
            jax.ShapeDtypeStruct((n, WINDOW, D_KV), F32),
            jax.ShapeDtypeStruct((n, WINDOW, D_KV), F32),
        ],
        scratch_shapes=[pltpu.VMEM((nb, Q_PAD, D_ATTN), F32)],
        compiler_params=_cparams(("parallel",)),
    )(sink, qg3, qg3, kv3, kv3, cache_k, cache_v, gain)


ROWS = 128


def _expand_heads(v):
    r = v.shape[0]
    lo = lax.broadcasted_iota(jnp.int32, (r, LANES), 1) < HEAD_DIM
    cols = []
    for c in range(D_SSM // LANES):
        va = jnp.broadcast_to(v[:, 2 * c:2 * c + 1], (r, LANES))
        vb = jnp.broadcast_to(v[:, 2 * c + 1:2 * c + 2], (r, LANES))
        cols.append(jnp.where(lo, va, vb))
    return jnp.concatenate(cols, axis=1)


def _conv_taps(load_rows, w_ref, b_ref):
    acc = b_ref[...] + w_ref[CONV_K - 1:CONV_K, :] * load_rows(CONV_K - 1)
    for k in range(CONV_K - 1):
        acc = acc + w_ref[k:k + 1, :] * load_rows(k)
    return _silu(acc)


def _conv_rolled(x, tail_scr, w_ref, b_ref):
    taps = CONV_K - 1
    width = x.shape[1]
    tiles = ROWS // SUBLANES
    with_prev = jnp.concatenate([tail_scr[...], x], axis=0).reshape(tiles + 1, SUBLANES, width)
    row = lax.broadcasted_iota(jnp.int32, (tiles, SUBLANES, width), 1)
    acc = b_ref[...] + w_ref[taps:taps + 1, :] * x
    for j in range(1, CONV_K):
        rot = pltpu.roll(with_prev, j, axis=1)
        shifted = jnp.where(row < j, rot[:tiles], rot[1:]).reshape(ROWS, width)
        acc = acc + w_ref[taps - j:taps - j + 1, :] * shifted
    tail_scr[...] = x[ROWS - SUBLANES:, :]
    return _silu(acc)


def _masked_row_sums(mask, x):
    m = jnp.where(mask, 1.0, 0.0).astype(BF16)
    hi = x.astype(BF16)
    rest = x - hi.astype(F32)
    mid = rest.astype(BF16)
    lo = (rest - mid.astype(F32)).astype(BF16)
    dot = lambda piece: jnp.dot(m, piece, preferred_element_type=F32)
    return dot(hi) + dot(mid) + dot(lo)


def _ssd_intra(xs, bm, cm, dt_raw, dtb_ref, alog_ref, causal, same, ybuf, after_pair=None):
    dt = jax.nn.softplus(dt_raw + dtb_ref[...])
    a_neg = -jnp.exp(alog_ref[...])
    dta = dt * (a_neg * LOG2E)
    a = _masked_row_sums(causal, dta)
    tot = a[ROWS - 1:ROWS, :] if same is None else _masked_row_sums(same, dta)
    r_t = (a - jnp.log2(dt)).T
    lo = lax.broadcasted_iota(jnp.int32, (ROWS, LANES), 1) < HEAD_DIM
    heads_per_group = N_HEADS_SSM // N_GROUPS
    for g in range(N_GROUPS):
        cg = cm[:, g * D_STATE:(g + 1) * D_STATE].astype(BF16)
        bg = bm[:, g * D_STATE:(g + 1) * D_STATE].astype(BF16)
        cb = lax.dot_general(cg, bg, (((1,), (1,)), ((), ())), preferred_element_type=F32)
        for j in range(heads_per_group // 2):
            c = g * (heads_per_group // 2) + j
            ms = []
            for e in range(2):
                h = 2 * c + e
                diff = jnp.broadcast_to(a[:, h:h + 1], (ROWS, ROWS)) - r_t[h:h + 1, :]
                ms.append(cb * jnp.exp2(jnp.where(causal, diff, -jnp.inf)))
            lhs = jnp.concatenate(ms, axis=1).astype(BF16)
            xc = xs[:, c * LANES:(c + 1) * LANES]
            zero = jnp.zeros_like(xc)
            rhs = jnp.concatenate([jnp.where(lo, xc, zero), jnp.where(lo, zero, xc)], axis=0).astype(BF16)
            ybuf[:, c * LANES:(c + 1) * LANES] = jnp.dot(lhs, rhs, preferred_element_type=F32)
            if after_pair is not None:
                after_pair()
    e_expa = _expand_heads(jnp.exp2(a))
    e_w = _expand_heads(jnp.exp2(tot - a) * dt)
    return e_expa, e_w


def _gated_group_norm(y, z, gain_ref, out_ref_store):
    yz = y * _silu(z)
    for g in range(N_GROUPS):
        blk = yz[:, g * GROUP_W:(g + 1) * GROUP_W]
        nrm = blk * lax.rsqrt(jnp.mean(blk * blk, axis=-1, keepdims=True) + EPS)
        out_ref_store(g, nrm * gain_ref[:, g * GROUP_W:(g + 1) * GROUP_W])


OUT_CHUNK = 256
OUT_DEPTH = 1024


def _ssd_out_body(z_ref, xs_ref, b_ref, c_ref, dt_ref, attn_ref, x_ref,
                  cwx, cwb, cwc, cbx, cbb, cbc, dtb_ref, alog_ref, dskip_ref, gain_ref, w_ref, post_ref,
                  y_ref, cvx_ref, cvb_ref, cvc_ref, st_ref,
                  tlx, tlb, tlc, h_t, ybuf, mix, o_acc, *, chunks, steps_per_seq, total):
    s = pl.program_id(0)
    c2 = jnp.minimum(s, total - 1) % steps_per_seq

    @pl.when(s == 0)
    def _():
        mix[0] = jnp.zeros((chunks * ROWS, D_SSM), BF16)

    @pl.when(c2 == 0)
    def _():
        tlx[...] = jnp.zeros_like(tlx)
        tlb[...] = jnp.zeros_like(tlb)
        tlc[...] = jnp.zeros_like(tlc)
        h_t[...] = jnp.zeros_like(h_t)

    mix[1] = mix[0]
    slabs = []
    for n in range(0, D_MODEL, OUT_CHUNK):
        for k in range(0, D_ATTN + D_SSM, OUT_DEPTH):
            def slab(n=n, k=k):
                if k < D_ATTN:
                    lhs = pltpu.bitcast(attn_ref[0, :, k:k + OUT_DEPTH], BF16)
                else:
                    lhs = mix[1, :, k - D_ATTN:k - D_ATTN + OUT_DEPTH]
                w = pltpu.bitcast(w_ref[k // 2:(k + OUT_DEPTH) // 2, n:n + OUT_CHUNK], BF16)
                part = jnp.dot(lhs, w, preferred_element_type=F32)
                if k == 0:
                    o_acc[:, n:n + OUT_CHUNK] = part
                else:
                    o_acc[:, n:n + OUT_CHUNK] += part
            slabs.append(slab)
    points_per_chunk = N_HEADS_SSM // 2 + N_GROUPS + 2
    interleave, flush = _spread(slabs, chunks * points_per_chunk)

    rl = lax.broadcasted_iota(jnp.int32, (ROWS, ROWS), 0)
    cs = lax.broadcasted_iota(jnp.int32, (ROWS, ROWS), 1)
    causal = cs <= rl
    live = {}

    def conv_phase(ch):
        rows = slice(ch * ROWS, (ch + 1) * ROWS)
        xs = _conv_rolled(xs_ref[0, rows, :], tlx, cwx, cbx)
        bm = _conv_rolled(b_ref[0, rows, :], tlb, cwb, cbb)
        cm = _conv_rolled(c_ref[0, rows, :], tlc, cwc, cbc)
        live[ch] = dict(xs=xs, bm=bm, cm=cm)
        interleave()

    def intra_phase(ch):
        v = live[ch]
        rows = slice(ch * ROWS, (ch + 1) * ROWS)
        v["e_expa"], v["e_w"] = _ssd_intra(v["xs"], v["bm"], v["cm"], dt_ref[0, rows, :], dtb_ref, alog_ref,
                                           causal, None, ybuf.at[ch], after_pair=interleave)

    def state_phase(ch):
        v = live[ch]
        yb = ybuf.at[ch]
        xw = (v["xs"] * v["e_w"]).astype(BF16)
        for g in range(N_GROUPS):
            gs = slice(g * GROUP_W, (g + 1) * GROUP_W)
            ds = slice(g * D_STATE, (g + 1) * D_STATE)
            h_prev = h_t[:, gs]
            y_inter = jnp.dot(v["cm"][:, ds].astype(BF16), h_prev.astype(BF16), preferred_element_type=F32)
            yb[:, gs] = yb[:, gs] + y_inter * v["e_expa"][:, gs]
            s_t = jnp.dot(v["bm"][:, ds].T.astype(BF16), xw[:, gs], preferred_element_type=F32)
            h_t[:, gs] = h_prev * v["e_expa"][ROWS - 1:ROWS, gs] + s_t
            interleave()

    def norm_phase(ch):
        v = live.pop(ch)
        rows = slice(ch * ROWS, (ch + 1) * ROWS)
        y = ybuf[ch] + dskip_ref[...] * v["xs"]

        def store(g, val):
            mix[0, rows, g * GROUP_W:(g + 1) * GROUP_W] = val.astype(BF16)
        _gated_group_norm(y, z_ref[0, rows, :].astype(F32), gain_ref, store)
        interleave()

    conv_phase(0)
    intra_phase(0)
    for ch in range(chunks):
        if ch + 1 < chunks:
            conv_phase(ch + 1)
        state_phase(ch)
        if ch + 1 < chunks:
            intra_phase(ch + 1)
        norm_phase(ch)

    flush()
    o = o_acc[...]
    o = o * lax.rsqrt(jnp.mean(o * o, axis=-1, keepdims=True) + EPS)
    y_ref[0] = x_ref[0] + o * post_ref[...]

    @pl.when((c2 == steps_per_seq - 1) & (s < total))
    def _():
        first = SUBLANES - (CONV_K - 1)
        cvx_ref[0] = tlx[first:SUBLANES, :]
        cvb_ref[0] = tlb[first:SUBLANES, :]
        cvc_ref[0] = tlc[first:SUBLANES, :]
        st_ref[0] = h_t[...].T


def _ssd_outproj(z3, xbc3, dt3, attn3, x3, p, chunks):
    b, s, _ = z3.shape
    tm = chunks * ROWS
    per_seq = s // tm
    total = b * per_seq
    cur = lambda i: jnp.minimum(i, total - 1)
    prv = lambda i: jnp.maximum(i - 1, 0)
    blk = lambda w, col: pl.BlockSpec((1, tm, w), lambda i: (cur(i) // per_seq, cur(i) % per_seq, col))
    lag = lambda w: pl.BlockSpec((1, tm, w), lambda i: (prv(i) // per_seq, prv(i) % per_seq, 0))
    par = lambda r, w: pl.BlockSpec((r, w), lambda i: (0, 0), pipeline_mode=pl.Buffered(1))
    per_b = lambda r, w: pl.BlockSpec((1, r, w), lambda i: (cur(i) // per_seq, 0, 0))
    tail = CONV_K - 1
    return pl.pallas_call(
        functools.partial(_ssd_out_body, chunks=chunks, steps_per_seq=per_seq, total=total),
        name="ssd_outproj",
        grid=(total + 1,),
        in_specs=[
            blk(D_SSM, 0), blk(D_SSM, 0),
            blk(BC_W, D_SSM // BC_W), blk(BC_W, D_SSM // BC_W + 1),
            blk(DT_W, 0),
            pl.BlockSpec((1, tm // 2, D_ATTN), lambda i: (prv(i) // per_seq, prv(i) % per_seq, 0)), lag(D_MODEL),
            par(CONV_K, D_SSM), par(CONV_K, BC_W), par(CONV_K, BC_W),
            par(1, D_SSM), par(1, BC_W), par(1, BC_W),
            par(1, DT_W), par(1, DT_W), par(1, D_SSM), par(1, D_SSM),
            par((D_ATTN + D_SSM) // 2, D_MODEL), par(1, D_MODEL),
        ],
        out_specs=[
            lag(D_MODEL),
            per_b(tail, D_SSM), per_b(tail, BC_W), per_b(tail, BC_W),
            per_b(D_SSM, D_STATE),
        ],
        out_shape=[
            jax.ShapeDtypeStruct((b, s, D_MODEL), F32),
            jax.ShapeDtypeStruct((b, tail, D_SSM), F32),
            jax.ShapeDtypeStruct((b, tail, BC_W), F32),
            jax.ShapeDtypeStruct((b, tail, BC_W), F32),
            jax.ShapeDtypeStruct((b, D_SSM, D_STATE), F32),
        ],
        scratch_shapes=[
            pltpu.VMEM((SUBLANES, D_SSM), F32),
            pltpu.VMEM((SUBLANES, BC_W), F32),
            pltpu.VMEM((SUBLANES, BC_W), F32),
            pltpu.VMEM((D_STATE, D_SSM), F32),
            pltpu.VMEM((chunks, ROWS, D_SSM), F32),
            pltpu.VMEM((2, tm, D_SSM), BF16),
            pltpu.VMEM((tm, D_MODEL), F32),
        ],
        compiler_params=_cparams(("arbitrary",)),
    )(z3, xbc3, xbc3, xbc3, dt3, attn3, x3,
      p["cwx"], p["cwb"], p["cwc"], p["cbx"], p["cbb"], p["cbc"],
      p["dtb"], p["alog"], p["dskip"], p["ssm_gain"], p["w_out_packed"], p["norm_post"])


def _ssd_step_body(z_ref, xs_ref, b_ref, c_ref, dt_ref, sx_ref, sb_ref, sc_ref, h0_ref,
                   cwx, cwb, cwc, cbx, cbb, cbc, dtb_ref, alog_ref, dskip_ref, gain_ref,
                   out_ref, cvx_ref, cvb_ref, cvc_ref, st_ref,
                   xpx, xpb, xpc, ybuf, cm_scr, ea_scr, ea_t_scr, xw_t_scr, *, nb, t):
    real = nb * t
    tail = CONV_K - 1
    lo_row = SUBLANES - tail

    def conv(xp, new_ref, st_in_ref, st_out_ref, w_ref, b_ref):
        xp[:, lo_row:SUBLANES, :] = st_in_ref[...]
        xp[:, SUBLANES:SUBLANES + t, :] = new_ref[...]
        st_out_ref[...] = xp[:, SUBLANES + t - tail:SUBLANES + t, :]
        w = new_ref.shape[-1]
        act = _conv_taps(lambda k: xp[:, lo_row + k:lo_row + k + t, :].reshape(real, w), w_ref, b_ref)
        return jnp.concatenate([act, jnp.zeros((ROWS - real, w), F32)], axis=0)

    xs = conv(xpx, xs_ref, sx_ref, cvx_ref, cwx, cbx)
    bm = conv(xpb, b_ref, sb_ref, cvb_ref, cwb, cbb)
    cm = conv(xpc, c_ref, sc_ref, cvc_ref, cwc, cbc)
    dt_raw = jnp.concatenate([dt_ref[...].reshape(real, DT_W), jnp.zeros((ROWS - real, DT_W), F32)], axis=0)

    rl = lax.broadcasted_iota(jnp.int32, (ROWS, ROWS), 0)
    cs = lax.broadcasted_iota(jnp.int32, (ROWS, ROWS), 1)
    same = (rl // t) == (cs // t)
    causal = same & (cs <= rl)
    e_expa, e_w = _ssd_intra(xs, bm, cm, dt_raw, dtb_ref, alog_ref, causal, same, ybuf)

    cm_scr[...] = cm
    ea_scr[...] = e_expa
    ea_t_scr[...] = e_expa.T
    xw_t_scr[...] = (xs * e_w).T.astype(BF16)
    row_id = lax.broadcasted_iota(jnp.int32, (ROWS, D_STATE), 0)
    pair = 2 * SUBLANES
    seqs = range(nb)
    for g in range(N_GROUPS):
        gs = slice(g * GROUP_W, (g + 1) * GROUP_W)
        ds = slice(g * D_STATE, (g + 1) * D_STATE)
        h0 = [h0_ref[b, gs, :] for b in seqs]
        y_inter = [lax.dot_general(cm_scr[b * t:b * t + pair, ds].astype(BF16), h0[b].astype(BF16),
                                   (((1,), (1,)), ((), ())), preferred_element_type=F32)
                   for b in seqs]
        for b in seqs:
            rows = slice(b * t, (b + 1) * t)
            ybuf[rows, gs] = ybuf[rows, gs] + y_inter[b][0:t] * ea_scr[rows, gs]
        s_new = []
        for b in seqs:
            in_seq = (row_id >= b * t) & (row_id < (b + 1) * t)
            b_rows = jnp.where(in_seq, bm[:, ds], 0.0).astype(BF16)
            s_new.append(jnp.dot(xw_t_scr[gs, :], b_rows, preferred_element_type=F32))
        for b in seqs:
            last_col = (b + 1) * t - 1
            decay = jnp.broadcast_to(ea_t_scr[gs, last_col:last_col + 1], (GROUP_W, D_STATE))
            st_ref[b, gs, :] = h0[b] * decay + s_new[b]

    y = ybuf[0:real, :] + dskip_ref[...] * xs[0:real, :]
    z = z_ref[...].reshape(real, D_SSM)

    def store(g, v):
        out_ref[:, :, g * GROUP_W:(g + 1) * GROUP_W] = v.reshape(nb, t, GROUP_W)
    _gated_group_norm(y, z, gain_ref, store)


def _ssd_step(z3, xbc3, dt3, conv_state, h0, p, nb):
    n, t, _ = z3.shape
    tail = CONV_K - 1
    blk = lambda w, col: pl.BlockSpec((nb, t, w), lambda i: (i, 0, col))
    cst = lambda w, col: pl.BlockSpec((nb, tail, w), lambda i: (i, 0, col))
    par = lambda r, w: pl.BlockSpec((r, w), lambda i: (0, 0))
    state = pl.BlockSpec((nb, D_SSM, D_STATE), lambda i: (i, 0, 0))
    return pl.pallas_call(
        functools.partial(_ssd_step_body, nb=nb, t=t),
        name="ssd_step",
        grid=(n // nb,),
        in_specs=[
            blk(D_SSM, 0), blk(D_SSM, 0),
            blk(BC_W, D_SSM // BC_W), blk(BC_W, D_SSM // BC_W + 1),
            blk(DT_W, 0),
            cst(D_SSM, 0), cst(BC_W, D_SSM // BC_W), cst(BC_W, D_SSM // BC_W + 1),
            state,
            par(CONV_K, D_SSM), par(CONV_K, BC_W), par(CONV_K, BC_W),
            par(1, D_SSM), par(1, BC_W), par(1, BC_W),
            par(1, DT_W), par(1, DT_W), par(1, D_SSM), par(1, D_SSM),
        ],
        out_specs=[
            blk(D_SSM, 0),
            cst(D_SSM, 0), cst(BC_W, 0), cst(BC_W, 0),
            state,
        ],
        out_shape=[
            jax.ShapeDtypeStruct((n, t, D_SSM), F32),
            jax.ShapeDtypeStruct((n, tail, D_SSM), F32),
            jax.ShapeDtypeStruct((n, tail, BC_W), F32),
            jax.ShapeDtypeStruct((n, tail, BC_W), F32),
            jax.ShapeDtypeStruct((n, D_SSM, D_STATE), F32),
        ],
        scratch_shapes=[
            pltpu.VMEM((nb, SUBLANES + t, D_SSM), F32),
            pltpu.VMEM((nb, SUBLANES + t, BC_W), F32),
            pltpu.VMEM((nb, SUBLANES + t, BC_W), F32),
            pltpu.VMEM((ROWS, D_SSM), F32),
            pltpu.VMEM((ROWS, BC_W), F32),
            pltpu.VMEM((ROWS, D_SSM), F32),
            pltpu.VMEM((D_SSM, ROWS), F32),
            pltpu.VMEM((D_SSM, ROWS), BF16),
        ],
        compiler_params=_cparams(("parallel",)),
    )(z3, xbc3, xbc3, xbc3, dt3, conv_state, conv_state, conv_state, h0,
      p["cwx"], p["cwb"], p["cwc"], p["cbx"], p["cbb"], p["cbc"],
      p["dtb"], p["alog"], p["dskip"], p["ssm_gain"])


def _layer_params(norm_pre, w_in, attn_sink, attn_norm, conv_w, conv_b, dt_bias, a_log, d_skip,
                  ssm_norm, w_out, norm_post):
    w_bf16 = w_in.astype(BF16)
    w_tail = jnp.pad(w_bf16[:, 2 * WA_COLS:], ((0, 0), (0, DT_W - N_HEADS_SSM)))
    pad_h = lambda v: jnp.pad(v, (0, DT_W - N_HEADS_SSM)).reshape(1, DT_W)
    return dict(
        norm_pre=norm_pre.reshape(1, D_MODEL), w_in=w_bf16, w_tail=w_tail,
        sink=attn_sink, attn_gain=attn_norm.reshape(1, D_ATTN),
        cwx=conv_w[:, :D_SSM], cwb=conv_w[:, D_SSM:D_SSM + BC_W], cwc=conv_w[:, D_SSM + BC_W:],
        cbx=conv_b[:D_SSM].reshape(1, -1), cbb=conv_b[D_SSM:D_SSM + BC_W].reshape(1, -1),
        cbc=conv_b[D_SSM + BC_W:].reshape(1, -1),
        dtb=pad_h(dt_bias), alog=pad_h(a_log),
        dskip=jnp.broadcast_to(d_skip[:, None], (N_HEADS_SSM, HEAD_DIM)).reshape(1, D_SSM),
        ssm_gain=ssm_norm.reshape(1, D_SSM),
        w_out_packed=_pack_rows(w_out, PACK_TM),
        norm_post=norm_post.reshape(1, D_MODEL),
    )


IN_TM, OUT_TM = 512, 512
ATTN_BLOCKS = 2
SSD_CHUNKS = 2
STEP_NB_ATTN, STEP_NB_SSD = 8, 8


def _layer(x, caches, p):
    n, t, _ = x.shape
    x2d = x.reshape(n * t, D_MODEL)
    act_dtype = BF16 if caches is None else F32
    h, qg, kv = _inproj_attn(x2d, p["norm_pre"], p["w_in"], IN_TM if caches is None else IN_TM // 2, act_dtype)
    qg3, kv3 = qg.reshape(n, t, 2 * D_ATTN), kv.reshape(n, t, 2 * D_KV)
    if caches is None:
        attn, z3, xbc3, dt3 = _attn_prompt(qg3, kv3, h.reshape(n, t, D_MODEL), p["w_in"], p["w_tail"],
                                           p["sink"], p["attn_gain"], ATTN_BLOCKS)
        new_k = kv3[:, t - WINDOW:, :D_KV]
        new_v = kv3[:, t - WINDOW:, D_KV:]
        y, cvx, cvb, cvc, h_new = _ssd_outproj(z3, xbc3, dt3, attn, x, p, SSD_CHUNKS)
    else:
        z, xbc, dt = _inproj_ssm(h, p["w_in"], p["w_tail"], IN_TM, act_dtype)
        z3, xbc3, dt3 = z.reshape(n, t, D_SSM), xbc.reshape(n, t, CONV_DIM), dt.reshape(n, t, DT_W)
        kbuf, vbuf, conv_buf, h0 = caches
        attn, new_k, new_v = _attn_step(qg3, kv3, kbuf.reshape(n, WINDOW, D_KV), vbuf.reshape(n, WINDOW, D_KV),
                                        p["sink"], p["attn_gain"], STEP_NB_ATTN)
        ssm, cvx, cvb, cvc, h_new = _ssd_step(z3, xbc3, dt3, conv_buf, h0.reshape(n, D_SSM, D_STATE), p,
                                              STEP_NB_SSD)
        y = _outproj(attn.reshape(n * t, D_ATTN), ssm.reshape(n * t, D_SSM), x2d,
                     p["w_out_packed"], p["norm_post"], OUT_TM)
    return (y.reshape(n, t, D_MODEL),
            new_k.reshape(n, WINDOW, N_KV_HEADS, HEAD_DIM), new_v.reshape(n, WINDOW, N_KV_HEADS, HEAD_DIM),
            jnp.concatenate([cvx, cvb, cvc], axis=-1),
            h_new.reshape(n, N_HEADS_SSM, HEAD_DIM, D_STATE))


def kernel(x_prompt, x_sample, cache_k, cache_v, state_conv, state_ssm, norm_pre, w_in, attn_sink,
           attn_norm, conv_w, conv_b, dt_bias, a_log, d_skip, ssm_norm, w_out, norm_post):
    depth = w_in.shape[0]
    yp, ys = x_prompt, x_sample
    outs_p, outs_s = [], []
    for l in range(depth):
        p = _layer_params(norm_pre[l], w_in[l], attn_sink[l], attn_norm[l], conv_w[l], conv_b[l],
                          dt_bias[l], a_log[l], d_skip[l], ssm_norm[l], w_out[l], norm_post[l])
        yp, *rest_p = _layer(yp, None, p)
        ys, *rest_s = _layer(ys, (cache_k[l], cache_v[l], state_conv[l], state_ssm[l]), p)
        outs_p.append(rest_p)
        outs_s.append(rest_s)
    stack = lambda outs, i: jnp.stack([o[i] for o in outs])
    return (yp, ys,
            stack(outs_p, 0), stack(outs_p, 1), stack(outs_p, 2), stack(outs_p, 3),
            stack(outs_s, 0), stack(outs_s, 1), stack(outs_s, 2), stack(outs_s, 3))
```

```python
import functools

import jax
import jax.numpy as jnp
from jax import lax
from jax.experimental import pallas as pl
from jax.experimental.pallas import tpu as pltpu

F32 = jnp.float32
BF16 = jnp.bfloat16

D_MODEL = 2048
HEAD_DIM = 64
N_KV_HEADS = 4
D_ATTN = 2048
D_KV = 256
WINDOW = 128
D_SSM = 2048
N_HEADS_SSM = 32
N_GROUPS = 4
D_STATE = 128
CONV_K = 4
GROUP_W = D_SSM // N_GROUPS
BC_W = N_GROUPS * D_STATE
CONV_DIM = D_SSM + 2 * BC_W
EPS = 1e-6
LOG2E = 1.4426950408889634
LANES = 128
SUBLANES = 8

DT_W = LANES
WA_COLS = 2 * D_ATTN + 2 * D_KV
WS_COLS = D_SSM + CONV_DIM + DT_W
WT_COLS = WS_COLS - WA_COLS

VMEM_LIMIT = 56 * 1024 * 1024


def _silu(x):
    h = 0.5 * x
    return h + h * jnp.tanh(h)


def _cparams(sem):
    return pltpu.CompilerParams(dimension_semantics=sem, vmem_limit_bytes=VMEM_LIMIT)


def _spread(slabs, points):
    ticks, issued = [0], [0]

    def run_until(stop):
        for run in slabs[issued[0]:stop]:
            run()
        issued[0] = max(issued[0], stop)

    def tick():
        ticks[0] += 1
        run_until(min(ticks[0], points) * len(slabs) // points)

    return tick, lambda: run_until(len(slabs))


def _inproj_attn_body(x_ref, xn_ref, g_ref, w_ref, h_ref, qg_ref, kv_ref, hbuf, *, row_chunk):
    i = pl.program_id(0)
    tm = x_ref.shape[0]
    slot = i % 2

    def norm_rows(src_ref, dst_slot, r):
        rows = slice(r, r + row_chunk)
        x = src_ref[rows, :]
        y = x * lax.rsqrt(jnp.mean(x * x, axis=-1, keepdims=True) + EPS)
        hbuf[dst_slot, rows, :] = (y * g_ref[...]).astype(BF16)

    @pl.when(i == 0)
    def _():
        for r in range(0, tm, row_chunk):
            norm_rows(x_ref, 0, r)

    h = hbuf[slot]
    h_ref[...] = h
    ahead = [functools.partial(norm_rows, xn_ref, 1 - slot, r) for r in range(0, tm, row_chunk)]
    tick, flush = _spread(ahead, WA_COLS // PROJ_WIDE)
    for lo in range(0, WA_COLS, PROJ_WIDE):
        r = jnp.dot(h, w_ref[:, lo:lo + PROJ_WIDE], preferred_element_type=F32)
        if lo < D_ATTN:
            qg_ref[:, lo:lo + PROJ_WIDE] = r.astype(qg_ref.dtype)
        elif lo < D_ATTN + 2 * D_KV:
            kv_ref[...] = r
        else:
            qg_ref[:, lo - 2 * D_KV:lo - 2 * D_KV + PROJ_WIDE] = r.astype(qg_ref.dtype)
        tick()
    flush()


def _inproj_ssm_body(h_ref, w_ref, wt_ref, z_ref, xbc_ref, dt_ref):
    for run in _ssm_projection_steps(h_ref[...], w_ref, wt_ref, z_ref, xbc_ref, dt_ref):
        run()


PROJ_CHUNK = 256
PROJ_WIDE = 512


def _ssm_projection_steps(h, w_ref, wt_ref, z_ref, xbc_ref, dt_ref):
    lead = z_ref.shape[:-1]
    rows = (slice(None),) * len(lead)

    def slab(out_ref, out_lo, w_lo, width):
        def run():
            w = w_ref[:, w_lo:w_lo + width] if w_lo < WA_COLS else wt_ref[:, w_lo - WA_COLS:w_lo - WA_COLS + width]
            r = jnp.dot(h, w, preferred_element_type=F32)
            out_ref[rows + (slice(out_lo, out_lo + width),)] = r.astype(out_ref.dtype).reshape(*lead, width)
        return run

    steps = [slab(z_ref, c, c, PROJ_CHUNK) for c in range(0, D_SSM, PROJ_CHUNK)]
    steps += [slab(xbc_ref, c, D_SSM + c, PROJ_CHUNK) for c in range(0, CONV_DIM, PROJ_CHUNK)]
    steps.append(slab(dt_ref, 0, D_SSM + CONV_DIM, DT_W))
    return steps


def _inproj_attn(x2d, gain, w_in, tm, act_dtype):
    m = x2d.shape[0]
    tm = min(tm, m)
    steps = m // tm
    row = lambda w: pl.BlockSpec((tm, w), lambda i: (i, 0))
    next_row = pl.BlockSpec((tm, D_MODEL), lambda i: (jnp.minimum(i + 1, steps - 1), 0))
    resident = lambda r, w: pl.BlockSpec((r, w), lambda i: (0, 0), pipeline_mode=pl.Buffered(1))
    return pl.pallas_call(
        functools.partial(_inproj_attn_body, row_chunk=min(128, tm)),
        name="inproj_attn",
        grid=(steps,),
        in_specs=[row(D_MODEL), next_row, resident(1, D_MODEL), resident(D_MODEL, WA_COLS)],
        out_specs=[row(D_MODEL), row(2 * D_ATTN), row(2 * D_KV)],
        out_shape=[jax.ShapeDtypeStruct((m, D_MODEL), BF16),
                   jax.ShapeDtypeStruct((m, 2 * D_ATTN), act_dtype),
                   jax.ShapeDtypeStruct((m, 2 * D_KV), F32)],
        scratch_shapes=[pltpu.VMEM((2, tm, D_MODEL), BF16)],
        compiler_params=_cparams(("arbitrary",)),
    )(x2d, x2d, gain, w_in)


def _inproj_ssm(h, w_in, w_tail, tm, act_dtype):
    m = h.shape[0]
    tm = min(tm, m)
    row = lambda w: pl.BlockSpec((tm, w), lambda i: (i, 0))
    once = dict(pipeline_mode=pl.Buffered(1))
    return pl.pallas_call(
        _inproj_ssm_body,
        name="inproj_ssm",
        grid=(m // tm,),
        in_specs=[row(D_MODEL),
                  pl.BlockSpec((D_MODEL, WA_COLS), lambda i: (0, 1), **once),
                  pl.BlockSpec((D_MODEL, WT_COLS), lambda i: (0, 0), **once)],
        out_specs=[row(D_SSM), row(CONV_DIM), row(DT_W)],
        out_shape=[jax.ShapeDtypeStruct((m, D_SSM), act_dtype),
                   jax.ShapeDtypeStruct((m, CONV_DIM), F32),
                   jax.ShapeDtypeStruct((m, DT_W), F32)],
        compiler_params=_cparams(("parallel",)),
    )(h, w_in, w_tail)


PACK_TM = 512


def _pack_rows_body(x_ref, o_ref):
    o_ref[...] = pltpu.bitcast(x_ref[...].astype(BF16), jnp.uint32)


CAST_TM = 256


def _cast_rows_body(x_ref, o_ref):
    o_ref[...] = x_ref[...].astype(BF16)


def _cast_bf16(w, tm):
    r, c = w.shape
    return pl.pallas_call(
        _cast_rows_body,
        name="cast_rows",
        grid=(r // tm,),
        in_specs=[pl.BlockSpec((tm, c), lambda i: (i, 0))],
        out_specs=pl.BlockSpec((tm, c), lambda i: (i, 0)),
        out_shape=jax.ShapeDtypeStruct((r, c), BF16),
        compiler_params=_cparams(("parallel",)),
    )(w)


def _pack_rows(w, tm):
    r, c = w.shape
    return pl.pallas_call(
        _pack_rows_body,
        name="pack_rows",
        grid=(r // tm,),
        in_specs=[pl.BlockSpec((tm, c), lambda i: (i, 0))],
        out_specs=pl.BlockSpec((tm // 2, c), lambda i: (i, 0)),
        out_shape=jax.ShapeDtypeStruct((r // 2, c), jnp.uint32),
        compiler_params=_cparams(("parallel",)),
    )(w)


def _outproj_body(a_ref, s_ref, x_ref, wa_ref, ws_ref, g_ref, y_ref):
    o = jnp.dot(a_ref[...].astype(BF16), pltpu.bitcast(wa_ref[...], BF16), preferred_element_type=F32)
    o = o + jnp.dot(s_ref[...].astype(BF16), pltpu.bitcast(ws_ref[...], BF16), preferred_element_type=F32)
    o = o * lax.rsqrt(jnp.mean(o * o, axis=-1, keepdims=True) + EPS)
    y_ref[...] = x_ref[...] + o * g_ref[...]


def _outproj(attn, ssm, x2d, w_out, gain, tm):
    m = x2d.shape[0]
    tm = min(tm, m)
    row = lambda i: (i, 0)
    once = dict(pipeline_mode=pl.Buffered(1))
    return pl.pallas_call(
        _outproj_body,
        name="outproj",
        grid=(m // tm,),
        in_specs=[
            pl.BlockSpec((tm, D_ATTN), row),
            pl.BlockSpec((tm, D_SSM), row),
            pl.BlockSpec((tm, D_MODEL), row),
            pl.BlockSpec((D_ATTN // 2, D_MODEL), lambda i: (0, 0), **once),
            pl.BlockSpec((D_SSM // 2, D_MODEL), lambda i: (1, 0), **once),
            pl.BlockSpec((1, D_MODEL), lambda i: (0, 0), **once),
        ],
        out_specs=pl.BlockSpec((tm, D_MODEL), row),
        out_shape=jax.ShapeDtypeStruct((m, D_MODEL), F32),
        compiler_params=_cparams(("parallel",)),
    )(attn, ssm, x2d, w_out, w_out, gain)


def _head_pair_blockdiag(slab):
    lo = lax.broadcasted_iota(jnp.int32, slab.shape, 1) < HEAD_DIM
    rolled = pltpu.roll(slab, HEAD_DIM, axis=1)
    zero = jnp.zeros_like(slab)
    bd_a = jnp.concatenate([jnp.where(lo, slab, zero), jnp.where(lo, zero, rolled)], axis=0)
    bd_b = jnp.concatenate([jnp.where(lo, rolled, zero), jnp.where(lo, zero, slab)], axis=0)
    return bd_a.astype(BF16), bd_b.astype(BF16)


ATTEND_TICKS = N_KV_HEADS * ((D_ATTN // N_KV_HEADS) // LANES + 2)


def _attend(qs, kks, vvs, masks, sink_ref, o_scrs, tick=lambda: None):
    tq, tk = masks[0].shape
    seqs = range(len(qs))
    kbd, vbd = [], []
    for b in seqs:
        kk = kks[b] * (HEAD_DIM ** -0.5 * LOG2E)
        kb, vb = [], []
        for c in range(D_KV // LANES):
            kb.extend(_head_pair_blockdiag(kk[:, c * LANES:(c + 1) * LANES]))
            vb.extend(_head_pair_blockdiag(vvs[b][:, c * LANES:(c + 1) * LANES]))
        kbd.append(kb)
        vbd.append(vb)
    lo = lax.broadcasted_iota(jnp.int32, (tq, LANES), 1) < HEAD_DIM
    pairs_per_group = (D_ATTN // N_KV_HEADS) // LANES
    for g in range(N_KV_HEADS):
        cols = [g * pairs_per_group + j for j in range(pairs_per_group)]
        s = []
        for b in seqs:
            q_stack = jnp.concatenate([qs[b][:, c * LANES:(c + 1) * LANES] for c in cols], axis=0).astype(BF16)
            s.append(lax.dot_general(q_stack, kbd[b][g], (((1,), (1,)), ((), ())), preferred_element_type=F32))
        tick()
        p_rows = [[] for _ in seqs]
        inv_rows = [[] for _ in seqs]
        for j, col in enumerate(cols):
            ps = [[] for _ in seqs]
            invs = [[] for _ in seqs]
            for e in range(2):
                snk = sink_ref[2 * col + e] * LOG2E
                for b in seqs:
                    se = jnp.where(masks[b], s[b][j * tq:(j + 1) * tq, e * tk:(e + 1) * tk], -jnp.inf)
                    mx = jnp.maximum(jnp.max(se, axis=-1, keepdims=True), snk)
                    p = jnp.exp2(se - mx)
                    den = jnp.sum(p, axis=-1, keepdims=True) + jnp.exp2(snk - mx)
                    ps[b].append(p)
                    invs[b].append(1.0 / den)
            for b in seqs:
                p_rows[b].append(jnp.concatenate(ps[b], axis=1).astype(BF16))
                inv_rows[b].append(jnp.where(lo, invs[b][0], invs[b][1]))
            tick()
        o = [jnp.dot(jnp.concatenate(p_rows[b], axis=0), vbd[b][g], preferred_element_type=F32) for b in seqs]
        for b in seqs:
            for j, col in enumerate(cols):
                o_scrs[b][:, col * LANES:(col + 1) * LANES] = o[b][j * tq:(j + 1) * tq, :] * inv_rows[b][j]
        tick()


def _gate_norm(o, gate, gain):
    y = o * _silu(gate)
    y = y * lax.rsqrt(jnp.mean(y * y, axis=-1, keepdims=True) + EPS)
    return y * gain


def _attn_prompt_body(sink_ref, q_ref, g_ref, kp_ref, kc_ref, vp_ref, vc_ref, gain_ref, h_ref, w_ref, wt_ref,
                      out_ref, z_ref, xbc_ref, dt_ref, o_scr, *, blocks):
    i = pl.program_id(1)
    ql = lax.broadcasted_iota(jnp.int32, (WINDOW, 2 * WINDOW), 0)
    ks = lax.broadcasted_iota(jnp.int32, (WINDOW, 2 * WINDOW), 1)
    rel = ql + WINDOW - ks
    band = (rel >= 0) & (rel < WINDOW)
    steps = _ssm_projection_steps(h_ref[0], w_ref, wt_ref, z_ref, xbc_ref, dt_ref)
    tick, flush = _spread(steps, ATTEND_TICKS + blocks)

    qs, kks, vvs, masks = [], [], [], []
    for sub in range(blocks):
        rows = slice(sub * WINDOW, (sub + 1) * WINDOW)
        if sub == 0:
            k_prev, v_prev = kp_ref[0], vp_ref[0]
            first_key = jnp.where(i > 0, 0, WINDOW)
            masks.append(band & (ks >= first_key))
        else:
            before = slice((sub - 1) * WINDOW, sub * WINDOW)
            k_prev, v_prev = kc_ref[0, before, :], vc_ref[0, before, :]
            masks.append(band)
        qs.append(q_ref[0, rows, :])
        kks.append(jnp.concatenate([k_prev, kc_ref[0, rows, :]], axis=0))
        vvs.append(jnp.concatenate([v_prev, vc_ref[0, rows, :]], axis=0))
    _attend(qs, kks, vvs, masks, sink_ref, [o_scr.at[sub] for sub in range(blocks)], tick)
    for sub in range(blocks):
        rows = slice(sub * WINDOW, (sub + 1) * WINDOW)
        y = _gate_norm(o_scr[sub], g_ref[0, rows, :].astype(F32), gain_ref[...]).astype(BF16)
        out_ref[0, sub * (WINDOW // 2):(sub + 1) * (WINDOW // 2), :] = pltpu.bitcast(y, jnp.uint32)
        tick()
    flush()


def _attn_prompt(qg3, kv3, h3, w_in, w_tail, sink, gain, blocks):
    b, s, _ = qg3.shape
    tm = blocks * WINDOW
    blk = lambda w, col: pl.BlockSpec((1, tm, w), lambda bi, i: (bi, i, col))
    prev = lambda w, col: pl.BlockSpec((1, WINDOW, w), lambda bi, i: (bi, jnp.maximum(i * blocks - 1, 0), col))
    once = dict(pipeline_mode=pl.Buffered(1))
    return pl.pallas_call(
        functools.partial(_attn_prompt_body, blocks=blocks),
        name="attn_prompt",
        grid=(b, s // tm),
        in_specs=[
            pl.BlockSpec(memory_space=pltpu.SMEM),
            blk(D_ATTN, 0), blk(D_ATTN, 1),
            prev(D_KV, 0), blk(D_KV, 0),
            prev(D_KV, 1), blk(D_KV, 1),
            pl.BlockSpec((1, D_ATTN), lambda bi, i: (0, 0), **once),
            blk(D_MODEL, 0),
            pl.BlockSpec((D_MODEL, WA_COLS), lambda bi, i: (0, 1), **once),
            pl.BlockSpec((D_MODEL, WT_COLS), lambda bi, i: (0, 0), **once),
        ],
        out_specs=[pl.BlockSpec((1, tm // 2, D_ATTN), lambda bi, i: (bi, i, 0)),
                   blk(D_SSM, 0), blk(CONV_DIM, 0), blk(DT_W, 0)],
        out_shape=[jax.ShapeDtypeStruct((b, s // 2, D_ATTN), jnp.uint32),
                   jax.ShapeDtypeStruct((b, s, D_SSM), BF16),
                   jax.ShapeDtypeStruct((b, s, CONV_DIM), F32),
                   jax.ShapeDtypeStruct((b, s, DT_W), F32)],
        scratch_shapes=[pltpu.VMEM((blocks, WINDOW, D_ATTN), F32)],
        compiler_params=_cparams(("parallel", "arbitrary")),
    )(sink, qg3, qg3, kv3, kv3, kv3, kv3, gain, h3, w_in, w_tail)


Q_PAD = 16


def _attn_step_body(sink_ref, q_ref, g_ref, kn_ref, vn_ref, ck_ref, cv_ref, gain_ref,
                    out_ref, ko_ref, vo_ref, o_scr, *, nb, t):
    tk = 2 * WINDOW
    ql = lax.broadcasted_iota(jnp.int32, (Q_PAD, tk), 0)
    ks = lax.broadcasted_iota(jnp.int32, (Q_PAD, tk), 1)
    rel = ql + WINDOW - ks
    mask = (rel >= 0) & (rel < WINDOW) & (ql < t)
    pad_k = jnp.zeros((tk - WINDOW - t, D_KV), F32)
    pad_q = jnp.zeros((Q_PAD - t, D_ATTN), F32)

    kks = [jnp.concatenate([ck_ref[b], kn_ref[b], pad_k], axis=0) for b in range(nb)]
    vvs = [jnp.concatenate([cv_ref[b], vn_ref[b], pad_k], axis=0) for b in range(nb)]
    qs = [jnp.concatenate([q_ref[b], pad_q], axis=0) for b in range(nb)]
    for b in range(nb):
        ko_ref[b] = kks[b][t:t + WINDOW, :]
        vo_ref[b] = vvs[b][t:t + WINDOW, :]
    _attend(qs, kks, vvs, [mask] * nb, sink_ref, [o_scr.at[b] for b in range(nb)])
    for b in range(nb):
        out_ref[b] = _gate_norm(o_scr[b, 0:t, :], g_ref[b], gain_ref[...])


def _attn_step(qg3, kv3, cache_k, cache_v, sink, gain, nb):
    n, t, _ = qg3.shape
    blk = lambda w, col: pl.BlockSpec((nb, t, w), lambda i: (i, 0, col))
    cache = pl.BlockSpec((nb, WINDOW, D_KV), lambda i: (i, 0, 0))
    return pl.pallas_call(
        functools.partial(_attn_step_body, nb=nb, t=t),
        name="attn_step",
        grid=(n // nb,),
        in_specs=[
            pl.BlockSpec(memory_space=pltpu.SMEM),
            blk(D_ATTN, 0), blk(D_ATTN, 1),
            blk(D_KV, 0), blk(D_KV, 1),
            cache,
            cache,
            pl.BlockSpec((1, D_ATTN), lambda i: (0, 0)),
        ],
        out_specs=[pl.BlockSpec((nb, t, D_ATTN), lambda i: (i, 0, 0)), cache, cache],
        out_shape=[
            jax.ShapeDtypeStruct((n, t, D_ATTN), F32),
            jax.ShapeDtypeStruct((n, WINDOW, D_KV), F32),
            jax.ShapeDtypeStruct((n, WINDOW, D_KV), F32),
        ],
        scratch_shapes=[pltpu.VMEM((nb, Q_PAD, D_ATTN), F32)],
        compiler_params=_cparams(("parallel",)),
    )(sink, qg3, qg3, kv3, kv3, cache_k, cache_v, gain)


ROWS = 128


def _expand_heads(v):
    r = v.shape[0]
    lo = lax.broadcasted_iota(jnp.int32, (r, LANES), 1) < HEAD_DIM
    cols = []
    for c in range(D_SSM // LANES):
        va = jnp.broadcast_to(v[:, 2 * c:2 * c + 1], (r, LANES))
        vb = jnp.broadcast_to(v[:, 2 * c + 1:2 * c + 2], (r, LANES))
        cols.append(jnp.where(lo, va, vb))
    return jnp.concatenate(cols, axis=1)


def _conv_taps(load_rows, w_ref, b_ref):
    acc = b_ref[...] + w_ref[CONV_K - 1:CONV_K, :] * load_rows(CONV_K - 1)
    for k in range(CONV_K - 1):
        acc = acc + w_ref[k:k + 1, :] * load_rows(k)
    return _silu(acc)


def _conv_rolled(x, tail_scr, w_ref, b_ref):
    taps = CONV_K - 1
    width = x.shape[1]
    tiles = ROWS // SUBLANES
    with_prev = jnp.concatenate([tail_scr[...], x], axis=0).reshape(tiles + 1, SUBLANES, width)
    row = lax.broadcasted_iota(jnp.int32, (tiles, SUBLANES, width), 1)
    acc = b_ref[...] + w_ref[taps:taps + 1, :] * x
    for j in range(1, CONV_K):
        rot = pltpu.roll(with_prev, j, axis=1)
        shifted = jnp.where(row < j, rot[:tiles], rot[1:]).reshape(ROWS, width)
        acc = acc + w_ref[taps - j:taps - j + 1, :] * shifted
    tail_scr[...] = x[ROWS - SUBLANES:, :]
    return _silu(acc)


def _masked_row_sums(mask, x):
    m = jnp.where(mask, 1.0, 0.0).astype(BF16)
    hi = x.astype(BF16)
    rest = x - hi.astype(F32)
    mid = rest.astype(BF16)
    lo = (rest - mid.astype(F32)).astype(BF16)
    dot = lambda piece: jnp.dot(m, piece, preferred_element_type=F32)
    return dot(hi) + dot(mid) + dot(lo)


def _ssd_intra(xs, bm, cm, dt_raw, dtb_ref, alog_ref, causal, same, ybuf, after_pair=None):
    dt = jax.nn.softplus(dt_raw + dtb_ref[...])
    a_neg = -jnp.exp(alog_ref[...])
    dta = dt * (a_neg * LOG2E)
    a = _masked_row_sums(causal, dta)
    tot = a[ROWS - 1:ROWS, :] if same is None else _masked_row_sums(same, dta)
    r_t = (a - jnp.log2(dt)).T
    lo = lax.broadcasted_iota(jnp.int32, (ROWS, LANES), 1) < HEAD_DIM
    heads_per_group = N_HEADS_SSM // N_GROUPS
    for g in range(N_GROUPS):
        cg = cm[:, g * D_STATE:(g + 1) * D_STATE].astype(BF16)
        bg = bm[:, g * D_STATE:(g + 1) * D_STATE].astype(BF16)
        cb = lax.dot_general(cg, bg, (((1,), (1,)), ((), ())), preferred_element_type=F32)
        for j in range(heads_per_group // 2):
            c = g * (heads_per_group // 2) + j
            ms = []
            for e in range(2):
                h = 2 * c + e
                diff = jnp.broadcast_to(a[:, h:h + 1], (ROWS, ROWS)) - r_t[h:h + 1, :]
                ms.append(cb * jnp.exp2(jnp.where(causal, diff, -jnp.inf)))
            lhs = jnp.concatenate(ms, axis=1).astype(BF16)
            xc = xs[:, c * LANES:(c + 1) * LANES]
            zero = jnp.zeros_like(xc)
            rhs = jnp.concatenate([jnp.where(lo, xc, zero), jnp.where(lo, zero, xc)], axis=0).astype(BF16)
            ybuf[:, c * LANES:(c + 1) * LANES] = jnp.dot(lhs, rhs, preferred_element_type=F32)
            if after_pair is not None:
                after_pair()
    e_expa = _expand_heads(jnp.exp2(a))
    e_w = _expand_heads(jnp.exp2(tot - a) * dt)
    return e_expa, e_w


def _gated_group_norm(y, z, gain_ref, out_ref_store):
    yz = y * _silu(z)
    for g in range(N_GROUPS):
        blk = yz[:, g * GROUP_W:(g + 1) * GROUP_W]
        nrm = blk * lax.rsqrt(jnp.mean(blk * blk, axis=-1, keepdims=True) + EPS)
        out_ref_store(g, nrm * gain_ref[:, g * GROUP_W:(g + 1) * GROUP_W])


OUT_CHUNK = 256
OUT_DEPTH = 1024


def _ssd_out_body(z_ref, xs_ref, b_ref, c_ref, dt_ref, attn_ref, x_ref,
                  cwx, cwb, cwc, cbx, cbb, cbc, dtb_ref, alog_ref, dskip_ref, gain_ref, w_ref, post_ref,
                  y_ref, cvx_ref, cvb_ref, cvc_ref, st_ref,
                  tlx, tlb, tlc, h_t, ybuf, mix, o_acc, *, chunks, steps_per_seq, total):
    s = pl.program_id(0)
    c2 = jnp.minimum(s, total - 1) % steps_per_seq

    @pl.when(s == 0)
    def _():
        mix[0] = jnp.zeros((chunks * ROWS, D_SSM), BF16)

    @pl.when(c2 == 0)
    def _():
        tlx[...] = jnp.zeros_like(tlx)
        tlb[...] = jnp.zeros_like(tlb)
        tlc[...] = jnp.zeros_like(tlc)
        h_t[...] = jnp.zeros_like(h_t)

    mix[1] = mix[0]
    slabs = []
    for n in range(0, D_MODEL, OUT_CHUNK):
        for k in range(0, D_ATTN + D_SSM, OUT_DEPTH):
            def slab(n=n, k=k):
                if k < D_ATTN:
                    lhs = pltpu.bitcast(attn_ref[0, :, k:k + OUT_DEPTH], BF16)
                else:
                    lhs = mix[1, :, k - D_ATTN:k - D_ATTN + OUT_DEPTH]
                w = pltpu.bitcast(w_ref[k // 2:(k + OUT_DEPTH) // 2, n:n + OUT_CHUNK], BF16)
                part = jnp.dot(lhs, w, preferred_element_type=F32)
                if k == 0:
                    o_acc[:, n:n + OUT_CHUNK] = part
                else:
                    o_acc[:, n:n + OUT_CHUNK] += part
            slabs.append(slab)
    points_per_chunk = N_HEADS_SSM // 2 + N_GROUPS + 2
    interleave, flush = _spread(slabs, chunks * points_per_chunk)

    rl = lax.broadcasted_iota(jnp.int32, (ROWS, ROWS), 0)
    cs = lax.broadcasted_iota(jnp.int32, (ROWS, ROWS), 1)
    causal = cs <= rl
    live = {}

    def conv_phase(ch):
        rows = slice(ch * ROWS, (ch + 1) * ROWS)
        xs = _conv_rolled(xs_ref[0, rows, :], tlx, cwx, cbx)
        bm = _conv_rolled(b_ref[0, rows, :], tlb, cwb, cbb)
        cm = _conv_rolled(c_ref[0, rows, :], tlc, cwc, cbc)
        live[ch] = dict(xs=xs, bm=bm, cm=cm)
        interleave()

    def intra_phase(ch):
        v = live[ch]
        rows = slice(ch * ROWS, (ch + 1) * ROWS)
        v["e_expa"], v["e_w"] = _ssd_intra(v["xs"], v["bm"], v["cm"], dt_ref[0, rows, :], dtb_ref, alog_ref,
                                           causal, None, ybuf.at[ch], after_pair=interleave)

    def state_phase(ch):
        v = live[ch]
        yb = ybuf.at[ch]
        xw = (v["xs"] * v["e_w"]).astype(BF16)
        for g in range(N_GROUPS):
            gs = slice(g * GROUP_W, (g + 1) * GROUP_W)
            ds = slice(g * D_STATE, (g + 1) * D_STATE)
            h_prev = h_t[:, gs]
            y_inter = jnp.dot(v["cm"][:, ds].astype(BF16), h_prev.astype(BF16), preferred_element_type=F32)
            yb[:, gs] = yb[:, gs] + y_inter * v["e_expa"][:, gs]
            s_t = jnp.dot(v["bm"][:, ds].T.astype(BF16), xw[:, gs], preferred_element_type=F32)
            h_t[:, gs] = h_prev * v["e_expa"][ROWS - 1:ROWS, gs] + s_t
            interleave()

    def norm_phase(ch):
        v = live.pop(ch)
        rows = slice(ch * ROWS, (ch + 1) * ROWS)
        y = ybuf[ch] + dskip_ref[...] * v["xs"]

        def store(g, val):
            mix[0, rows, g * GROUP_W:(g + 1) * GROUP_W] = val.astype(BF16)
        _gated_group_norm(y, z_ref[0, rows, :].astype(F32), gain_ref, store)
        interleave()

    conv_phase(0)
    intra_phase(0)
    for ch in range(chunks):
        if ch + 1 < chunks:
            conv_phase(ch + 1)
        state_phase(ch)
        if ch + 1 < chunks:
            intra_phase(ch + 1)
        norm_phase(ch)

    flush()
    o = o_acc[...]
    o = o * lax.rsqrt(jnp.mean(o * o, axis=-1, keepdims=True) + EPS)
    y_ref[0] = x_ref[0] + o * post_ref[...]

    @pl.when((c2 == steps_per_seq - 1) & (s < total))
    def _():
        first = SUBLANES - (CONV_K - 1)
        cvx_ref[0] = tlx[first:SUBLANES, :]
        cvb_ref[0] = tlb[first:SUBLANES, :]
        cvc_ref[0] = tlc[first:SUBLANES, :]
        st_ref[0] = h_t[...].T


def _ssd_outproj(z3, xbc3, dt3, attn3, x3, p, chunks):
    b, s, _ = z3.shape
    tm = chunks * ROWS
    per_seq = s // tm
    total = b * per_seq
    cur = lambda i: jnp.minimum(i, total - 1)
    prv = lambda i: jnp.maximum(i - 1, 0)
    blk = lambda w, col: pl.BlockSpec((1, tm, w), lambda i: (cur(i) // per_seq, cur(i) % per_seq, col))
    lag = lambda w: pl.BlockSpec((1, tm, w), lambda i: (prv(i) // per_seq, prv(i) % per_seq, 0))
    par = lambda r, w: pl.BlockSpec((r, w), lambda i: (0, 0), pipeline_mode=pl.Buffered(1))
    per_b = lambda r, w: pl.BlockSpec((1, r, w), lambda i: (cur(i) // per_seq, 0, 0))
    tail = CONV_K - 1
    return pl.pallas_call(
        functools.partial(_ssd_out_body, chunks=chunks, steps_per_seq=per_seq, total=total),
        name="ssd_outproj",
        grid=(total + 1,),
        in_specs=[
            blk(D_SSM, 0), blk(D_SSM, 0),
            blk(BC_W, D_SSM // BC_W), blk(BC_W, D_SSM // BC_W + 1),
            blk(DT_W, 0),
            pl.BlockSpec((1, tm // 2, D_ATTN), lambda i: (prv(i) // per_seq, prv(i) % per_seq, 0)), lag(D_MODEL),
            par(CONV_K, D_SSM), par(CONV_K, BC_W), par(CONV_K, BC_W),
            par(1, D_SSM), par(1, BC_W), par(1, BC_W),
            par(1, DT_W), par(1, DT_W), par(1, D_SSM), par(1, D_SSM),
            par((D_ATTN + D_SSM) // 2, D_MODEL), par(1, D_MODEL),
        ],
        out_specs=[
            lag(D_MODEL),
            per_b(tail, D_SSM), per_b(tail, BC_W), per_b(tail, BC_W),
            per_b(D_SSM, D_STATE),
        ],
        out_shape=[
            jax.ShapeDtypeStruct((b, s, D_MODEL), F32),
            jax.ShapeDtypeStruct((b, tail, D_SSM), F32),
            jax.ShapeDtypeStruct((b, tail, BC_W), F32),
            jax.ShapeDtypeStruct((b, tail, BC_W), F32),
            jax.ShapeDtypeStruct((b, D_SSM, D_STATE), F32),
        ],
        scratch_shapes=[
            pltpu.VMEM((SUBLANES, D_SSM), F32),
            pltpu.VMEM((SUBLANES, BC_W), F32),
            pltpu.VMEM((SUBLANES, BC_W), F32),
            pltpu.VMEM((D_STATE, D_SSM), F32),
            pltpu.VMEM((chunks, ROWS, D_SSM), F32),
            pltpu.VMEM((2, tm, D_SSM), BF16),
            pltpu.VMEM((tm, D_MODEL), F32),
        ],
        compiler_params=_cparams(("arbitrary",)),
    )(z3, xbc3, xbc3, xbc3, dt3, attn3, x3,
      p["cwx"], p["cwb"], p["cwc"], p["cbx"], p["cbb"], p["cbc"],
      p["dtb"], p["alog"], p["dskip"], p["ssm_gain"], p["w_out_packed"], p["norm_post"])


def _ssd_step_body(z_ref, xs_ref, b_ref, c_ref, dt_ref, sx_ref, sb_ref, sc_ref, h0_ref,
                   cwx, cwb, cwc, cbx, cbb, cbc, dtb_ref, alog_ref, dskip_ref, gain_ref,
                   out_ref, cvx_ref, cvb_ref, cvc_ref, st_ref,
                   xpx, xpb, xpc, ybuf, cm_scr, ea_scr, ea_t_scr, xw_t_scr, *, nb, t):
    real = nb * t
    tail = CONV_K - 1
    lo_row = SUBLANES - tail

    def conv(xp, new_ref, st_in_ref, st_out_ref, w_ref, b_ref):
        xp[:, lo_row:SUBLANES, :] = st_in_ref[...]
        xp[:, SUBLANES:SUBLANES + t, :] = new_ref[...]
        st_out_ref[...] = xp[:, SUBLANES + t - tail:SUBLANES + t, :]
        w = new_ref.shape[-1]
        act = _conv_taps(lambda k: xp[:, lo_row + k:lo_row + k + t, :].reshape(real, w), w_ref, b_ref)
        return jnp.concatenate([act, jnp.zeros((ROWS - real, w), F32)], axis=0)

    xs = conv(xpx, xs_ref, sx_ref, cvx_ref, cwx, cbx)
    bm = conv(xpb, b_ref, sb_ref, cvb_ref, cwb, cbb)
    cm = conv(xpc, c_ref, sc_ref, cvc_ref, cwc, cbc)
    dt_raw = jnp.concatenate([dt_ref[...].reshape(real, DT_W), jnp.zeros((ROWS - real, DT_W), F32)], axis=0)

    rl = lax.broadcasted_iota(jnp.int32, (ROWS, ROWS), 0)
    cs = lax.broadcasted_iota(jnp.int32, (ROWS, ROWS), 1)
    same = (rl // t) == (cs // t)
    causal = same & (cs <= rl)
    e_expa, e_w = _ssd_intra(xs, bm, cm, dt_raw, dtb_ref, alog_ref, causal, same, ybuf)

    cm_scr[...] = cm
    ea_scr[...] = e_expa
    ea_t_scr[...] = e_expa.T
    xw_t_scr[...] = (xs * e_w).T.astype(BF16)
    row_id = lax.broadcasted_iota(jnp.int32, (ROWS, D_STATE), 0)
    pair = 2 * SUBLANES
    seqs = range(nb)
    for g in range(N_GROUPS):
        gs = slice(g * GROUP_W, (g + 1) * GROUP_W)
        ds = slice(g * D_STATE, (g + 1) * D_STATE)
        h0 = [h0_ref[b, gs, :] for b in seqs]
        y_inter = [lax.dot_general(cm_scr[b * t:b * t + pair, ds].astype(BF16), h0[b].astype(BF16),
                                   (((1,), (1,)), ((), ())), preferred_element_type=F32)
                   for b in seqs]
        for b in seqs:
            rows = slice(b * t, (b + 1) * t)
            ybuf[rows, gs] = ybuf[rows, gs] + y_inter[b][0:t] * ea_scr[rows, gs]
        s_new = []
        for b in seqs:
            in_seq = (row_id >= b * t) & (row_id < (b + 1) * t)
            b_rows = jnp.where(in_seq, bm[:, ds], 0.0).astype(BF16)
            s_new.append(jnp.dot(xw_t_scr[gs, :], b_rows, preferred_element_type=F32))
        for b in seqs:
            last_col = (b + 1) * t - 1
            decay = jnp.broadcast_to(ea_t_scr[gs, last_col:last_col + 1], (GROUP_W, D_STATE))
            st_ref[b, gs, :] = h0[b] * decay + s_new[b]

    y = ybuf[0:real, :] + dskip_ref[...] * xs[0:real, :]
    z = z_ref[...].reshape(real, D_SSM)

    def store(g, v):
        out_ref[:, :, g * GROUP_W:(g + 1) * GROUP_W] = v.reshape(nb, t, GROUP_W)
    _gated_group_norm(y, z, gain_ref, store)


def _ssd_step(z3, xbc3, dt3, conv_state, h0, p, nb):
    n, t, _ = z3.shape
    tail = CONV_K - 1
    blk = lambda w, col: pl.BlockSpec((nb, t, w), lambda i: (i, 0, col))
    cst = lambda w, col: pl.BlockSpec((nb, tail, w), lambda i: (i, 0, col))
    par = lambda r, w: pl.BlockSpec((r, w), lambda i: (0, 0))
    state = pl.BlockSpec((nb, D_SSM, D_STATE), lambda i: (i, 0, 0))
    return pl.pallas_call(
        functools.partial(_ssd_step_body, nb=nb, t=t),
        name="ssd_step",
        grid=(n // nb,),
        in_specs=[
            blk(D_SSM, 0), blk(D_SSM, 0),
            blk(BC_W, D_SSM // BC_W), blk(BC_W, D_SSM // BC_W + 1),
            blk(DT_W, 0),
            cst(D_SSM, 0), cst(BC_W, D_SSM // BC_W), cst(BC_W, D_SSM // BC_W + 1),
            state,
            par(CONV_K, D_SSM), par(CONV_K, BC_W), par(CONV_K, BC_W),
            par(1, D_SSM), par(1, BC_W), par(1, BC_W),
            par(1, DT_W), par(1, DT_W), par(1, D_SSM), par(1, D_SSM),
        ],
        out_specs=[
            blk(D_SSM, 0),
            cst(D_SSM, 0), cst(BC_W, 0), cst(BC_W, 0),
            state,
        ],
        out_shape=[
            jax.ShapeDtypeStruct((n, t, D_SSM), F32),
            jax.ShapeDtypeStruct((n, tail, D_SSM), F32),
            jax.ShapeDtypeStruct((n, tail, BC_W), F32),
            jax.ShapeDtypeStruct((n, tail, BC_W), F32),
            jax.ShapeDtypeStruct((n, D_SSM, D_STATE), F32),
        ],
        scratch_shapes=[
            pltpu.VMEM((nb, SUBLANES + t, D_SSM), F32),
            pltpu.VMEM((nb, SUBLANES + t, BC_W), F32),
            pltpu.VMEM((nb, SUBLANES + t, BC_W), F32),
            pltpu.VMEM((ROWS, D_SSM), F32),
            pltpu.VMEM((ROWS, BC_W), F32),
            pltpu.VMEM((ROWS, D_SSM), F32),
            pltpu.VMEM((D_SSM, ROWS), F32),
            pltpu.VMEM((D_SSM, ROWS), BF16),
        ],
        compiler_params=_cparams(("parallel",)),
    )(z3, xbc3, xbc3, xbc3, dt3, conv_state, conv_state, conv_state, h0,
      p["cwx"], p["cwb"], p["cwc"], p["cbx"], p["cbb"], p["cbc"],
      p["dtb"], p["alog"], p["dskip"], p["ssm_gain"])


def _layer_params(norm_pre, w_in, attn_sink, attn_norm, conv_w, conv_b, dt_bias, a_log, d_skip,
                  ssm_norm, w_out, norm_post):
    w_bf16 = _cast_bf16(w_in, CAST_TM)
    w_tail = jnp.pad(w_bf16[:, 2 * WA_COLS:], ((0, 0), (0, DT_W - N_HEADS_SSM)))
    pad_h = lambda v: jnp.pad(v, (0, DT_W - N_HEADS_SSM)).reshape(1, DT_W)
    return dict(
        norm_pre=norm_pre.reshape(1, D_MODEL), w_in=w_bf16, w_tail=w_tail,
        sink=attn_sink, attn_gain=attn_norm.reshape(1, D_ATTN),
        cwx=conv_w[:, :D_SSM], cwb=conv_w[:, D_SSM:D_SSM + BC_W], cwc=conv_w[:, D_SSM + BC_W:],
        cbx=conv_b[:D_SSM].reshape(1, -1), cbb=conv_b[D_SSM:D_SSM + BC_W].reshape(1, -1),
        cbc=conv_b[D_SSM + BC_W:].reshape(1, -1),
        dtb=pad_h(dt_bias), alog=pad_h(a_log),
        dskip=jnp.broadcast_to(d_skip[:, None], (N_HEADS_SSM, HEAD_DIM)).reshape(1, D_SSM),
        ssm_gain=ssm_norm.reshape(1, D_SSM),
        w_out_packed=_pack_rows(w_out, PACK_TM),
        norm_post=norm_post.reshape(1, D_MODEL),
    )


IN_TM, OUT_TM = 512, 512
ATTN_BLOCKS = 2
SSD_CHUNKS = 2
STEP_NB_ATTN, STEP_NB_SSD = 8, 8


def _layer(x, caches, p):
    n, t, _ = x.shape
    x2d = x.reshape(n * t, D_MODEL)
    act_dtype = BF16 if caches is None else F32
    h, qg, kv = _inproj_attn(x2d, p["norm_pre"], p["w_in"], IN_TM if caches is None else IN_TM // 2, act_dtype)
    qg3, kv3 = qg.reshape(n, t, 2 * D_ATTN), kv.reshape(n, t, 2 * D_KV)
    if caches is None:
        attn, z3, xbc3, dt3 = _attn_prompt(qg3, kv3, h.reshape(n, t, D_MODEL), p["w_in"], p["w_tail"],
                                           p["sink"], p["attn_gain"], ATTN_BLOCKS)
        new_k = kv3[:, t - WINDOW:, :D_KV]
        new_v = kv3[:, t - WINDOW:, D_KV:]
        y, cvx, cvb, cvc, h_new = _ssd_outproj(z3, xbc3, dt3, attn, x, p, SSD_CHUNKS)
    else:
        z, xbc, dt = _inproj_ssm(h, p["w_in"], p["w_tail"], IN_TM, act_dtype)
        z3, xbc3, dt3 = z.reshape(n, t, D_SSM), xbc.reshape(n, t, CONV_DIM), dt.reshape(n, t, DT_W)
        kbuf, vbuf, conv_buf, h0 = caches
        attn, new_k, new_v = _attn_step(qg3, kv3, kbuf.reshape(n, WINDOW, D_KV), vbuf.reshape(n, WINDOW, D_KV),
                                        p["sink"], p["attn_gain"], STEP_NB_ATTN)
        ssm, cvx, cvb, cvc, h_new = _ssd_step(z3, xbc3, dt3, conv_buf, h0.reshape(n, D_SSM, D_STATE), p,
                                              STEP_NB_SSD)
        y = _outproj(attn.reshape(n * t, D_ATTN), ssm.reshape(n * t, D_SSM), x2d,
                     p["w_out_packed"], p["norm_post"], OUT_TM)
    return (y.reshape(n, t, D_MODEL),
            new_k.reshape(n, WINDOW, N_KV_HEADS, HEAD_DIM), new_v.reshape(n, WINDOW, N_KV_HEADS, HEAD_DIM),
            jnp.concatenate([cvx, cvb, cvc], axis=-1),
            h_new.reshape(n, N_HEADS_SSM, HEAD_DIM, D_STATE))


def kernel(x_prompt, x_sample, cache_k, cache_v, state_conv, state_ssm, norm_pre, w_in, attn_sink,
           attn_norm, conv_w, conv_b, dt_bias, a_log, d_skip, ssm_norm, w_out, norm_post):
    depth = w_in.shape[0]
    yp, ys = x_prompt, x_sample
    outs_p, outs_s = [], []
    for l in range(depth):
        p = _layer_params(norm_pre[l], w_in[l], attn_sink[l], attn_norm[l], conv_w[l], conv_b[l],
                          dt_bias[l], a_log[l], d_skip[l], ssm_norm[l], w_out[l], norm_post[l])
        yp, *rest_p = _layer(yp, None, p)
        ys, *rest_s = _layer(ys, (cache_k[l], cache_v[l], state_conv[l], state_ssm[l]), p)
        outs_p.append(rest_p)
        outs_s.append(rest_s)
    stack = lambda outs, i: jnp.stack([o[i] for o in outs])
    return (yp, ys,
            stack(outs_p, 0), stack(outs_p, 1), stack(outs_p, 2), stack(outs_p, 3),
            stack(outs_s, 0), stack(outs_s, 1), stack(outs_s, 2), stack(outs_s, 3))
```
